```python
import jax, jax.numpy as jnp
from jax import lax
import numpy as np

D_MODEL = 2048
BATCH = 2
SEQ = 4096
DEPTH = 1

EPS = 1e-6
N_HEADS = 16
QK_NOPE = 128
QK_ROPE = 64
V_HEAD = 128
Q_LORA = 512
KV_LORA = 512
MLA_WIDTH = N_HEADS * V_HEAD
ROPE_THETA = 10000.0
Q_BLOCK = 128
POOL_WINDOWS = (2, 4, 8, 16)
POOL_GROUPS = len(POOL_WINDOWS)
POOL_GROUP_DIM = 256
POOL_WIDTH = POOL_GROUPS * POOL_GROUP_DIM
IN_SPLITS = (Q_LORA, KV_LORA, QK_ROPE, MLA_WIDTH, POOL_WIDTH, POOL_WIDTH, D_MODEL, D_MODEL)
N_IN = sum(IN_SPLITS)
IN_OFFSETS = tuple(int(v) for v in np.cumsum(IN_SPLITS)[:-1])

kernel_name = "hybrid_mla_pool_gated_encoder_block"


def _rmsnorm(x, g):
    xf = x.astype(jnp.float32)
    y = xf * lax.rsqrt(jnp.mean(xf * xf, axis=-1, keepdims=True) + EPS)
    return y.astype(x.dtype) * g


def _rope_tables(positions, dtype):
    inv_freq = 1.0 / (ROPE_THETA ** (jnp.arange(0, QK_ROPE, 2, dtype=jnp.float32) / QK_ROPE))
    ang = positions.astype(jnp.float32)[..., None] * inv_freq
    return jnp.cos(ang).astype(dtype), jnp.sin(ang).astype(dtype)


def _apply_rope(x, cos, sin):
    x1, x2 = jnp.split(x, 2, axis=-1)
    return jnp.concatenate([x1 * cos - x2 * sin, x2 * cos + x1 * sin], axis=-1)


def _mla_attention(q_nope, q_rope, k_nope, k_rope, v):
    B, S, H, _ = q_nope.shape
    n_blocks = S // Q_BLOCK
    scale = (QK_NOPE + QK_ROPE) ** -0.5

    def block(i):
        start = i * Q_BLOCK
        qn = lax.dynamic_slice_in_dim(q_nope, start, Q_BLOCK, axis=1)
        qr = lax.dynamic_slice_in_dim(q_rope, start, Q_BLOCK, axis=1)
        s = (jnp.einsum('bqhn,bkhn->bhqk', qn, k_nope)
             + jnp.einsum('bqhr,bkr->bhqk', qr, k_rope))
        p = jax.nn.softmax(s.astype(jnp.float32) * scale, axis=-1).astype(v.dtype)
        return jnp.einsum('bhqk,bkhv->bqhv', p, v)

    o = lax.map(block, jnp.arange(n_blocks))
    return jnp.moveaxis(o, 0, 1).reshape(B, S, H * V_HEAD)


def _multiscale_pool(v):
    S = v.shape[1]
    vf = v.astype(jnp.float32)
    cs = jnp.concatenate([jnp.zeros_like(vf[:, :1]), jnp.cumsum(vf, axis=1)], axis=1)
    t = jnp.arange(S)[:, None]
    w = jnp.array(POOL_WINDOWS, dtype=jnp.int32)[None, :]
    lo = jnp.clip(t - w // 2, 0, S)
    hi = jnp.clip(t + w - w // 2, 0, S)
    g = jnp.arange(POOL_GROUPS)[None, :]
    window_sum = cs[:, hi, g] - cs[:, lo, g]
    count = (hi - lo).astype(jnp.float32)[None, :, :, None]
    return (window_sum / count - vf).astype(v.dtype)


def setup_inputs(seed: int = 0) -> dict:
    key = jax.random.key(seed)
    ks = jax.random.split(key, 20)
    f32 = jnp.float32

    def w(k, shape, fan_in):
        return jax.random.normal(k, shape, f32) * (fan_in ** -0.5)

    def gain(k, shape):
        return 1.0 + 0.02 * jax.random.normal(k, shape, f32)

    x = jax.random.normal(ks[0], (BATCH, SEQ, D_MODEL), f32)
    c = jax.random.normal(ks[1], (BATCH, D_MODEL), f32)
    offsets = jax.random.randint(ks[2], (BATCH, 1), 0, 1024, dtype=jnp.int32)
    positions = offsets + jnp.arange(SEQ, dtype=jnp.int32)[None, :]
    return {
        "x": x,
        "c": c,
        "positions": positions,
        "ada_w": w(ks[3], (DEPTH, D_MODEL, 3 * D_MODEL), D_MODEL),
        "ada_b": 0.02 * jax.random.normal(ks[4], (DEPTH, 3 * D_MODEL), f32),
        "norm_g": gain(ks[5], (DEPTH, D_MODEL)),
        "w_in": w(ks[6], (DEPTH, D_MODEL, N_IN), D_MODEL),
        "q_norm_g": gain(ks[7], (DEPTH, Q_LORA)),
        "w_uq": w(ks[8], (DEPTH, Q_LORA, N_HEADS * (QK_NOPE + QK_ROPE)), Q_LORA),
        "kv_norm_g": gain(ks[9], (DEPTH, KV_LORA)),
        "w_ukv": w(ks[10], (DEPTH, KV_LORA, N_HEADS * (QK_NOPE + V_HEAD)), KV_LORA),
        "w_o_mla": w(ks[11], (DEPTH, MLA_WIDTH, D_MODEL), MLA_WIDTH),
        "pool_w": w(ks[12], (DEPTH, POOL_GROUPS, POOL_GROUP_DIM, POOL_GROUP_DIM), POOL_GROUP_DIM),
        "pool_scale": 1.0 + 0.1 * jax.random.normal(ks[13], (DEPTH, POOL_WIDTH), f32),
        "w_o_pool": w(ks[14], (DEPTH, POOL_WIDTH, D_MODEL), POOL_WIDTH),
        "w_out": w(ks[15], (DEPTH, D_MODEL, D_MODEL), D_MODEL),
        "final_g": gain(ks[16], (D_MODEL,)),
    }


def reference(x, c, positions, ada_w, ada_b, norm_g, w_in, q_norm_g, w_uq, kv_norm_g,
              w_ukv, w_o_mla, pool_w, pool_scale, w_o_pool, w_out, final_g):
    B, S, D = x.shape
    cos, sin = _rope_tables(positions, x.dtype)
    c_act = jax.nn.silu(c)
    for l in range(DEPTH):
        mod = c_act @ ada_w[l] + ada_b[l]
        shift, scale, gate = jnp.split(mod, 3, axis=-1)
        h = _rmsnorm(x, norm_g[l]) * (1.0 + scale[:, None, :]) + shift[:, None, :]

        z = h @ w_in[l]
        c_q, c_kv, k_rope, g_mla, v_pool, g_pool, m_mla, m_pool = jnp.split(z, IN_OFFSETS, axis=-1)

        q = (_rmsnorm(c_q, q_norm_g[l]) @ w_uq[l]).reshape(B, S, N_HEADS, QK_NOPE + QK_ROPE)
        q_nope, q_rope = q[..., :QK_NOPE], q[..., QK_NOPE:]
        q_rope = _apply_rope(q_rope, cos[:, :, None, :], sin[:, :, None, :])
        kv = (_rmsnorm(c_kv, kv_norm_g[l]) @ w_ukv[l]).reshape(B, S, N_HEADS, QK_NOPE + V_HEAD)
        k_nope, v = kv[..., :QK_NOPE], kv[..., QK_NOPE:]
        k_rope = _apply_rope(k_rope, cos, sin)
        attn = _mla_attention(q_nope, q_rope, k_nope, k_rope, v)
        p_mla = (attn * jax.nn.silu(g_mla)) @ w_o_mla[l]

        vp = v_pool.reshape(B, S, POOL_GROUPS, POOL_GROUP_DIM)
        pooled = _multiscale_pool(vp)
        mixed = jnp.einsum('bsgc,gcd->bsgd', pooled, pool_w[l]).reshape(B, S, POOL_WIDTH)
        p_pool = (mixed * pool_scale[l] * jax.nn.silu(g_pool)) @ w_o_pool[l]

        y = jax.nn.sigmoid(m_mla) * p_mla + jax.nn.sigmoid(m_pool) * p_pool
        x = x + gate[:, None, :] * (y @ w_out[l])
    return _rmsnorm(x, final_g)
```

```python
import functools
import math

import jax
import jax.numpy as jnp
from jax import lax
from jax.experimental import pallas as pl
from jax.experimental.pallas import tpu as pltpu

EPS = 1e-6
N_HEADS = 16
QK_NOPE = 128
QK_ROPE = 64
QK_HEAD = QK_NOPE + QK_ROPE
V_HEAD = 128
Q_LORA = 512
KV_LORA = 512
MLA_WIDTH = N_HEADS * V_HEAD
ROPE_THETA = 10000.0
POOL_WINDOWS = (2, 4, 8, 16)
POOL_GROUPS = len(POOL_WINDOWS)
POOL_GROUP_DIM = 256
POOL_WIDTH = POOL_GROUPS * POOL_GROUP_DIM
POOL_HALO = 16
SMALL_WIDTH = Q_LORA + KV_LORA + 2 * QK_ROPE

V7X_VMEM_LIMIT = 56 * 1024 * 1024

F32 = jnp.float32
BF16 = jnp.bfloat16
NT_DIMS = (((1,), (1,)), ((), ()))


def _sigmoid(v):
    return 1.0 / (1.0 + jnp.exp(-v))


def _params(semantics, vmem=V7X_VMEM_LIMIT):
    return pltpu.CompilerParams(dimension_semantics=semantics, vmem_limit_bytes=vmem)


def _adaln_kernel(ct_ref, w_ref, b_ref, o_ref):
    w = w_ref[...]
    for b in range(ct_ref.shape[1]):
        cb = ct_ref[:, b:b + 1]
        act = cb * _sigmoid(cb)
        o_ref[b:b + 1, :] = jnp.sum(w * act, axis=0, keepdims=True) + b_ref[...]


def _adaln(c, w, bias):
    B, D = c.shape
    n = w.shape[1]
    tn = 512
    return pl.pallas_call(
        _adaln_kernel,
        grid=(n // tn,),
        in_specs=[pl.BlockSpec((D, B), lambda j: (0, 0)),
                  pl.BlockSpec((D, tn), lambda j: (0, j)),
                  pl.BlockSpec((1, tn), lambda j: (0, j))],
        out_specs=pl.BlockSpec((B, tn), lambda j: (0, j)),
        out_shape=jax.ShapeDtypeStruct((B, n), F32),
        compiler_params=_params(("arbitrary",)),
        name="adaln",
    )(c.T, w, bias.reshape(1, n))


def _norm_proj_kernel(x_ref, shift_ref, scale_ref, g_ref, ws_ref, h_ref, zs_ref):
    x = x_ref[...]
    y = x * lax.rsqrt(jnp.mean(x * x, axis=-1, keepdims=True) + EPS) * g_ref[...]
    h = (y * (1.0 + scale_ref[...]) + shift_ref[...]).astype(BF16)
    h_ref[...] = h
    zs_ref[...] = jnp.dot(h, ws_ref[...], preferred_element_type=F32)


def _norm_proj(xt, mod4, norm_g, w_small, seq):
    T, D = xt.shape
    tm = 512
    per_b = seq // tm
    return pl.pallas_call(
        _norm_proj_kernel,
        grid=(T // tm,),
        in_specs=[pl.BlockSpec((tm, D), lambda i: (i, 0)),
                  pl.BlockSpec((None, None, 1, D), lambda i: (i // per_b, 0, 0, 0)),
                  pl.BlockSpec((None, None, 1, D), lambda i: (i // per_b, 1, 0, 0)),
                  pl.BlockSpec((1, D), lambda i: (0, 0)),
                  pl.BlockSpec((D, SMALL_WIDTH), lambda i: (0, 0))],
        out_specs=[pl.BlockSpec((tm, D), lambda i: (i, 0)),
                   pl.BlockSpec((tm, SMALL_WIDTH), lambda i: (i, 0))],
        out_shape=[jax.ShapeDtypeStruct((T, D), BF16),
                   jax.ShapeDtypeStruct((T, SMALL_WIDTH), F32)],
        compiler_params=_params(("arbitrary",)),
        name="norm_proj",
    )(xt, mod4, mod4, norm_g.reshape(1, D), w_small)


def _gate_proj_kernel(h_ref, w_ref, o_ref, *, silu_tiles, identity_tiles):
    j = pl.program_id(1)
    acc = jnp.dot(h_ref[...], w_ref[...], preferred_element_type=F32)

    @pl.when(j < silu_tiles)
    def _():
        o_ref[...] = (acc * _sigmoid(acc)).astype(o_ref.dtype)

    @pl.when(jnp.logical_and(j >= silu_tiles, j < silu_tiles + identity_tiles))
    def _():
        o_ref[...] = acc.astype(o_ref.dtype)

    @pl.when(j >= silu_tiles + identity_tiles)
    def _():
        o_ref[...] = _sigmoid(acc).astype(o_ref.dtype)


def _gate_proj(h, w_big):
    T, D = h.shape
    n = w_big.shape[1]
    tm, tn = 1024, 1024
    kern = functools.partial(_gate_proj_kernel,
                             silu_tiles=(MLA_WIDTH + POOL_WIDTH) // tn,
                             identity_tiles=POOL_WIDTH // tn)
    return pl.pallas_call(
        kern,
        grid=(T // tm, n // tn),
        in_specs=[pl.BlockSpec((tm, D), lambda i, j: (i, 0)),
                  pl.BlockSpec((D, tn), lambda i, j: (0, j))],
        out_specs=pl.BlockSpec((tm, tn), lambda i, j: (i, j)),
        out_shape=jax.ShapeDtypeStruct((T, n), BF16),
        compiler_params=_params(("arbitrary", "arbitrary")),
        name="gate_proj",
    )(h, w_big)


def _mla_prep_kernel(zs_ref, posr_ref, posc_ref, invr_ref, invc_ref, qg_ref, kvg_ref,
                     wq_ref, wk_ref, wv_ref, qt_ref, k_ref, vt_ref, *, q_scale):
    def rms(v, g):
        return (v * lax.rsqrt(jnp.mean(v * v, axis=-1, keepdims=True) + EPS) * g).astype(BF16)

    cqn = rms(zs_ref[:, 0:Q_LORA], qg_ref[...])
    ckvn = rms(zs_ref[:, Q_LORA:Q_LORA + KV_LORA], kvg_ref[...])
    kr = zs_ref[:, Q_LORA + KV_LORA:Q_LORA + KV_LORA + QK_ROPE]
    kr_sw = zs_ref[:, Q_LORA + KV_LORA + QK_ROPE:SMALL_WIDTH]

    ang_t = invc_ref[...] * posr_ref[...].astype(F32)
    cos_t, sin_t = jnp.cos(ang_t), jnp.sin(ang_t)
    ang = posc_ref[...].astype(F32) * invr_ref[...]
    cos, sin = jnp.cos(ang), jnp.sin(ang)

    qf = lax.dot_general(wq_ref[...], cqn, NT_DIMS, preferred_element_type=F32)
    rope0 = N_HEADS * QK_NOPE
    swap0 = rope0 + N_HEADS * QK_ROPE
    for h in range(N_HEADS):
        qt_ref[h * QK_HEAD:h * QK_HEAD + QK_NOPE, :] = (
            qf[h * QK_NOPE:(h + 1) * QK_NOPE] * q_scale).astype(BF16)
        rot = (qf[rope0 + h * QK_ROPE:rope0 + (h + 1) * QK_ROPE] * cos_t
               + qf[swap0 + h * QK_ROPE:swap0 + (h + 1) * QK_ROPE] * sin_t)
        qt_ref[h * QK_HEAD + QK_NOPE:(h + 1) * QK_HEAD, :] = (rot * q_scale).astype(BF16)

    kn = jnp.dot(ckvn, wk_ref[...], preferred_element_type=F32)
    k_rot = (kr * cos + kr_sw * sin).astype(BF16)
    for h in range(N_HEADS):
        k_ref[h, :, 0:QK_NOPE] = kn[:, h * QK_NOPE:(h + 1) * QK_NOPE].astype(BF16)
        k_ref[h, :, QK_NOPE:QK_HEAD] = k_rot

    vt_ref[...] = lax.dot_general(wv_ref[...], ckvn, NT_DIMS,
                                  preferred_element_type=F32).astype(BF16)


def _mla_prep(zs, posr, posc, inv_signed, q_norm_g, kv_norm_g, wq_t, wk, wv_t, batch, seq, q_scale):
    tm = 256
    per_b = seq // tm
    const = lambda i: (0, 0)
    kern = functools.partial(_mla_prep_kernel, q_scale=q_scale)
    return pl.pallas_call(
        kern,
        grid=(batch * per_b,),
        in_specs=[pl.BlockSpec((tm, SMALL_WIDTH), lambda i: (i, 0)),
                  pl.BlockSpec((None, 1, tm), lambda i: (i // per_b, 0, i % per_b)),
                  pl.BlockSpec((None, tm, 1), lambda i: (i // per_b, i % per_b, 0)),
                  pl.BlockSpec((1, QK_ROPE), const),
                  pl.BlockSpec((QK_ROPE, 1), const),
                  pl.BlockSpec((1, Q_LORA), const),
                  pl.BlockSpec((1, KV_LORA), const),
                  pl.BlockSpec(wq_t.shape, const),
                  pl.BlockSpec(wk.shape, const),
                  pl.BlockSpec(wv_t.shape, const)],
        out_specs=[pl.BlockSpec((None, N_HEADS * QK_HEAD, tm), lambda i: (i // per_b, 0, i % per_b)),
                   pl.BlockSpec((None, N_HEADS, tm, QK_HEAD), lambda i: (i // per_b, 0, i % per_b, 0)),
                   pl.BlockSpec((None, MLA_WIDTH, tm), lambda i: (i // per_b, 0, i % per_b))],
        out_shape=[jax.ShapeDtypeStruct((batch, N_HEADS * QK_HEAD, seq), BF16),
                   jax.ShapeDtypeStruct((batch, N_HEADS, seq, QK_HEAD), BF16),
                   jax.ShapeDtypeStruct((batch, MLA_WIDTH, seq), BF16)],
        compiler_params=_params(("arbitrary",)),
        name="mla_prep",
    )(zs, posr, posc, inv_signed.reshape(1, QK_ROPE), inv_signed.reshape(QK_ROPE, 1),
      q_norm_g.reshape(1, Q_LORA), kv_norm_g.reshape(1, KV_LORA), wq_t, wk, wv_t)


def _attention_kernel(qt_ref, k_ref, vt_ref, o_ref):
    s = jnp.dot(k_ref[...], qt_ref[...], preferred_element_type=F32)
    m = jnp.max(s, axis=0, keepdims=True)
    p = jnp.exp2(s - m)
    l = jnp.sum(p, axis=0, keepdims=True)
    ot = jnp.dot(vt_ref[...], p.astype(BF16), preferred_element_type=F32)
    o_ref[...] = (ot / l).T.astype(o_ref.dtype)


def _attention(qt, k, vt, batch, seq):
    tq = 256
    return pl.pallas_call(
        _attention_kernel,
        grid=(batch, N_HEADS, seq // tq),
        in_specs=[pl.BlockSpec((None, QK_HEAD, tq), lambda b, h, i: (b, h, i)),
                  pl.BlockSpec((None, None, seq, QK_HEAD), lambda b, h, i: (b, h, 0, 0)),
                  pl.BlockSpec((None, V_HEAD, seq), lambda b, h, i: (b, h, 0))],
        out_specs=pl.BlockSpec((None, tq, V_HEAD), lambda b, h, i: (b, i, h)),
        out_shape=jax.ShapeDtypeStruct((batch, seq, MLA_WIDTH), BF16),
        compiler_params=_params(("arbitrary", "arbitrary", "arbitrary")),
        name="attention",
    )(qt, k, vt)


def _merge_out_kernel(x_ref, gate_ref, attn_ref, gm_ref, gp_ref, vp_ref, vprev_ref, vnext_ref,
                      mm_ref, mp_ref, pw_ref, ps_ref, wop_ref, wom_ref, wout_ref, fg_ref, o_ref,
                      *, seq, final_norm):
    tm = x_ref.shape[0]
    t0 = (pl.program_id(0) % (seq // tm)) * tm

    cur = vp_ref[...].astype(F32)
    prev = jnp.where(t0 > 0, vprev_ref[...].astype(F32), 0.0)
    nxt = jnp.where(t0 + tm < seq, vnext_ref[...].astype(F32), 0.0)
    ext = jnp.concatenate([prev, cur, nxt], axis=0)
    n_ext = tm + 2 * POOL_HALO
    tok = t0 + lax.broadcasted_iota(jnp.int32, (tm, 1), 0)
    mixed = []
    for g, w in enumerate(POOL_WINDOWS):
        acc = ext[:, g * POOL_GROUP_DIM:(g + 1) * POOL_GROUP_DIM]
        acc = acc + pltpu.roll(acc, 1, axis=0)
        half = 1
        while 2 * half < w:
            acc = pltpu.roll(acc, half, axis=0) + pltpu.roll(acc, n_ext - half, axis=0)
            half *= 2
        wsum = acc[POOL_HALO:POOL_HALO + tm]
        count = (jnp.minimum(tok + w // 2, seq) - jnp.maximum(tok - w // 2, 0)).astype(F32)
        pooled = wsum / count - cur[:, g * POOL_GROUP_DIM:(g + 1) * POOL_GROUP_DIM]
        mixed.append(jnp.dot(pooled.astype(BF16), pw_ref[g], preferred_element_type=F32))
    mixed = jnp.concatenate(mixed, axis=1)
    u = (mixed * ps_ref[...] * gp_ref[...].astype(F32)).astype(BF16)
    p_pool = jnp.dot(u, wop_ref[...], preferred_element_type=F32)

    gated = (attn_ref[...].astype(F32) * gm_ref[...].astype(F32)).astype(BF16)
    p_mla = jnp.dot(gated, wom_ref[...], preferred_element_type=F32)

    y = mm_ref[...].astype(F32) * p_mla + mp_ref[...].astype(F32) * p_pool
    r = jnp.dot(y.astype(BF16), wout_ref[...], preferred_element_type=F32)
    xo = x_ref[...] + gate_ref[...] * r
    if final_norm:
        xo = xo * lax.rsqrt(jnp.mean(xo * xo, axis=-1, keepdims=True) + EPS) * fg_ref[...]
    o_ref[...] = xo


def _merge_out(xt, mod4, attn, zbig, pool_w, pool_scale, w_o_pool, w_o_mla, w_out, final_g,
               seq, final_norm):
    T, D = xt.shape
    tm = 256
    per_b = seq // tm
    halo_per_tile = tm // POOL_HALO
    n_halo = T // POOL_HALO
    gp_blk = MLA_WIDTH // POOL_WIDTH
    vp_blk = gp_blk + 1
    mm_blk = (MLA_WIDTH + 2 * POOL_WIDTH) // D
    resident = functools.partial(pl.BlockSpec, pipeline_mode=pl.Buffered(1))
    kern = functools.partial(_merge_out_kernel, seq=seq, final_norm=final_norm)
    return pl.pallas_call(
        kern,
        grid=(T // tm,),
        in_specs=[pl.BlockSpec((tm, D), lambda i: (i, 0)),
                  pl.BlockSpec((None, None, 1, D), lambda i: (i // per_b, 2, 0, 0)),
                  pl.BlockSpec((tm, MLA_WIDTH), lambda i: (i, 0)),
                  pl.BlockSpec((tm, MLA_WIDTH), lambda i: (i, 0)),
                  pl.BlockSpec((tm, POOL_WIDTH), lambda i: (i, gp_blk)),
                  pl.BlockSpec((tm, POOL_WIDTH), lambda i: (i, vp_blk)),
                  pl.BlockSpec((POOL_HALO, POOL_WIDTH),
                               lambda i: (jnp.maximum(i * halo_per_tile - 1, 0), vp_blk)),
                  pl.BlockSpec((POOL_HALO, POOL_WIDTH),
                               lambda i: (jnp.minimum((i + 1) * halo_per_tile, n_halo - 1), vp_blk)),
                  pl.BlockSpec((tm, D), lambda i: (i, mm_blk)),
                  pl.BlockSpec((tm, D), lambda i: (i, mm_blk + 1)),
                  resident(pool_w.shape, lambda i: (0, 0, 0)),
                  resident((1, POOL_WIDTH), lambda i: (0, 0)),
                  resident(w_o_pool.shape, lambda i: (0, 0)),
                  resident(w_o_mla.shape, lambda i: (0, 0)),
                  resident(w_out.shape, lambda i: (0, 0)),
                  resident((1, D), lambda i: (0, 0))],
        out_specs=pl.BlockSpec((tm, D), lambda i: (i, 0)),
        out_shape=jax.ShapeDtypeStruct((T, D), F32),
        compiler_params=_params(("arbitrary",)),
        name="merge_out",
    )(xt, mod4, attn, zbig, zbig, zbig, zbig, zbig, zbig, zbig,
      pool_w, pool_scale.reshape(1, POOL_WIDTH), w_o_pool, w_o_mla, w_out, final_g.reshape(1, D))


def _split_in_proj(w_in):
    o = 0
    cq = w_in[:, o:o + Q_LORA]; o += Q_LORA
    ckv = w_in[:, o:o + KV_LORA]; o += KV_LORA
    kr = w_in[:, o:o + QK_ROPE]; o += QK_ROPE
    g_mla = w_in[:, o:o + MLA_WIDTH]; o += MLA_WIDTH
    v_pool = w_in[:, o:o + POOL_WIDTH]; o += POOL_WIDTH
    g_pool = w_in[:, o:o + POOL_WIDTH]; o += POOL_WIDTH
    m_mla = w_in[:, o:o + MLA_WIDTH]; o += MLA_WIDTH
    m_pool = w_in[:, o:]
    kr_sw = jnp.concatenate([kr[:, QK_ROPE // 2:], kr[:, :QK_ROPE // 2]], axis=1)
    w_small = jnp.concatenate([cq, ckv, kr, kr_sw], axis=1).astype(BF16)
    w_big = jnp.concatenate([g_mla, g_pool, v_pool, m_mla, m_pool], axis=1).astype(BF16)
    return w_small, w_big


def _split_up_proj(w_uq, w_ukv):
    wq = w_uq.reshape(Q_LORA, N_HEADS, QK_HEAD)
    nope = wq[:, :, :QK_NOPE].reshape(Q_LORA, N_HEADS * QK_NOPE)
    rope = wq[:, :, QK_NOPE:]
    rope_sw = jnp.concatenate([rope[:, :, QK_ROPE // 2:], rope[:, :, :QK_ROPE // 2]], axis=2)
    wq_t = jnp.concatenate([nope, rope.reshape(Q_LORA, -1), rope_sw.reshape(Q_LORA, -1)], axis=1).T
    wkv = w_ukv.reshape(KV_LORA, N_HEADS, QK_NOPE + V_HEAD)
    wk = wkv[:, :, :QK_NOPE].reshape(KV_LORA, N_HEADS * QK_NOPE)
    wv_t = wkv[:, :, QK_NOPE:].reshape(KV_LORA, MLA_WIDTH).T
    return wq_t.astype(BF16), wk.astype(BF16), wv_t.astype(BF16)


def kernel(x, c, positions, ada_w, ada_b, norm_g, w_in, q_norm_g, w_uq, kv_norm_g, w_ukv, w_o_mla,
           pool_w, pool_scale, w_o_pool, w_out, final_g):
    B, S, D = x.shape
    depth = ada_w.shape[0]
    inv_freq = 1.0 / (ROPE_THETA ** (jnp.arange(0, QK_ROPE, 2, dtype=F32) / QK_ROPE))
    inv_signed = jnp.concatenate([-inv_freq, inv_freq])
    posr = positions.reshape(B, 1, S)
    posc = positions.reshape(B, S, 1)
    q_scale = QK_HEAD ** -0.5 * math.log2(math.e)

    xt = x.reshape(B * S, D)
    for l in range(depth):
        mod4 = _adaln(c, ada_w[l], ada_b[l]).reshape(B, 3, 1, D)
        w_small, w_big = _split_in_proj(w_in[l])
        wq_t, wk, wv_t = _split_up_proj(w_uq[l], w_ukv[l])

        h, zs = _norm_proj(xt, mod4, norm_g[l], w_small, S)
        zbig = _gate_proj(h, w_big)
        qt, k, vt = _mla_prep(zs, posr, posc, inv_signed, q_norm_g[l], kv_norm_g[l],
                              wq_t, wk, wv_t, B, S, q_scale)
        attn = _attention(qt, k, vt, B, S).reshape(B * S, MLA_WIDTH)
        xt = _merge_out(xt, mod4, attn, zbig, pool_w[l].astype(BF16), pool_scale[l],
                        w_o_pool[l].astype(BF16), w_o_mla[l].astype(BF16), w_out[l].astype(BF16),
                        final_g, S, final_norm=(l == depth - 1))
    return xt.reshape(B, S, D)
```

```python
import functools
import math

import jax
import jax.numpy as jnp
from jax import lax
from jax.experimental import pallas as pl
from jax.experimental.pallas import tpu as pltpu

EPS = 1e-6
N_HEADS = 16
QK_NOPE = 128
QK_ROPE = 64
QK_HEAD = QK_NOPE + QK_ROPE
V_HEAD = 128
Q_LORA = 512
KV_LORA = 512
MLA_WIDTH = N_HEADS * V_HEAD
ROPE_THETA = 10000.0
POOL_WINDOWS = (2, 4, 8, 16)
POOL_GROUPS = len(POOL_WINDOWS)
POOL_GROUP_DIM = 256
POOL_WIDTH = POOL_GROUPS * POOL_GROUP_DIM
POOL_HALO = 16
SMALL_WIDTH = Q_LORA + KV_LORA + 2 * QK_ROPE

V7X_VMEM_LIMIT = 56 * 1024 * 1024

F32 = jnp.float32
BF16 = jnp.bfloat16
NT_DIMS = (((1,), (1,)), ((), ()))


def _sigmoid(v):
    return 1.0 / (1.0 + jnp.exp(-v))


def _params(semantics, vmem=V7X_VMEM_LIMIT):
    return pltpu.CompilerParams(dimension_semantics=semantics, vmem_limit_bytes=vmem)


def _adaln_kernel(ct_ref, w_ref, b_ref, o_ref):
    w = w_ref[...]
    for b in range(ct_ref.shape[1]):
        cb = ct_ref[:, b:b + 1]
        act = cb * _sigmoid(cb)
        o_ref[b:b + 1, :] = jnp.sum(w * act, axis=0, keepdims=True) + b_ref[...]


def _adaln(c, w, bias):
    B, D = c.shape
    n = w.shape[1]
    tn = 512
    return pl.pallas_call(
        _adaln_kernel,
        grid=(n // tn,),
        in_specs=[pl.BlockSpec((D, B), lambda j: (0, 0)),
                  pl.BlockSpec((D, tn), lambda j: (0, j)),
                  pl.BlockSpec((1, tn), lambda j: (0, j))],
        out_specs=pl.BlockSpec((B, tn), lambda j: (0, j)),
        out_shape=jax.ShapeDtypeStruct((B, n), F32),
        compiler_params=_params(("arbitrary",)),
        name="adaln",
    )(c.T, w, bias.reshape(1, n))


def _norm_proj_kernel(x_ref, shift_ref, scale_ref, g_ref, ws_ref, h_ref, zs_ref):
    x = x_ref[...]
    y = x * lax.rsqrt(jnp.mean(x * x, axis=-1, keepdims=True) + EPS) * g_ref[...]
    h = (y * (1.0 + scale_ref[...]) + shift_ref[...]).astype(BF16)
    h_ref[...] = h
    zs_ref[...] = jnp.dot(h, ws_ref[...], preferred_element_type=F32)


def _norm_proj(xt, mod4, norm_g, w_small, seq):
    T, D = xt.shape
    tm = 512
    per_b = seq // tm
    return pl.pallas_call(
        _norm_proj_kernel,
        grid=(T // tm,),
        in_specs=[pl.BlockSpec((tm, D), lambda i: (i, 0)),
                  pl.BlockSpec((None, None, 1, D), lambda i: (i // per_b, 0, 0, 0)),
                  pl.BlockSpec((None, None, 1, D), lambda i: (i // per_b, 1, 0, 0)),
                  pl.BlockSpec((1, D), lambda i: (0, 0)),
                  pl.BlockSpec((D, SMALL_WIDTH), lambda i: (0, 0))],
        out_specs=[pl.BlockSpec((tm, D), lambda i: (i, 0)),
                   pl.BlockSpec((tm, SMALL_WIDTH), lambda i: (i, 0))],
        out_shape=[jax.ShapeDtypeStruct((T, D), BF16),
                   jax.ShapeDtypeStruct((T, SMALL_WIDTH), F32)],
        compiler_params=_params(("arbitrary",)),
        name="norm_proj",
    )(xt, mod4, mod4, norm_g.reshape(1, D), w_small)


def _gate_proj_kernel(h_ref, w_ref, o_ref, *, silu_tiles, identity_tiles):
    j = pl.program_id(1)
    acc = jnp.dot(h_ref[...], w_ref[...], preferred_element_type=F32)

    @pl.when(j < silu_tiles)
    def _():
        o_ref[...] = (acc * _sigmoid(acc)).astype(o_ref.dtype)

    @pl.when(jnp.logical_and(j >= silu_tiles, j < silu_tiles + identity_tiles))
    def _():
        o_ref[...] = acc.astype(o_ref.dtype)

    @pl.when(j >= silu_tiles + identity_tiles)
    def _():
        o_ref[...] = _sigmoid(acc).astype(o_ref.dtype)


def _gate_proj(h, w_big):
    T, D = h.shape
    n = w_big.shape[1]
    tm, tn = 1024, 1024
    kern = functools.partial(_gate_proj_kernel,
                             silu_tiles=(MLA_WIDTH + POOL_WIDTH) // tn,
                             identity_tiles=POOL_WIDTH // tn)
    return pl.pallas_call(
        kern,
        grid=(T // tm, n // tn),
        in_specs=[pl.BlockSpec((tm, D), lambda i, j: (i, 0)),
                  pl.BlockSpec((D, tn), lambda i, j: (0, j))],
        out_specs=pl.BlockSpec((tm, tn), lambda i, j: (i, j)),
        out_shape=jax.ShapeDtypeStruct((T, n), BF16),
        compiler_params=_params(("arbitrary", "arbitrary")),
        name="gate_proj",
    )(h, w_big)


def _mla_prep_kernel(zs_ref, posr_ref, posc_ref, invr_ref, invc_ref, qg_ref, kvg_ref,
                     wq_ref, wk_ref, wv_ref, qt_ref, k_ref, vt_ref, *, q_scale):
    def rms(v, g):
        return (v * lax.rsqrt(jnp.mean(v * v, axis=-1, keepdims=True) + EPS) * g).astype(BF16)

    cqn = rms(zs_ref[:, 0:Q_LORA], qg_ref[...])
    ckvn = rms(zs_ref[:, Q_LORA:Q_LORA + KV_LORA], kvg_ref[...])
    kr = zs_ref[:, Q_LORA + KV_LORA:Q_LORA + KV_LORA + QK_ROPE]
    kr_sw = zs_ref[:, Q_LORA + KV_LORA + QK_ROPE:SMALL_WIDTH]

    ang_t = invc_ref[...] * posr_ref[...].astype(F32)
    cos_t, sin_t = jnp.cos(ang_t), jnp.sin(ang_t)
    ang = posc_ref[...].astype(F32) * invr_ref[...]
    cos, sin = jnp.cos(ang), jnp.sin(ang)

    qf = lax.dot_general(wq_ref[...], cqn, NT_DIMS, preferred_element_type=F32)
    rope0 = N_HEADS * QK_NOPE
    swap0 = rope0 + N_HEADS * QK_ROPE
    for h in range(N_HEADS):
        qt_ref[h * QK_HEAD:h * QK_HEAD + QK_NOPE, :] = (
            qf[h * QK_NOPE:(h + 1) * QK_NOPE] * q_scale).astype(BF16)
        rot = (qf[rope0 + h * QK_ROPE:rope0 + (h + 1) * QK_ROPE] * cos_t
               + qf[swap0 + h * QK_ROPE:swap0 + (h + 1) * QK_ROPE] * sin_t)
        qt_ref[h * QK_HEAD + QK_NOPE:(h + 1) * QK_HEAD, :] = (rot * q_scale).astype(BF16)

    kn = jnp.dot(ckvn, wk_ref[...], preferred_element_type=F32)
    k_rot = (kr * cos + kr_sw * sin).astype(BF16)
    for h in range(N_HEADS):
        k_ref[h, :, 0:QK_NOPE] = kn[:, h * QK_NOPE:(h + 1) * QK_NOPE].astype(BF16)
        k_ref[h, :, QK_NOPE:QK_HEAD] = k_rot

    vt_ref[...] = lax.dot_general(wv_ref[...], ckvn, NT_DIMS,
                                  preferred_element_type=F32).astype(BF16)


def _mla_prep(zs, posr, posc, inv_signed, q_norm_g, kv_norm_g, wq_t, wk, wv_t, batch, seq, q_scale):
    tm = 256
    per_b = seq // tm
    const = lambda i: (0, 0)
    kern = functools.partial(_mla_prep_kernel, q_scale=q_scale)
    return pl.pallas_call(
        kern,
        grid=(batch * per_b,),
        in_specs=[pl.BlockSpec((tm, SMALL_WIDTH), lambda i: (i, 0)),
                  pl.BlockSpec((None, 1, tm), lambda i: (i // per_b, 0, i % per_b)),
                  pl.BlockSpec((None, tm, 1), lambda i: (i // per_b, i % per_b, 0)),
                  pl.BlockSpec((1, QK_ROPE), const),
                  pl.BlockSpec((QK_ROPE, 1), const),
                  pl.BlockSpec((1, Q_LORA), const),
                  pl.BlockSpec((1, KV_LORA), const),
                  pl.BlockSpec(wq_t.shape, const),
                  pl.BlockSpec(wk.shape, const),
                  pl.BlockSpec(wv_t.shape, const)],
        out_specs=[pl.BlockSpec((None, N_HEADS * QK_HEAD, tm), lambda i: (i // per_b, 0, i % per_b)),
                   pl.BlockSpec((None, N_HEADS, tm, QK_HEAD), lambda i: (i // per_b, 0, i % per_b, 0)),
                   pl.BlockSpec((None, MLA_WIDTH, tm), lambda i: (i // per_b, 0, i % per_b))],
        out_shape=[jax.ShapeDtypeStruct((batch, N_HEADS * QK_HEAD, seq), BF16),
                   jax.ShapeDtypeStruct((batch, N_HEADS, seq, QK_HEAD), BF16),
                   jax.ShapeDtypeStruct((batch, MLA_WIDTH, seq), BF16)],
        compiler_params=_params(("arbitrary",)),
        name="mla_prep",
    )(zs, posr, posc, inv_signed.reshape(1, QK_ROPE), inv_signed.reshape(QK_ROPE, 1),
      q_norm_g.reshape(1, Q_LORA), kv_norm_g.reshape(1, KV_LORA), wq_t, wk, wv_t)


def _attention_kernel(qt_ref, k_ref, vt_ref, o_ref, s_a, m_a, s_b, m_b):
    t = pl.program_id(0)
    n_tiles = pl.num_programs(0) - 1

    def scores(s_ref, m_ref):
        s = jnp.dot(k_ref[...], qt_ref[...], preferred_element_type=F32)
        s_ref[...] = s
        m_ref[...] = jnp.max(s, axis=0, keepdims=True)

    def finish(s_ref, m_ref):
        p = jnp.exp2(s_ref[...] - m_ref[...])
        l = jnp.sum(p, axis=0, keepdims=True)
        ot = jnp.dot(vt_ref[...], p.astype(BF16), preferred_element_type=F32)
        o_ref[...] = (ot / l).T.astype(o_ref.dtype)

    steady = jnp.logical_and(t > 0, t < n_tiles)
    even = t % 2 == 0

    @pl.when(t == 0)
    def _():
        scores(s_a, m_a)

    @pl.when(jnp.logical_and(steady, even))
    def _():
        scores(s_a, m_a)
        finish(s_b, m_b)

    @pl.when(jnp.logical_and(steady, jnp.logical_not(even)))
    def _():
        scores(s_b, m_b)
        finish(s_a, m_a)

    @pl.when(jnp.logical_and(t == n_tiles, even))
    def _():
        finish(s_b, m_b)

    @pl.when(jnp.logical_and(t == n_tiles, jnp.logical_not(even)))
    def _():
        finish(s_a, m_a)


def _attention(qt, k, vt, batch, seq):
    tq = 256
    nq = seq // tq
    n_tiles = batch * N_HEADS * nq

    def tile(t):
        return t // (N_HEADS * nq), (t // nq) % N_HEADS, t % nq

    def cur(t):
        return tile(jnp.minimum(t, n_tiles - 1))

    def prev(t):
        return tile(jnp.maximum(t - 1, 0))

    return pl.pallas_call(
        _attention_kernel,
        grid=(n_tiles + 1,),
        in_specs=[pl.BlockSpec((None, QK_HEAD, tq), lambda t: cur(t)),
                  pl.BlockSpec((None, None, seq, QK_HEAD), lambda t: cur(t)[:2] + (0, 0)),
                  pl.BlockSpec((None, V_HEAD, seq), lambda t: prev(t)[:2] + (0,))],
        out_specs=pl.BlockSpec((None, tq, V_HEAD), lambda t: (prev(t)[0], prev(t)[2], prev(t)[1])),
        out_shape=jax.ShapeDtypeStruct((batch, seq, MLA_WIDTH), BF16),
        scratch_shapes=[pltpu.VMEM((seq, tq), F32), pltpu.VMEM((1, tq), F32),
                        pltpu.VMEM((seq, tq), F32), pltpu.VMEM((1, tq), F32)],
        compiler_params=_params(("arbitrary",)),
        name="attention",
    )(qt, k, vt)


def _merge_out_kernel(x_ref, gate_ref, attn_ref, gm_ref, gp_ref, vp_ref, vprev_ref, vnext_ref,
                      mm_ref, mp_ref, pw_ref, ps_ref, wop_ref, wom_ref, wout_ref, fg_ref, o_ref,
                      *, seq, final_norm):
    tm = x_ref.shape[0]
    t0 = (pl.program_id(0) % (seq // tm)) * tm

    cur = vp_ref[...].astype(F32)
    prev = jnp.where(t0 > 0, vprev_ref[...].astype(F32), 0.0)
    nxt = jnp.where(t0 + tm < seq, vnext_ref[...].astype(F32), 0.0)
    ext = jnp.concatenate([prev, cur, nxt], axis=0)
    n_ext = tm + 2 * POOL_HALO
    tok = t0 + lax.broadcasted_iota(jnp.int32, (tm, 1), 0)
    mixed = []
    for g, w in enumerate(POOL_WINDOWS):
        acc = ext[:, g * POOL_GROUP_DIM:(g + 1) * POOL_GROUP_DIM]
        acc = acc + pltpu.roll(acc, 1, axis=0)
        half = 1
        while 2 * half < w:
            acc = pltpu.roll(acc, half, axis=0) + pltpu.roll(acc, n_ext - half, axis=0)
            half *= 2
        wsum = acc[POOL_HALO:POOL_HALO + tm]
        count = (jnp.minimum(tok + w // 2, seq) - jnp.maximum(tok - w // 2, 0)).astype(F32)
        pooled = wsum / count - cur[:, g * POOL_GROUP_DIM:(g + 1) * POOL_GROUP_DIM]
        mixed.append(jnp.dot(pooled.astype(BF16), pw_ref[g], preferred_element_type=F32))
    mixed = jnp.concatenate(mixed, axis=1)
    u = (mixed * ps_ref[...] * gp_ref[...].astype(F32)).astype(BF16)
    p_pool = jnp.dot(u, wop_ref[...], preferred_element_type=F32)

    gated = (attn_ref[...].astype(F32) * gm_ref[...].astype(F32)).astype(BF16)
    p_mla = jnp.dot(gated, wom_ref[...], preferred_element_type=F32)

    y = mm_ref[...].astype(F32) * p_mla + mp_ref[...].astype(F32) * p_pool
    r = jnp.dot(y.astype(BF16), wout_ref[...], preferred_element_type=F32)
    xo = x_ref[...] + gate_ref[...] * r
    if final_norm:
        xo = xo * lax.rsqrt(jnp.mean(xo * xo, axis=-1, keepdims=True) + EPS) * fg_ref[...]
    o_ref[...] = xo


def _merge_out(xt, mod4, attn, zbig, pool_w, pool_scale, w_o_pool, w_o_mla, w_out, final_g,
               seq, final_norm):
    T, D = xt.shape
    tm = 256
    per_b = seq // tm
    halo_per_tile = tm // POOL_HALO
    n_halo = T // POOL_HALO
    gp_blk = MLA_WIDTH // POOL_WIDTH
    vp_blk = gp_blk + 1
    mm_blk = (MLA_WIDTH + 2 * POOL_WIDTH) // D
    resident = functools.partial(pl.BlockSpec, pipeline_mode=pl.Buffered(1))
    kern = functools.partial(_merge_out_kernel, seq=seq, final_norm=final_norm)
    return pl.pallas_call(
        kern,
        grid=(T // tm,),
        in_specs=[pl.BlockSpec((tm, D), lambda i: (i, 0)),
                  pl.BlockSpec((None, None, 1, D), lambda i: (i // per_b, 2, 0, 0)),
                  pl.BlockSpec((tm, MLA_WIDTH), lambda i: (i, 0)),
                  pl.BlockSpec((tm, MLA_WIDTH), lambda i: (i, 0)),
                  pl.BlockSpec((tm, POOL_WIDTH), lambda i: (i, gp_blk)),
                  pl.BlockSpec((tm, POOL_WIDTH), lambda i: (i, vp_blk)),
                  pl.BlockSpec((POOL_HALO, POOL_WIDTH),
                               lambda i: (jnp.maximum(i * halo_per_tile - 1, 0), vp_blk)),
                  pl.BlockSpec((POOL_HALO, POOL_WIDTH),
                               lambda i: (jnp.minimum((i + 1) * halo_per_tile, n_halo - 1), vp_blk)),
                  pl.BlockSpec((tm, D), lambda i: (i, mm_blk)),
                  pl.BlockSpec((tm, D), lambda i: (i, mm_blk + 1)),
                  resident(pool_w.shape, lambda i: (0, 0, 0)),
                  resident((1, POOL_WIDTH), lambda i: (0, 0)),
                  resident(w_o_pool.shape, lambda i: (0, 0)),
                  resident(w_o_mla.shape, lambda i: (0, 0)),
                  resident(w_out.shape, lambda i: (0, 0)),
                  resident((1, D), lambda i: (0, 0))],
        out_specs=pl.BlockSpec((tm, D), lambda i: (i, 0)),
        out_shape=jax.ShapeDtypeStruct((T, D), F32),
        compiler_params=_params(("arbitrary",)),
        name="merge_out",
    )(xt, mod4, attn, zbig, zbig, zbig, zbig, zbig, zbig, zbig,
      pool_w, pool_scale.reshape(1, POOL_WIDTH), w_o_pool, w_o_mla, w_out, final_g.reshape(1, D))


def _split_in_proj(w_in):
    o = 0
    cq = w_in[:, o:o + Q_LORA]; o += Q_LORA
    ckv = w_in[:, o:o + KV_LORA]; o += KV_LORA
    kr = w_in[:, o:o + QK_ROPE]; o += QK_ROPE
    g_mla = w_in[:, o:o + MLA_WIDTH]; o += MLA_WIDTH
    v_pool = w_in[:, o:o + POOL_WIDTH]; o += POOL_WIDTH
    g_pool = w_in[:, o:o + POOL_WIDTH]; o += POOL_WIDTH
    m_mla = w_in[:, o:o + MLA_WIDTH]; o += MLA_WIDTH
    m_pool = w_in[:, o:]
    kr_sw = jnp.concatenate([kr[:, QK_ROPE // 2:], kr[:, :QK_ROPE // 2]], axis=1)
    w_small = jnp.concatenate([cq, ckv, kr, kr_sw], axis=1).astype(BF16)
    w_big = jnp.concatenate([g_mla, g_pool, v_pool, m_mla, m_pool], axis=1).astype(BF16)
    return w_small, w_big


def _split_up_proj(w_uq, w_ukv):
    wq = w_uq.reshape(Q_LORA, N_HEADS, QK_HEAD)
    nope = wq[:, :, :QK_NOPE].reshape(Q_LORA, N_HEADS * QK_NOPE)
    rope = wq[:, :, QK_NOPE:]
    rope_sw = jnp.concatenate([rope[:, :, QK_ROPE // 2:], rope[:, :, :QK_ROPE // 2]], axis=2)
    wq_t = jnp.concatenate([nope, rope.reshape(Q_LORA, -1), rope_sw.reshape(Q_LORA, -1)], axis=1).T
    wkv = w_ukv.reshape(KV_LORA, N_HEADS, QK_NOPE + V_HEAD)
    wk = wkv[:, :, :QK_NOPE].reshape(KV_LORA, N_HEADS * QK_NOPE)
    wv_t = wkv[:, :, QK_NOPE:].reshape(KV_LORA, MLA_WIDTH).T
    return wq_t.astype(BF16), wk.astype(BF16), wv_t.astype(BF16)


def kernel(x, c, positions, ada_w, ada_b, norm_g, w_in, q_norm_g, w_uq, kv_norm_g, w_ukv, w_o_mla,
           pool_w, pool_scale, w_o_pool, w_out, final_g):
    B, S, D = x.shape
    depth = ada_w.shape[0]
    inv_freq = 1.0 / (ROPE_THETA ** (jnp.arange(0, QK_ROPE, 2, dtype=F32) / QK_ROPE))
    inv_signed = jnp.concatenate([-inv_freq, inv_freq])
    posr = positions.reshape(B, 1, S)
    posc = positions.reshape(B, S, 1)
    q_scale = QK_HEAD ** -0.5 * math.log2(math.e)

    xt = x.reshape(B * S, D)
    for l in range(depth):
        mod4 = _adaln(c, ada_w[l], ada_b[l]).reshape(B, 3, 1, D)
        w_small, w_big = _split_in_proj(w_in[l])
        wq_t, wk, wv_t = _split_up_proj(w_uq[l], w_ukv[l])

        h, zs = _norm_proj(xt, mod4, norm_g[l], w_small, S)
        zbig = _gate_proj(h, w_big)
        qt, k, vt = _mla_prep(zs, posr, posc, inv_signed, q_norm_g[l], kv_norm_g[l],
                              wq_t, wk, wv_t, B, S, q_scale)
        attn = _attention(qt, k, vt, B, S).reshape(B * S, MLA_WIDTH)
        xt = _merge_out(xt, mod4, attn, zbig, pool_w[l].astype(BF16), pool_scale[l],
                        w_o_pool[l].astype(BF16), w_o_mla[l].astype(BF16), w_out[l].astype(BF16),
                        final_g, S, final_norm=(l == depth - 1))
    return xt.reshape(B, S, D)
```

```python
import functools
import math

import jax
import jax.numpy as jnp
from jax import lax
from jax.experimental import pallas as pl
from jax.experimental.pallas import tpu as pltpu

EPS = 1e-6
N_HEADS = 16
QK_NOPE = 128
QK_ROPE = 64
QK_HEAD = QK_NOPE + QK_ROPE
V_HEAD = 128
Q_LORA = 512
KV_LORA = 512
MLA_WIDTH = N_HEADS * V_HEAD
ROPE_THETA = 10000.0
POOL_WINDOWS = (2, 4, 8, 16)
POOL_GROUPS = len(POOL_WINDOWS)
POOL_GROUP_DIM = 256
POOL_WIDTH = POOL_GROUPS * POOL_GROUP_DIM
POOL_HALO = 16
SMALL_WIDTH = Q_LORA + KV_LORA + 2 * QK_ROPE

V7X_VMEM_LIMIT = 56 * 1024 * 1024

F32 = jnp.float32
BF16 = jnp.bfloat16
NT_DIMS = (((1,), (1,)), ((), ()))


def _sigmoid(v):
    return 1.0 / (1.0 + jnp.exp(-v))


def _params(semantics, vmem=V7X_VMEM_LIMIT):
    return pltpu.CompilerParams(dimension_semantics=semantics, vmem_limit_bytes=vmem)


def _adaln_kernel(ct_ref, w_ref, b_ref, o_ref):
    w = w_ref[...]
    for b in range(ct_ref.shape[1]):
        cb = ct_ref[:, b:b + 1]
        act = cb * _sigmoid(cb)
        o_ref[b:b + 1, :] = jnp.sum(w * act, axis=0, keepdims=True) + b_ref[...]


def _adaln(c, w, bias):
    B, D = c.shape
    n = w.shape[1]
    tn = 512
    return pl.pallas_call(
        _adaln_kernel,
        grid=(n // tn,),
        in_specs=[pl.BlockSpec((D, B), lambda j: (0, 0)),
                  pl.BlockSpec((D, tn), lambda j: (0, j)),
                  pl.BlockSpec((1, tn), lambda j: (0, j))],
        out_specs=pl.BlockSpec((B, tn), lambda j: (0, j)),
        out_shape=jax.ShapeDtypeStruct((B, n), F32),
        compiler_params=_params(("arbitrary",)),
        name="adaln",
    )(c.T, w, bias.reshape(1, n))


def _norm_proj_kernel(x_ref, shift_ref, scale_ref, g_ref, ws_ref, h_ref, zs_ref, ws_bf16):
    @pl.when(pl.program_id(0) == 0)
    def _():
        ws_bf16[...] = ws_ref[...].astype(BF16)

    x = x_ref[...]
    y = x * lax.rsqrt(jnp.mean(x * x, axis=-1, keepdims=True) + EPS) * g_ref[...]
    h = (y * (1.0 + scale_ref[...]) + shift_ref[...]).astype(BF16)
    h_ref[...] = h
    zs_ref[...] = jnp.dot(h, ws_bf16[...], preferred_element_type=F32)


def _norm_proj(xt, mod4, norm_g, w_in, seq):
    T, D = xt.shape
    tm = 512
    per_b = seq // tm
    return pl.pallas_call(
        _norm_proj_kernel,
        grid=(T // tm,),
        in_specs=[pl.BlockSpec((tm, D), lambda i: (i, 0)),
                  pl.BlockSpec((None, None, 1, D), lambda i: (i // per_b, 0, 0, 0)),
                  pl.BlockSpec((None, None, 1, D), lambda i: (i // per_b, 1, 0, 0)),
                  pl.BlockSpec((1, D), lambda i: (0, 0)),
                  pl.BlockSpec((D, SMALL_WIDTH), lambda i: (0, 0), pipeline_mode=pl.Buffered(1))],
        out_specs=[pl.BlockSpec((tm, D), lambda i: (i, 0)),
                   pl.BlockSpec((tm, SMALL_WIDTH), lambda i: (i, 0))],
        out_shape=[jax.ShapeDtypeStruct((T, D), BF16),
                   jax.ShapeDtypeStruct((T, SMALL_WIDTH), F32)],
        scratch_shapes=[pltpu.VMEM((D, SMALL_WIDTH), BF16)],
        compiler_params=_params(("arbitrary",)),
        name="norm_proj",
    )(xt, mod4, mod4, norm_g.reshape(1, D), w_in)


_ACTIVATIONS = {
    "silu": lambda a: a * _sigmoid(a),
    "linear": lambda a: a,
    "sigmoid": _sigmoid,
}


def _gate_proj_kernel(h_ref, w_ref, o_ref, *, tile_kinds):
    j = pl.program_id(1)
    for kind, act in _ACTIVATIONS.items():
        tiles = [t for t, k in enumerate(tile_kinds) if k == kind]
        cond = functools.reduce(jnp.logical_or, [j == t for t in tiles])

        @pl.when(cond)
        def _(act=act):
            acc = jnp.dot(h_ref[...], w_ref[...], preferred_element_type=F32)
            o_ref[...] = act(acc).astype(o_ref.dtype)


def _gate_proj(h, w_big):
    T, D = h.shape
    n = w_big.shape[1]
    tm, tn = 1024, 1024
    tile_kinds = (("silu",) * (MLA_WIDTH // tn) + ("linear",) * (POOL_WIDTH // tn)
                  + ("silu",) * (POOL_WIDTH // tn) + ("sigmoid",) * (2 * D // tn))
    kern = functools.partial(_gate_proj_kernel, tile_kinds=tile_kinds)
    return pl.pallas_call(
        kern,
        grid=(T // tm, n // tn),
        in_specs=[pl.BlockSpec((tm, D), lambda i, j: (i, 0)),
                  pl.BlockSpec((D, tn), lambda i, j: (0, j))],
        out_specs=pl.BlockSpec((tm, tn), lambda i, j: (i, j)),
        out_shape=jax.ShapeDtypeStruct((T, n), BF16),
        compiler_params=_params(("arbitrary", "arbitrary")),
        name="gate_proj",
    )(h, w_big)


def _mla_prep_kernel(zs_ref, posr_ref, posc_ref, invr_ref, invc_ref, qg_ref, kvg_ref,
                     wq_ref, wk_ref, wv_ref, qt_ref, k_ref, vt_ref, *, q_scale):
    def rms(v, g):
        return (v * lax.rsqrt(jnp.mean(v * v, axis=-1, keepdims=True) + EPS) * g).astype(BF16)

    cqn = rms(zs_ref[:, 0:Q_LORA], qg_ref[...])
    ckvn = rms(zs_ref[:, Q_LORA:Q_LORA + KV_LORA], kvg_ref[...])
    kr = zs_ref[:, Q_LORA + KV_LORA:Q_LORA + KV_LORA + QK_ROPE]
    kr_sw = jnp.concatenate([kr[:, QK_ROPE // 2:], kr[:, :QK_ROPE // 2]], axis=1)

    ang_t = invc_ref[...] * posr_ref[...].astype(F32)
    cos_t, sin_t = jnp.cos(ang_t), jnp.sin(ang_t)
    ang = posc_ref[...].astype(F32) * invr_ref[...]
    cos, sin = jnp.cos(ang), jnp.sin(ang)

    qf = lax.dot_general(wq_ref[...], cqn, NT_DIMS, preferred_element_type=F32)
    rope0 = N_HEADS * QK_NOPE
    swap0 = rope0 + N_HEADS * QK_ROPE
    for h in range(N_HEADS):
        qt_ref[h * QK_HEAD:h * QK_HEAD + QK_NOPE, :] = (
            qf[h * QK_NOPE:(h + 1) * QK_NOPE] * q_scale).astype(BF16)
        rot = (qf[rope0 + h * QK_ROPE:rope0 + (h + 1) * QK_ROPE] * cos_t
               + qf[swap0 + h * QK_ROPE:swap0 + (h + 1) * QK_ROPE] * sin_t)
        qt_ref[h * QK_HEAD + QK_NOPE:(h + 1) * QK_HEAD, :] = (rot * q_scale).astype(BF16)

    kn = jnp.dot(ckvn, wk_ref[...], preferred_element_type=F32)
    k_rot = (kr * cos + kr_sw * sin).astype(BF16)
    for h in range(N_HEADS):
        k_ref[h, :, 0:QK_NOPE] = kn[:, h * QK_NOPE:(h + 1) * QK_NOPE].astype(BF16)
        k_ref[h, :, QK_NOPE:QK_HEAD] = k_rot

    vt_ref[...] = lax.dot_general(wv_ref[...], ckvn, NT_DIMS,
                                  preferred_element_type=F32).astype(BF16)


def _mla_prep(zs, posr, posc, inv_signed, q_norm_g, kv_norm_g, wq_t, wk, wv_t, batch, seq, q_scale):
    tm = 256
    per_b = seq // tm
    const = lambda i: (0, 0)
    kern = functools.partial(_mla_prep_kernel, q_scale=q_scale)
    return pl.pallas_call(
        kern,
        grid=(batch * per_b,),
        in_specs=[pl.BlockSpec((tm, SMALL_WIDTH), lambda i: (i, 0)),
                  pl.BlockSpec((None, 1, tm), lambda i: (i // per_b, 0, i % per_b)),
                  pl.BlockSpec((None, tm, 1), lambda i: (i // per_b, i % per_b, 0)),
                  pl.BlockSpec((1, QK_ROPE), const),
                  pl.BlockSpec((QK_ROPE, 1), const),
                  pl.BlockSpec((1, Q_LORA), const),
                  pl.BlockSpec((1, KV_LORA), const),
                  pl.BlockSpec(wq_t.shape, const),
                  pl.BlockSpec(wk.shape, const),
                  pl.BlockSpec(wv_t.shape, const)],
        out_specs=[pl.BlockSpec((None, N_HEADS * QK_HEAD, tm), lambda i: (i // per_b, 0, i % per_b)),
                   pl.BlockSpec((None, N_HEADS, tm, QK_HEAD), lambda i: (i // per_b, 0, i % per_b, 0)),
                   pl.BlockSpec((None, MLA_WIDTH, tm), lambda i: (i // per_b, 0, i % per_b))],
        out_shape=[jax.ShapeDtypeStruct((batch, N_HEADS * QK_HEAD, seq), BF16),
                   jax.ShapeDtypeStruct((batch, N_HEADS, seq, QK_HEAD), BF16),
                   jax.ShapeDtypeStruct((batch, MLA_WIDTH, seq), BF16)],
        compiler_params=_params(("arbitrary",)),
        name="mla_prep",
    )(zs, posr, posc, inv_signed.reshape(1, QK_ROPE), inv_signed.reshape(QK_ROPE, 1),
      q_norm_g.reshape(1, Q_LORA), kv_norm_g.reshape(1, KV_LORA), wq_t, wk, wv_t)


def _attention_kernel(qt_ref, k_ref, vt_ref, o_ref, s_a, m_a, s_b, m_b):
    t = pl.program_id(0)
    n_tiles = pl.num_programs(0) - 1

    def scores(s_ref, m_ref):
        s = jnp.dot(k_ref[...], qt_ref[...], preferred_element_type=F32)
        s_ref[...] = s
        m_ref[...] = jnp.max(s, axis=0, keepdims=True)

    def finish(s_ref, m_ref):
        p = jnp.exp2(s_ref[...] - m_ref[...])
        l = jnp.sum(p, axis=0, keepdims=True)
        ot = jnp.dot(vt_ref[...], p.astype(BF16), preferred_element_type=F32)
        o_ref[...] = (ot / l).T.astype(o_ref.dtype)

    steady = jnp.logical_and(t > 0, t < n_tiles)
    even = t % 2 == 0

    @pl.when(t == 0)
    def _():
        scores(s_a, m_a)

    @pl.when(jnp.logical_and(steady, even))
    def _():
        scores(s_a, m_a)
        finish(s_b, m_b)

    @pl.when(jnp.logical_and(steady, jnp.logical_not(even)))
    def _():
        scores(s_b, m_b)
        finish(s_a, m_a)

    @pl.when(jnp.logical_and(t == n_tiles, even))
    def _():
        finish(s_b, m_b)

    @pl.when(jnp.logical_and(t == n_tiles, jnp.logical_not(even)))
    def _():
        finish(s_a, m_a)


def _attention(qt, k, vt, batch, seq):
    tq = 256
    nq = seq // tq
    n_tiles = batch * N_HEADS * nq

    def tile(t):
        return t // (N_HEADS * nq), (t // nq) % N_HEADS, t % nq

    def cur(t):
        return tile(jnp.minimum(t, n_tiles - 1))

    def prev(t):
        return tile(jnp.maximum(t - 1, 0))

    return pl.pallas_call(
        _attention_kernel,
        grid=(n_tiles + 1,),
        in_specs=[pl.BlockSpec((None, QK_HEAD, tq), lambda t: cur(t)),
                  pl.BlockSpec((None, None, seq, QK_HEAD), lambda t: cur(t)[:2] + (0, 0)),
                  pl.BlockSpec((None, V_HEAD, seq), lambda t: prev(t)[:2] + (0,))],
        out_specs=pl.BlockSpec((None, tq, V_HEAD), lambda t: (prev(t)[0], prev(t)[2], prev(t)[1])),
        out_shape=jax.ShapeDtypeStruct((batch, seq, MLA_WIDTH), BF16),
        scratch_shapes=[pltpu.VMEM((seq, tq), F32), pltpu.VMEM((1, tq), F32),
                        pltpu.VMEM((seq, tq), F32), pltpu.VMEM((1, tq), F32)],
        compiler_params=_params(("arbitrary",)),
        name="attention",
    )(qt, k, vt)


def _merge_out_kernel(x_ref, gate_ref, attn_ref, gm_ref, gp_ref, vp_ref, vprev_ref, vnext_ref,
                      mm_ref, mp_ref, pw_ref, ps_ref, wop_ref, wom_ref, wout_ref, fg_ref, o_ref,
                      *, seq, final_norm):
    tm = x_ref.shape[0]
    t0 = (pl.program_id(0) % (seq // tm)) * tm

    cur = vp_ref[...].astype(F32)
    prev = jnp.where(t0 > 0, vprev_ref[...].astype(F32), 0.0)
    nxt = jnp.where(t0 + tm < seq, vnext_ref[...].astype(F32), 0.0)
    ext = jnp.concatenate([prev, cur, nxt], axis=0)
    n_ext = tm + 2 * POOL_HALO
    tok = t0 + lax.broadcasted_iota(jnp.int32, (tm, 1), 0)
    mixed = []
    for g, w in enumerate(POOL_WINDOWS):
        acc = ext[:, g * POOL_GROUP_DIM:(g + 1) * POOL_GROUP_DIM]
        acc = acc + pltpu.roll(acc, 1, axis=0)
        half = 1
        while 2 * half < w:
            acc = pltpu.roll(acc, half, axis=0) + pltpu.roll(acc, n_ext - half, axis=0)
            half *= 2
        wsum = acc[POOL_HALO:POOL_HALO + tm]
        count = (jnp.minimum(tok + w // 2, seq) - jnp.maximum(tok - w // 2, 0)).astype(F32)
        pooled = wsum / count - cur[:, g * POOL_GROUP_DIM:(g + 1) * POOL_GROUP_DIM]
        mixed.append(jnp.dot(pooled.astype(BF16), pw_ref[g], preferred_element_type=F32))
    mixed = jnp.concatenate(mixed, axis=1)
    u = (mixed * ps_ref[...] * gp_ref[...].astype(F32)).astype(BF16)
    p_pool = jnp.dot(u, wop_ref[...], preferred_element_type=F32)

    gated = (attn_ref[...].astype(F32) * gm_ref[...].astype(F32)).astype(BF16)
    p_mla = jnp.dot(gated, wom_ref[...], preferred_element_type=F32)

    y = mm_ref[...].astype(F32) * p_mla + mp_ref[...].astype(F32) * p_pool
    r = jnp.dot(y.astype(BF16), wout_ref[...], preferred_element_type=F32)
    xo = x_ref[...] + gate_ref[...] * r
    if final_norm:
        xo = xo * lax.rsqrt(jnp.mean(xo * xo, axis=-1, keepdims=True) + EPS) * fg_ref[...]
    o_ref[...] = xo


def _merge_out(xt, mod4, attn, zbig, pool_w, pool_scale, w_o_pool, w_o_mla, w_out, final_g,
               seq, final_norm):
    T, D = xt.shape
    tm = 256
    per_b = seq // tm
    halo_per_tile = tm // POOL_HALO
    n_halo = T // POOL_HALO
    vp_blk = MLA_WIDTH // POOL_WIDTH
    gp_blk = vp_blk + 1
    mm_blk = (MLA_WIDTH + 2 * POOL_WIDTH) // D
    resident = functools.partial(pl.BlockSpec, pipeline_mode=pl.Buffered(1))
    kern = functools.partial(_merge_out_kernel, seq=seq, final_norm=final_norm)
    return pl.pallas_call(
        kern,
        grid=(T // tm,),
        in_specs=[pl.BlockSpec((tm, D), lambda i: (i, 0)),
                  pl.BlockSpec((None, None, 1, D), lambda i: (i // per_b, 2, 0, 0)),
                  pl.BlockSpec((tm, MLA_WIDTH), lambda i: (i, 0)),
                  pl.BlockSpec((tm, MLA_WIDTH), lambda i: (i, 0)),
                  pl.BlockSpec((tm, POOL_WIDTH), lambda i: (i, gp_blk)),
                  pl.BlockSpec((tm, POOL_WIDTH), lambda i: (i, vp_blk)),
                  pl.BlockSpec((POOL_HALO, POOL_WIDTH),
                               lambda i: (jnp.maximum(i * halo_per_tile - 1, 0), vp_blk)),
                  pl.BlockSpec((POOL_HALO, POOL_WIDTH),
                               lambda i: (jnp.minimum((i + 1) * halo_per_tile, n_halo - 1), vp_blk)),
                  pl.BlockSpec((tm, D), lambda i: (i, mm_blk)),
                  pl.BlockSpec((tm, D), lambda i: (i, mm_blk + 1)),
                  resident(pool_w.shape, lambda i: (0, 0, 0)),
                  resident((1, POOL_WIDTH), lambda i: (0, 0)),
                  resident(w_o_pool.shape, lambda i: (0, 0)),
                  resident(w_o_mla.shape, lambda i: (0, 0)),
                  resident(w_out.shape, lambda i: (0, 0)),
                  resident((1, D), lambda i: (0, 0))],
        out_specs=pl.BlockSpec((tm, D), lambda i: (i, 0)),
        out_shape=jax.ShapeDtypeStruct((T, D), F32),
        compiler_params=_params(("arbitrary",)),
        name="merge_out",
    )(xt, mod4, attn, zbig, zbig, zbig, zbig, zbig, zbig, zbig,
      pool_w, pool_scale.reshape(1, POOL_WIDTH), w_o_pool, w_o_mla, w_out, final_g.reshape(1, D))


def _split_up_proj(w_uq, w_ukv):
    wq = w_uq.reshape(Q_LORA, N_HEADS, QK_HEAD)
    nope = wq[:, :, :QK_NOPE].reshape(Q_LORA, N_HEADS * QK_NOPE)
    rope = wq[:, :, QK_NOPE:]
    rope_sw = jnp.concatenate([rope[:, :, QK_ROPE // 2:], rope[:, :, :QK_ROPE // 2]], axis=2)
    wq_t = jnp.concatenate([nope, rope.reshape(Q_LORA, -1), rope_sw.reshape(Q_LORA, -1)], axis=1).T
    wkv = w_ukv.reshape(KV_LORA, N_HEADS, QK_NOPE + V_HEAD)
    wk = wkv[:, :, :QK_NOPE].reshape(KV_LORA, N_HEADS * QK_NOPE)
    wv_t = wkv[:, :, QK_NOPE:].reshape(KV_LORA, MLA_WIDTH).T
    return wq_t.astype(BF16), wk.astype(BF16), wv_t.astype(BF16)


def kernel(x, c, positions, ada_w, ada_b, norm_g, w_in, q_norm_g, w_uq, kv_norm_g, w_ukv, w_o_mla,
           pool_w, pool_scale, w_o_pool, w_out, final_g):
    B, S, D = x.shape
    depth = ada_w.shape[0]
    inv_freq = 1.0 / (ROPE_THETA ** (jnp.arange(0, QK_ROPE, 2, dtype=F32) / QK_ROPE))
    inv_signed = jnp.concatenate([-inv_freq, inv_freq])
    posr = positions.reshape(B, 1, S)
    posc = positions.reshape(B, S, 1)
    q_scale = QK_HEAD ** -0.5 * math.log2(math.e)

    xt = x.reshape(B * S, D)
    for l in range(depth):
        mod4 = _adaln(c, ada_w[l], ada_b[l]).reshape(B, 3, 1, D)
        w_big = w_in[l][:, Q_LORA + KV_LORA + QK_ROPE:].astype(BF16)
        wq_t, wk, wv_t = _split_up_proj(w_uq[l], w_ukv[l])

        h, zs = _norm_proj(xt, mod4, norm_g[l], w_in[l], S)
        zbig = _gate_proj(h, w_big)
        qt, k, vt = _mla_prep(zs, posr, posc, inv_signed, q_norm_g[l], kv_norm_g[l],
                              wq_t, wk, wv_t, B, S, q_scale)
        attn = _attention(qt, k, vt, B, S).reshape(B * S, MLA_WIDTH)
        xt = _merge_out(xt, mod4, attn, zbig, pool_w[l].astype(BF16), pool_scale[l],
                        w_o_pool[l].astype(BF16), w_o_mla[l].astype(BF16), w_out[l].astype(BF16),
                        final_g, S, final_norm=(l == depth - 1))
    return xt.reshape(B, S, D)
```

```python
import functools
import math

import jax
import jax.numpy as jnp
from jax import lax
from jax.experimental import pallas as pl
from jax.experimental.pallas import tpu as pltpu

EPS = 1e-6
N_HEADS = 16
QK_NOPE = 128
QK_ROPE = 64
QK_HEAD = QK_NOPE + QK_ROPE
V_HEAD = 128
Q_LORA = 512
KV_LORA = 512
MLA_WIDTH = N_HEADS * V_HEAD
ROPE_THETA = 10000.0
POOL_WINDOWS = (2, 4, 8, 16)
POOL_GROUPS = len(POOL_WINDOWS)
POOL_GROUP_DIM = 256
POOL_WIDTH = POOL_GROUPS * POOL_GROUP_DIM
POOL_HALO = 16
SMALL_WIDTH = Q_LORA + KV_LORA + 2 * QK_ROPE

V7X_VMEM_LIMIT = 56 * 1024 * 1024
LANES = 128

F32 = jnp.float32
BF16 = jnp.bfloat16
NT_DIMS = (((1,), (1,)), ((), ()))


def _sigmoid(v):
    return 1.0 / (1.0 + jnp.exp(-v))


def _params(semantics, vmem=V7X_VMEM_LIMIT, flags=None):
    return pltpu.CompilerParams(dimension_semantics=semantics, vmem_limit_bytes=vmem, flags=flags)


def _adaln_kernel(ct_ref, w_ref, b_ref, o_ref):
    w = w_ref[...]
    for b in range(ct_ref.shape[1]):
        cb = ct_ref[:, b:b + 1]
        act = cb * _sigmoid(cb)
        o_ref[b:b + 1, :] = jnp.sum(w * act, axis=0, keepdims=True) + b_ref[...]


def _adaln(c, w, bias):
    B, D = c.shape
    n = w.shape[1]
    tn = 512
    return pl.pallas_call(
        _adaln_kernel,
        grid=(n // tn,),
        in_specs=[pl.BlockSpec((D, B), lambda j: (0, 0)),
                  pl.BlockSpec((D, tn), lambda j: (0, j)),
                  pl.BlockSpec((1, tn), lambda j: (0, j))],
        out_specs=pl.BlockSpec((B, tn), lambda j: (0, j)),
        out_shape=jax.ShapeDtypeStruct((B, n), F32),
        compiler_params=_params(("arbitrary",)),
        name="adaln",
    )(c.T, w, bias.reshape(1, n))


def _norm_proj_kernel(x_ref, shift_ref, scale_ref, g_ref, ws_ref, h_ref, zs_ref, ws_bf16):
    @pl.when(pl.program_id(0) == 0)
    def _():
        ws_bf16[...] = ws_ref[...].astype(BF16)

    x = x_ref[...]
    y = x * lax.rsqrt(jnp.mean(x * x, axis=-1, keepdims=True) + EPS) * g_ref[...]
    h = (y * (1.0 + scale_ref[...]) + shift_ref[...]).astype(BF16)
    h_ref[...] = h
    zs_ref[...] = jnp.dot(h, ws_bf16[...], preferred_element_type=F32)


def _norm_proj(xt, mod4, norm_g, w_in, seq):
    T, D = xt.shape
    tm = 512
    per_b = seq // tm
    return pl.pallas_call(
        _norm_proj_kernel,
        grid=(T // tm,),
        in_specs=[pl.BlockSpec((tm, D), lambda i: (i, 0)),
                  pl.BlockSpec((None, None, 1, D), lambda i: (i // per_b, 0, 0, 0)),
                  pl.BlockSpec((None, None, 1, D), lambda i: (i // per_b, 1, 0, 0)),
                  pl.BlockSpec((1, D), lambda i: (0, 0)),
                  pl.BlockSpec((D, SMALL_WIDTH), lambda i: (0, 0), pipeline_mode=pl.Buffered(1))],
        out_specs=[pl.BlockSpec((tm, D), lambda i: (i, 0)),
                   pl.BlockSpec((tm, SMALL_WIDTH), lambda i: (i, 0))],
        out_shape=[jax.ShapeDtypeStruct((T, D), BF16),
                   jax.ShapeDtypeStruct((T, SMALL_WIDTH), F32)],
        scratch_shapes=[pltpu.VMEM((D, SMALL_WIDTH), BF16)],
        compiler_params=_params(("arbitrary",)),
        name="norm_proj",
    )(xt, mod4, mod4, norm_g.reshape(1, D), w_in)


_ACTIVATIONS = {
    "silu": lambda a: a * _sigmoid(a),
    "linear": lambda a: a,
    "sigmoid": _sigmoid,
}


def _gate_proj_kernel(h_ref, wa_ref, wb_ref, o_ref, w_bf16, *, tile_kinds, shift):
    j = pl.program_id(0)

    @pl.when(pl.program_id(1) == 0)
    def _():
        w_bf16[...] = jnp.concatenate([wa_ref[:, shift:], wb_ref[:, :shift]], axis=1).astype(BF16)

    for kind, act in _ACTIVATIONS.items():
        tiles = [t for t, k in enumerate(tile_kinds) if k == kind]
        cond = functools.reduce(jnp.logical_or, [j == t for t in tiles])

        @pl.when(cond)
        def _(act=act):
            acc = jnp.dot(h_ref[...], w_bf16[...], preferred_element_type=F32)
            o_ref[...] = act(acc).astype(o_ref.dtype)


def _gate_proj(h, w_in):
    T, D = h.shape
    tm, tn = 1024, 1024
    start = Q_LORA + KV_LORA + QK_ROPE
    n = w_in.shape[1] - start
    shift = start % LANES
    first_blk, rem = divmod(start - shift, tn)
    assert rem == 0 and n % tn == 0 and 0 < shift < LANES
    tile_kinds = (("silu",) * (MLA_WIDTH // tn) + ("linear",) * (POOL_WIDTH // tn)
                  + ("silu",) * (POOL_WIDTH // tn) + ("sigmoid",) * (2 * D // tn))
    kern = functools.partial(_gate_proj_kernel, tile_kinds=tile_kinds, shift=shift)
    return pl.pallas_call(
        kern,
        grid=(n // tn, T // tm),
        in_specs=[pl.BlockSpec((tm, D), lambda j, i: (i, 0)),
                  pl.BlockSpec((D, tn), lambda j, i: (0, first_blk + j)),
                  pl.BlockSpec((D, LANES), lambda j, i: (0, (first_blk + j + 1) * (tn // LANES)))],
        out_specs=pl.BlockSpec((tm, tn), lambda j, i: (i, j)),
        out_shape=jax.ShapeDtypeStruct((T, n), BF16),
        scratch_shapes=[pltpu.VMEM((D, tn), BF16)],
        compiler_params=_params(("arbitrary", "arbitrary")),
        name="gate_proj",
    )(h, w_in, w_in)


def _mla_prep_kernel(zs_ref, posr_ref, posc_ref, invr_ref, invc_ref, qg_ref, kvg_ref,
                     wq_ref, wk_ref, wv_ref, qt_ref, k_ref, vt_ref, *, q_scale):
    def rms(v, g):
        return (v * lax.rsqrt(jnp.mean(v * v, axis=-1, keepdims=True) + EPS) * g).astype(BF16)

    cqn = rms(zs_ref[:, 0:Q_LORA], qg_ref[...])
    ckvn = rms(zs_ref[:, Q_LORA:Q_LORA + KV_LORA], kvg_ref[...])
    kr = zs_ref[:, Q_LORA + KV_LORA:Q_LORA + KV_LORA + QK_ROPE]
    kr_sw = jnp.concatenate([kr[:, QK_ROPE // 2:], kr[:, :QK_ROPE // 2]], axis=1)

    ang_t = invc_ref[...] * posr_ref[...].astype(F32)
    cos_t, sin_t = jnp.cos(ang_t), jnp.sin(ang_t)
    ang = posc_ref[...].astype(F32) * invr_ref[...]
    cos, sin = jnp.cos(ang), jnp.sin(ang)

    qf = lax.dot_general(wq_ref[...], cqn, NT_DIMS, preferred_element_type=F32)
    rope0 = N_HEADS * QK_NOPE
    swap0 = rope0 + N_HEADS * QK_ROPE
    for h in range(N_HEADS):
        qt_ref[h * QK_HEAD:h * QK_HEAD + QK_NOPE, :] = (
            qf[h * QK_NOPE:(h + 1) * QK_NOPE] * q_scale).astype(BF16)
        rot = (qf[rope0 + h * QK_ROPE:rope0 + (h + 1) * QK_ROPE] * cos_t
               + qf[swap0 + h * QK_ROPE:swap0 + (h + 1) * QK_ROPE] * sin_t)
        qt_ref[h * QK_HEAD + QK_NOPE:(h + 1) * QK_HEAD, :] = (rot * q_scale).astype(BF16)

    kn = jnp.dot(ckvn, wk_ref[...], preferred_element_type=F32)
    k_rot = (kr * cos + kr_sw * sin).astype(BF16)
    for h in range(N_HEADS):
        k_ref[h, :, 0:QK_NOPE] = kn[:, h * QK_NOPE:(h + 1) * QK_NOPE].astype(BF16)
        k_ref[h, :, QK_NOPE:QK_HEAD] = k_rot

    vt_ref[...] = lax.dot_general(wv_ref[...], ckvn, NT_DIMS,
                                  preferred_element_type=F32).astype(BF16)


def _mla_prep(zs, posr, posc, inv_signed, q_norm_g, kv_norm_g, wq_t, wk, wv_t, batch, seq, q_scale):
    tm = 256
    per_b = seq // tm
    const = lambda i: (0, 0)
    kern = functools.partial(_mla_prep_kernel, q_scale=q_scale)
    return pl.pallas_call(
        kern,
        grid=(batch * per_b,),
        in_specs=[pl.BlockSpec((tm, SMALL_WIDTH), lambda i: (i, 0)),
                  pl.BlockSpec((None, 1, tm), lambda i: (i // per_b, 0, i % per_b)),
                  pl.BlockSpec((None, tm, 1), lambda i: (i // per_b, i % per_b, 0)),
                  pl.BlockSpec((1, QK_ROPE), const),
                  pl.BlockSpec((QK_ROPE, 1), const),
                  pl.BlockSpec((1, Q_LORA), const),
                  pl.BlockSpec((1, KV_LORA), const),
                  pl.BlockSpec(wq_t.shape, const),
                  pl.BlockSpec(wk.shape, const),
                  pl.BlockSpec(wv_t.shape, const)],
        out_specs=[pl.BlockSpec((None, N_HEADS * QK_HEAD, tm), lambda i: (i // per_b, 0, i % per_b)),
                   pl.BlockSpec((None, N_HEADS, tm, QK_HEAD), lambda i: (i // per_b, 0, i % per_b, 0)),
                   pl.BlockSpec((None, MLA_WIDTH, tm), lambda i: (i // per_b, 0, i % per_b))],
        out_shape=[jax.ShapeDtypeStruct((batch, N_HEADS * QK_HEAD, seq), BF16),
                   jax.ShapeDtypeStruct((batch, N_HEADS, seq, QK_HEAD), BF16),
                   jax.ShapeDtypeStruct((batch, MLA_WIDTH, seq), BF16)],
        compiler_params=_params(("arbitrary",)),
        name="mla_prep",
    )(zs, posr, posc, inv_signed.reshape(1, QK_ROPE), inv_signed.reshape(QK_ROPE, 1),
      q_norm_g.reshape(1, Q_LORA), kv_norm_g.reshape(1, KV_LORA), wq_t, wk, wv_t)


def _attention_kernel(qt_ref, k_ref, vt_ref, o_ref, s_a, m_a, s_b, m_b, *, tq):
    t = pl.program_id(0)
    n_tiles = pl.num_programs(0) - 1
    nq = qt_ref.shape[1] // tq
    q_off = pl.multiple_of((t % nq) * tq, tq)
    o_off = pl.multiple_of(((t + nq - 1) % nq) * tq, tq)

    def scores(s_ref, m_ref):
        s = jnp.dot(k_ref[...], qt_ref[:, pl.ds(q_off, tq)], preferred_element_type=F32)
        s_ref[...] = s
        m_ref[...] = jnp.max(s, axis=0, keepdims=True)

    def finish(s_ref, m_ref):
        p = jnp.exp2(s_ref[...] - m_ref[...])
        l = jnp.sum(p, axis=0, keepdims=True)
        ot = jnp.dot(vt_ref[...], p.astype(BF16), preferred_element_type=F32)
        o_ref[pl.ds(o_off, tq), :] = (ot / l).T.astype(o_ref.dtype)

    steady = jnp.logical_and(t > 0, t < n_tiles)
    even = t % 2 == 0

    @pl.when(t == 0)
    def _():
        scores(s_a, m_a)

    @pl.when(jnp.logical_and(steady, even))
    def _():
        scores(s_a, m_a)
        finish(s_b, m_b)

    @pl.when(jnp.logical_and(steady, jnp.logical_not(even)))
    def _():
        scores(s_b, m_b)
        finish(s_a, m_a)

    @pl.when(jnp.logical_and(t == n_tiles, even))
    def _():
        finish(s_b, m_b)

    @pl.when(jnp.logical_and(t == n_tiles, jnp.logical_not(even)))
    def _():
        finish(s_a, m_a)


def _attention(qt, k, vt, batch, seq):
    tq = 256
    nq = seq // tq
    n_tiles = batch * N_HEADS * nq

    def head_of(tile):
        return tile // (N_HEADS * nq), (tile // nq) % N_HEADS

    def cur(t):
        return head_of(jnp.minimum(t, n_tiles - 1))

    def prev(t):
        return head_of(jnp.maximum(t - 1, 0))

    return pl.pallas_call(
        functools.partial(_attention_kernel, tq=tq),
        grid=(n_tiles + 1,),
        in_specs=[pl.BlockSpec((None, QK_HEAD, seq), lambda t: cur(t) + (0,)),
                  pl.BlockSpec((None, None, seq, QK_HEAD), lambda t: cur(t) + (0, 0)),
                  pl.BlockSpec((None, V_HEAD, seq), lambda t: prev(t) + (0,))],
        out_specs=pl.BlockSpec((None, seq, V_HEAD), lambda t: (prev(t)[0], 0, prev(t)[1])),
        out_shape=jax.ShapeDtypeStruct((batch, seq, MLA_WIDTH), BF16),
        scratch_shapes=[pltpu.VMEM((seq, tq), F32), pltpu.VMEM((1, tq), F32),
                        pltpu.VMEM((seq, tq), F32), pltpu.VMEM((1, tq), F32)],
        compiler_params=_params(("arbitrary",)),
        name="attention",
    )(qt, k, vt)


def _merge_out_kernel(x_ref, gate_ref, attn_ref, gm_ref, gp_ref, vp_ref, vprev_ref, vnext_ref,
                      mm_ref, mp_ref, pw_ref, ps_ref, wop_ref, wom_ref, wout_ref, fg_ref, o_ref,
                      *, seq, final_norm):
    tm = x_ref.shape[0]
    t0 = (pl.program_id(0) % (seq // tm)) * tm

    cur = vp_ref[...].astype(F32)
    prev = jnp.where(t0 > 0, vprev_ref[...].astype(F32), 0.0)
    nxt = jnp.where(t0 + tm < seq, vnext_ref[...].astype(F32), 0.0)
    ext = jnp.concatenate([prev, cur, nxt], axis=0)
    n_ext = tm + 2 * POOL_HALO
    tok = t0 + lax.broadcasted_iota(jnp.int32, (tm, 1), 0)
    mixed = []
    for g, w in enumerate(POOL_WINDOWS):
        acc = ext[:, g * POOL_GROUP_DIM:(g + 1) * POOL_GROUP_DIM]
        acc = acc + pltpu.roll(acc, 1, axis=0)
        half = 1
        while 2 * half < w:
            acc = pltpu.roll(acc, half, axis=0) + pltpu.roll(acc, n_ext - half, axis=0)
            half *= 2
        wsum = acc[POOL_HALO:POOL_HALO + tm]
        count = (jnp.minimum(tok + w // 2, seq) - jnp.maximum(tok - w // 2, 0)).astype(F32)
        pooled = wsum / count - cur[:, g * POOL_GROUP_DIM:(g + 1) * POOL_GROUP_DIM]
        mixed.append(jnp.dot(pooled.astype(BF16), pw_ref[g], preferred_element_type=F32))
    mixed = jnp.concatenate(mixed, axis=1)
    u = (mixed * ps_ref[...] * gp_ref[...].astype(F32)).astype(BF16)
    p_pool = jnp.dot(u, wop_ref[...], preferred_element_type=F32)

    gated = (attn_ref[...].astype(F32) * gm_ref[...].astype(F32)).astype(BF16)
    p_mla = jnp.dot(gated, wom_ref[...], preferred_element_type=F32)

    y = mm_ref[...].astype(F32) * p_mla + mp_ref[...].astype(F32) * p_pool
    r = jnp.dot(y.astype(BF16), wout_ref[...], preferred_element_type=F32)
    xo = x_ref[...] + gate_ref[...] * r
    if final_norm:
        xo = xo * lax.rsqrt(jnp.mean(xo * xo, axis=-1, keepdims=True) + EPS) * fg_ref[...]
    o_ref[...] = xo


def _merge_out(xt, mod4, attn, zbig, pool_w, pool_scale, w_o_pool, w_o_mla, w_out, final_g,
               seq, final_norm):
    T, D = xt.shape
    tm = 256
    per_b = seq // tm
    halo_per_tile = tm // POOL_HALO
    n_halo = T // POOL_HALO
    vp_blk = MLA_WIDTH // POOL_WIDTH
    gp_blk = vp_blk + 1
    mm_blk = (MLA_WIDTH + 2 * POOL_WIDTH) // D
    resident = functools.partial(pl.BlockSpec, pipeline_mode=pl.Buffered(1))
    kern = functools.partial(_merge_out_kernel, seq=seq, final_norm=final_norm)
    return pl.pallas_call(
        kern,
        grid=(T // tm,),
        in_specs=[pl.BlockSpec((tm, D), lambda i: (i, 0)),
                  pl.BlockSpec((None, None, 1, D), lambda i: (i // per_b, 2, 0, 0)),
                  pl.BlockSpec((tm, MLA_WIDTH), lambda i: (i, 0)),
                  pl.BlockSpec((tm, MLA_WIDTH), lambda i: (i, 0)),
                  pl.BlockSpec((tm, POOL_WIDTH), lambda i: (i, gp_blk)),
                  pl.BlockSpec((tm, POOL_WIDTH), lambda i: (i, vp_blk)),
                  pl.BlockSpec((POOL_HALO, POOL_WIDTH),
                               lambda i: (jnp.maximum(i * halo_per_tile - 1, 0), vp_blk)),
                  pl.BlockSpec((POOL_HALO, POOL_WIDTH),
                               lambda i: (jnp.minimum((i + 1) * halo_per_tile, n_halo - 1), vp_blk)),
                  pl.BlockSpec((tm, D), lambda i: (i, mm_blk)),
                  pl.BlockSpec((tm, D), lambda i: (i, mm_blk + 1)),
                  resident(pool_w.shape, lambda i: (0, 0, 0)),
                  resident((1, POOL_WIDTH), lambda i: (0, 0)),
                  resident(w_o_pool.shape, lambda i: (0, 0)),
                  resident(w_o_mla.shape, lambda i: (0, 0)),
                  resident(w_out.shape, lambda i: (0, 0)),
                  resident((1, D), lambda i: (0, 0))],
        out_specs=pl.BlockSpec((tm, D), lambda i: (i, 0)),
        out_shape=jax.ShapeDtypeStruct((T, D), F32),
        compiler_params=_params(("arbitrary",)),
        name="merge_out",
    )(xt, mod4, attn, zbig, zbig, zbig, zbig, zbig, zbig, zbig,
      pool_w, pool_scale.reshape(1, POOL_WIDTH), w_o_pool, w_o_mla, w_out, final_g.reshape(1, D))


def _split_up_proj(w_uq, w_ukv):
    wq = w_uq.reshape(Q_LORA, N_HEADS, QK_HEAD)
    nope = wq[:, :, :QK_NOPE].reshape(Q_LORA, N_HEADS * QK_NOPE)
    rope = wq[:, :, QK_NOPE:]
    rope_sw = jnp.concatenate([rope[:, :, QK_ROPE // 2:], rope[:, :, :QK_ROPE // 2]], axis=2)
    wq_t = jnp.concatenate([nope, rope.reshape(Q_LORA, -1), rope_sw.reshape(Q_LORA, -1)], axis=1).T
    wkv = w_ukv.reshape(KV_LORA, N_HEADS, QK_NOPE + V_HEAD)
    wk = wkv[:, :, :QK_NOPE].reshape(KV_LORA, N_HEADS * QK_NOPE)
    wv_t = wkv[:, :, QK_NOPE:].reshape(KV_LORA, MLA_WIDTH).T
    return wq_t.astype(BF16), wk.astype(BF16), wv_t.astype(BF16)


def kernel(x, c, positions, ada_w, ada_b, norm_g, w_in, q_norm_g, w_uq, kv_norm_g, w_ukv, w_o_mla,
           pool_w, pool_scale, w_o_pool, w_out, final_g):
    B, S, D = x.shape
    depth = ada_w.shape[0]
    inv_freq = 1.0 / (ROPE_THETA ** (jnp.arange(0, QK_ROPE, 2, dtype=F32) / QK_ROPE))
    inv_signed = jnp.concatenate([-inv_freq, inv_freq])
    posr = positions.reshape(B, 1, S)
    posc = positions.reshape(B, S, 1)
    q_scale = QK_HEAD ** -0.5 * math.log2(math.e)

    xt = x.reshape(B * S, D)
    for l in range(depth):
        mod4 = _adaln(c, ada_w[l], ada_b[l]).reshape(B, 3, 1, D)
        wq_t, wk, wv_t = _split_up_proj(w_uq[l], w_ukv[l])

        h, zs = _norm_proj(xt, mod4, norm_g[l], w_in[l], S)
        zbig = _gate_proj(h, w_in[l])
        qt, k, vt = _mla_prep(zs, posr, posc, inv_signed, q_norm_g[l], kv_norm_g[l],
                              wq_t, wk, wv_t, B, S, q_scale)
        attn = _attention(qt, k, vt, B, S).reshape(B * S, MLA_WIDTH)
        xt = _merge_out(xt, mod4, attn, zbig, pool_w[l].astype(BF16), pool_scale[l],
                        w_o_pool[l].astype(BF16), w_o_mla[l].astype(BF16), w_out[l].astype(BF16),
                        final_g, S, final_norm=(l == depth - 1))
    return xt.reshape(B, S, D)
```

```python
import functools
import math

import jax
import jax.numpy as jnp
from jax import lax
from jax.experimental import pallas as pl
from jax.experimental.pallas import tpu as pltpu

EPS = 1e-6
N_HEADS = 16
QK_NOPE = 128
QK_ROPE = 64
QK_HEAD = QK_NOPE + QK_ROPE
V_HEAD = 128
Q_LORA = 512
KV_LORA = 512
MLA_WIDTH = N_HEADS * V_HEAD
ROPE_THETA = 10000.0
POOL_WINDOWS = (2, 4, 8, 16)
POOL_GROUPS = len(POOL_WINDOWS)
POOL_GROUP_DIM = 256
POOL_WIDTH = POOL_GROUPS * POOL_GROUP_DIM
POOL_HALO = 16
SMALL_WIDTH = Q_LORA + KV_LORA + 2 * QK_ROPE

V7X_VMEM_LIMIT = 56 * 1024 * 1024
BF16_SUBLANES = 16

F32 = jnp.float32
BF16 = jnp.bfloat16
NT_DIMS = (((1,), (1,)), ((), ()))


def _sigmoid(v):
    return 1.0 / (1.0 + jnp.exp(-v))


def _params(semantics, vmem=V7X_VMEM_LIMIT, flags=None):
    return pltpu.CompilerParams(dimension_semantics=semantics, vmem_limit_bytes=vmem, flags=flags)


def _adaln_kernel(ct_ref, w_ref, b_ref, o_ref):
    w = w_ref[...]
    for b in range(ct_ref.shape[1]):
        cb = ct_ref[:, b:b + 1]
        act = cb * _sigmoid(cb)
        o_ref[b:b + 1, :] = jnp.sum(w * act, axis=0, keepdims=True) + b_ref[...]


def _adaln(c, w, bias):
    B, D = c.shape
    n = w.shape[1]
    tn = 512
    return pl.pallas_call(
        _adaln_kernel,
        grid=(n // tn,),
        in_specs=[pl.BlockSpec((D, B), lambda j: (0, 0)),
                  pl.BlockSpec((D, tn), lambda j: (0, j)),
                  pl.BlockSpec((1, tn), lambda j: (0, j))],
        out_specs=pl.BlockSpec((B, tn), lambda j: (0, j)),
        out_shape=jax.ShapeDtypeStruct((B, n), F32),
        compiler_params=_params(("arbitrary",)),
        name="adaln",
    )(c.T, w, bias.reshape(1, n))


def _norm_proj_kernel(x_ref, shift_ref, scale_ref, g_ref, ws_ref, h_ref, zs_ref, ws_bf16):
    @pl.when(pl.program_id(0) == 0)
    def _():
        ws_bf16[...] = ws_ref[...].astype(BF16)

    x = x_ref[...]
    y = x * lax.rsqrt(jnp.mean(x * x, axis=-1, keepdims=True) + EPS) * g_ref[...]
    h = (y * (1.0 + scale_ref[...]) + shift_ref[...]).astype(BF16)
    h_ref[...] = h
    zs_ref[...] = lax.dot_general(h, ws_bf16[...], NT_DIMS, preferred_element_type=F32)


def _norm_proj(xt, mod4, norm_g, w_in_t, seq):
    T, D = xt.shape
    tm = 512
    per_b = seq // tm
    return pl.pallas_call(
        _norm_proj_kernel,
        grid=(T // tm,),
        in_specs=[pl.BlockSpec((tm, D), lambda i: (i, 0)),
                  pl.BlockSpec((None, None, 1, D), lambda i: (i // per_b, 0, 0, 0)),
                  pl.BlockSpec((None, None, 1, D), lambda i: (i // per_b, 1, 0, 0)),
                  pl.BlockSpec((1, D), lambda i: (0, 0)),
                  pl.BlockSpec((SMALL_WIDTH, D), lambda i: (0, 0), pipeline_mode=pl.Buffered(1))],
        out_specs=[pl.BlockSpec((tm, D), lambda i: (i, 0)),
                   pl.BlockSpec((tm, SMALL_WIDTH), lambda i: (i, 0))],
        out_shape=[jax.ShapeDtypeStruct((T, D), BF16),
                   jax.ShapeDtypeStruct((T, SMALL_WIDTH), F32)],
        scratch_shapes=[pltpu.VMEM((SMALL_WIDTH, D), BF16)],
        compiler_params=_params(("arbitrary",)),
        name="norm_proj",
    )(xt, mod4, mod4, norm_g.reshape(1, D), w_in_t)


_ACTIVATIONS = {
    "silu": lambda a: a * _sigmoid(a),
    "linear": lambda a: a,
    "sigmoid": _sigmoid,
}


def _gate_proj_kernel(h_ref, wa_ref, wb_ref, o_ref, w_bf16, *, tile_kinds, shift):
    j = pl.program_id(0)
    tn = w_bf16.shape[0]

    @pl.when(pl.program_id(1) == 0)
    def _():
        w_bf16[0:tn - shift, :] = wa_ref[shift:, :].astype(BF16)
        w_bf16[tn - shift:, :] = wb_ref[...].astype(BF16)

    for kind, act in _ACTIVATIONS.items():
        tiles = [t for t, k in enumerate(tile_kinds) if k == kind]
        cond = functools.reduce(jnp.logical_or, [j == t for t in tiles])

        @pl.when(cond)
        def _(act=act):
            acc = lax.dot_general(h_ref[...], w_bf16[...], NT_DIMS, preferred_element_type=F32)
            o_ref[...] = act(acc).astype(o_ref.dtype)


def _gate_proj(h, w_in_t):
    T, D = h.shape
    tm, tn = 1024, 1024
    start = Q_LORA + KV_LORA + QK_ROPE
    n = w_in_t.shape[0] - start
    first_blk, shift = divmod(start, tn)
    assert n % tn == 0 and tn % shift == 0 and shift % BF16_SUBLANES == 0
    tile_kinds = (("silu",) * (MLA_WIDTH // tn) + ("linear",) * (POOL_WIDTH // tn)
                  + ("silu",) * (POOL_WIDTH // tn) + ("sigmoid",) * (2 * D // tn))
    kern = functools.partial(_gate_proj_kernel, tile_kinds=tile_kinds, shift=shift)
    return pl.pallas_call(
        kern,
        grid=(n // tn, T // tm),
        in_specs=[pl.BlockSpec((tm, D), lambda j, i: (i, 0)),
                  pl.BlockSpec((tn, D), lambda j, i: (first_blk + j, 0)),
                  pl.BlockSpec((shift, D), lambda j, i: ((first_blk + j + 1) * (tn // shift), 0))],
        out_specs=pl.BlockSpec((tm, tn), lambda j, i: (i, j)),
        out_shape=jax.ShapeDtypeStruct((T, n), BF16),
        scratch_shapes=[pltpu.VMEM((tn, D), BF16)],
        compiler_params=_params(("arbitrary", "arbitrary")),
        name="gate_proj",
    )(h, w_in_t, w_in_t)


def _mla_prep_kernel(zs_ref, posr_ref, posc_ref, invr_ref, invc_ref, qg_ref, kvg_ref,
                     wq_ref, wk_ref, wv_ref, qt_ref, k_ref, vt_ref, *, q_scale):
    def rms(v, g):
        return (v * lax.rsqrt(jnp.mean(v * v, axis=-1, keepdims=True) + EPS) * g).astype(BF16)

    cqn = rms(zs_ref[:, 0:Q_LORA], qg_ref[...])
    ckvn = rms(zs_ref[:, Q_LORA:Q_LORA + KV_LORA], kvg_ref[...])
    kr = zs_ref[:, Q_LORA + KV_LORA:Q_LORA + KV_LORA + QK_ROPE]
    kr_sw = jnp.concatenate([kr[:, QK_ROPE // 2:], kr[:, :QK_ROPE // 2]], axis=1)

    ang_t = invc_ref[...] * posr_ref[...].astype(F32)
    cos_t, sin_t = jnp.cos(ang_t), jnp.sin(ang_t)
    ang = posc_ref[...].astype(F32) * invr_ref[...]
    cos, sin = jnp.cos(ang), jnp.sin(ang)

    qf = lax.dot_general(wq_ref[...], cqn, NT_DIMS, preferred_element_type=F32)
    rope0 = N_HEADS * QK_NOPE
    swap0 = rope0 + N_HEADS * QK_ROPE
    for h in range(N_HEADS):
        qt_ref[h * QK_HEAD:h * QK_HEAD + QK_NOPE, :] = (
            qf[h * QK_NOPE:(h + 1) * QK_NOPE] * q_scale).astype(BF16)
        rot = (qf[rope0 + h * QK_ROPE:rope0 + (h + 1) * QK_ROPE] * cos_t
               + qf[swap0 + h * QK_ROPE:swap0 + (h + 1) * QK_ROPE] * sin_t)
        qt_ref[h * QK_HEAD + QK_NOPE:(h + 1) * QK_HEAD, :] = (rot * q_scale).astype(BF16)

    kn = jnp.dot(ckvn, wk_ref[...], preferred_element_type=F32)
    k_rot = (kr * cos + kr_sw * sin).astype(BF16)
    for h in range(N_HEADS):
        k_ref[h, :, 0:QK_NOPE] = kn[:, h * QK_NOPE:(h + 1) * QK_NOPE].astype(BF16)
        k_ref[h, :, QK_NOPE:QK_HEAD] = k_rot

    vt_ref[...] = lax.dot_general(wv_ref[...], ckvn, NT_DIMS,
                                  preferred_element_type=F32).astype(BF16)


def _mla_prep(zs, posr, posc, inv_signed, q_norm_g, kv_norm_g, wq_t, wk, wv_t, batch, seq, q_scale):
    tm = 256
    per_b = seq // tm
    const = lambda i: (0, 0)
    kern = functools.partial(_mla_prep_kernel, q_scale=q_scale)
    return pl.pallas_call(
        kern,
        grid=(batch * per_b,),
        in_specs=[pl.BlockSpec((tm, SMALL_WIDTH), lambda i: (i, 0)),
                  pl.BlockSpec((None, 1, tm), lambda i: (i // per_b, 0, i % per_b)),
                  pl.BlockSpec((None, tm, 1), lambda i: (i // per_b, i % per_b, 0)),
                  pl.BlockSpec((1, QK_ROPE), const),
                  pl.BlockSpec((QK_ROPE, 1), const),
                  pl.BlockSpec((1, Q_LORA), const),
                  pl.BlockSpec((1, KV_LORA), const),
                  pl.BlockSpec(wq_t.shape, const),
                  pl.BlockSpec(wk.shape, const),
                  pl.BlockSpec(wv_t.shape, const)],
        out_specs=[pl.BlockSpec((None, N_HEADS * QK_HEAD, tm), lambda i: (i // per_b, 0, i % per_b)),
                   pl.BlockSpec((None, N_HEADS, tm, QK_HEAD), lambda i: (i // per_b, 0, i % per_b, 0)),
                   pl.BlockSpec((None, MLA_WIDTH, tm), lambda i: (i // per_b, 0, i % per_b))],
        out_shape=[jax.ShapeDtypeStruct((batch, N_HEADS * QK_HEAD, seq), BF16),
                   jax.ShapeDtypeStruct((batch, N_HEADS, seq, QK_HEAD), BF16),
                   jax.ShapeDtypeStruct((batch, MLA_WIDTH, seq), BF16)],
        compiler_params=_params(("arbitrary",)),
        name="mla_prep",
    )(zs, posr, posc, inv_signed.reshape(1, QK_ROPE), inv_signed.reshape(QK_ROPE, 1),
      q_norm_g.reshape(1, Q_LORA), kv_norm_g.reshape(1, KV_LORA), wq_t, wk, wv_t)


def _attention_kernel(qt_ref, k_ref, vt_ref, o_ref, s_a, m_a, s_b, m_b, *, tq):
    t = pl.program_id(0)
    n_tiles = pl.num_programs(0) - 1
    nq = qt_ref.shape[1] // tq
    q_off = pl.multiple_of((t % nq) * tq, tq)
    o_off = pl.multiple_of(((t + nq - 1) % nq) * tq, tq)

    def scores(s_ref, m_ref):
        s = jnp.dot(k_ref[...], qt_ref[:, pl.ds(q_off, tq)], preferred_element_type=F32)
        s_ref[...] = s
        m_ref[...] = jnp.max(s, axis=0, keepdims=True)

    def finish(s_ref, m_ref):
        p = jnp.exp2(s_ref[...] - m_ref[...])
        l = jnp.sum(p, axis=0, keepdims=True)
        ot = jnp.dot(vt_ref[...], p.astype(BF16), preferred_element_type=F32)
        o_ref[pl.ds(o_off, tq), :] = (ot / l).T.astype(o_ref.dtype)

    steady = jnp.logical_and(t > 0, t < n_tiles)
    even = t % 2 == 0

    @pl.when(t == 0)
    def _():
        scores(s_a, m_a)

    @pl.when(jnp.logical_and(steady, even))
    def _():
        scores(s_a, m_a)
        finish(s_b, m_b)

    @pl.when(jnp.logical_and(steady, jnp.logical_not(even)))
    def _():
        scores(s_b, m_b)
        finish(s_a, m_a)

    @pl.when(jnp.logical_and(t == n_tiles, even))
    def _():
        finish(s_b, m_b)

    @pl.when(jnp.logical_and(t == n_tiles, jnp.logical_not(even)))
    def _():
        finish(s_a, m_a)


def _attention(qt, k, vt, batch, seq):
    tq = 256
    nq = seq // tq
    n_tiles = batch * N_HEADS * nq

    def head_of(tile):
        return tile // (N_HEADS * nq), (tile // nq) % N_HEADS

    def cur(t):
        return head_of(jnp.minimum(t, n_tiles - 1))

    def prev(t):
        return head_of(jnp.maximum(t - 1, 0))

    return pl.pallas_call(
        functools.partial(_attention_kernel, tq=tq),
        grid=(n_tiles + 1,),
        in_specs=[pl.BlockSpec((None, QK_HEAD, seq), lambda t: cur(t) + (0,)),
                  pl.BlockSpec((None, None, seq, QK_HEAD), lambda t: cur(t) + (0, 0)),
                  pl.BlockSpec((None, V_HEAD, seq), lambda t: prev(t) + (0,))],
        out_specs=pl.BlockSpec((None, seq, V_HEAD), lambda t: (prev(t)[0], 0, prev(t)[1])),
        out_shape=jax.ShapeDtypeStruct((batch, seq, MLA_WIDTH), BF16),
        scratch_shapes=[pltpu.VMEM((seq, tq), F32), pltpu.VMEM((1, tq), F32),
                        pltpu.VMEM((seq, tq), F32), pltpu.VMEM((1, tq), F32)],
        compiler_params=_params(("arbitrary",)),
        name="attention",
    )(qt, k, vt)


def _merge_out_kernel(x_ref, gate_ref, attn_ref, gm_ref, gp_ref, vp_ref, vprev_ref, vnext_ref,
                      mm_ref, mp_ref, pw_ref, ps_ref, wop_ref, wom_ref, wout_ref, fg_ref, o_ref,
                      *, seq, final_norm):
    tm = x_ref.shape[0]
    t0 = (pl.program_id(0) % (seq // tm)) * tm

    cur = vp_ref[...].astype(F32)
    prev = jnp.where(t0 > 0, vprev_ref[...].astype(F32), 0.0)
    nxt = jnp.where(t0 + tm < seq, vnext_ref[...].astype(F32), 0.0)
    ext = jnp.concatenate([prev, cur, nxt], axis=0)
    n_ext = tm + 2 * POOL_HALO
    tok = t0 + lax.broadcasted_iota(jnp.int32, (tm, 1), 0)
    mixed = []
    for g, w in enumerate(POOL_WINDOWS):
        acc = ext[:, g * POOL_GROUP_DIM:(g + 1) * POOL_GROUP_DIM]
        acc = acc + pltpu.roll(acc, 1, axis=0)
        half = 1
        while 2 * half < w:
            acc = pltpu.roll(acc, half, axis=0) + pltpu.roll(acc, n_ext - half, axis=0)
            half *= 2
        wsum = acc[POOL_HALO:POOL_HALO + tm]
        count = (jnp.minimum(tok + w // 2, seq) - jnp.maximum(tok - w // 2, 0)).astype(F32)
        pooled = wsum / count - cur[:, g * POOL_GROUP_DIM:(g + 1) * POOL_GROUP_DIM]
        mixed.append(jnp.dot(pooled.astype(BF16), pw_ref[g], preferred_element_type=F32))
    mixed = jnp.concatenate(mixed, axis=1)
    u = (mixed * ps_ref[...] * gp_ref[...].astype(F32)).astype(BF16)
    p_pool = jnp.dot(u, wop_ref[...], preferred_element_type=F32)

    gated = (attn_ref[...].astype(F32) * gm_ref[...].astype(F32)).astype(BF16)
    p_mla = jnp.dot(gated, wom_ref[...], preferred_element_type=F32)

    y = mm_ref[...].astype(F32) * p_mla + mp_ref[...].astype(F32) * p_pool
    r = jnp.dot(y.astype(BF16), wout_ref[...], preferred_element_type=F32)
    xo = x_ref[...] + gate_ref[...] * r
    if final_norm:
        xo = xo * lax.rsqrt(jnp.mean(xo * xo, axis=-1, keepdims=True) + EPS) * fg_ref[...]
    o_ref[...] = xo


def _merge_out(xt, mod4, attn, zbig, pool_w, pool_scale, w_o_pool, w_o_mla, w_out, final_g,
               seq, final_norm):
    T, D = xt.shape
    tm = 256
    per_b = seq // tm
    halo_per_tile = tm // POOL_HALO
    n_halo = T // POOL_HALO
    vp_blk = MLA_WIDTH // POOL_WIDTH
    gp_blk = vp_blk + 1
    mm_blk = (MLA_WIDTH + 2 * POOL_WIDTH) // D
    resident = functools.partial(pl.BlockSpec, pipeline_mode=pl.Buffered(1))
    kern = functools.partial(_merge_out_kernel, seq=seq, final_norm=final_norm)
    return pl.pallas_call(
        kern,
        grid=(T // tm,),
        in_specs=[pl.BlockSpec((tm, D), lambda i: (i, 0)),
                  pl.BlockSpec((None, None, 1, D), lambda i: (i // per_b, 2, 0, 0)),
                  pl.BlockSpec((tm, MLA_WIDTH), lambda i: (i, 0)),
                  pl.BlockSpec((tm, MLA_WIDTH), lambda i: (i, 0)),
                  pl.BlockSpec((tm, POOL_WIDTH), lambda i: (i, gp_blk)),
                  pl.BlockSpec((tm, POOL_WIDTH), lambda i: (i, vp_blk)),
                  pl.BlockSpec((POOL_HALO, POOL_WIDTH),
                               lambda i: (jnp.maximum(i * halo_per_tile - 1, 0), vp_blk)),
                  pl.BlockSpec((POOL_HALO, POOL_WIDTH),
                               lambda i: (jnp.minimum((i + 1) * halo_per_tile, n_halo - 1), vp_blk)),
                  pl.BlockSpec((tm, D), lambda i: (i, mm_blk)),
                  pl.BlockSpec((tm, D), lambda i: (i, mm_blk + 1)),
                  resident(pool_w.shape, lambda i: (0, 0, 0)),
                  resident((1, POOL_WIDTH), lambda i: (0, 0)),
                  resident(w_o_pool.shape, lambda i: (0, 0)),
                  resident(w_o_mla.shape, lambda i: (0, 0)),
                  resident(w_out.shape, lambda i: (0, 0)),
                  resident((1, D), lambda i: (0, 0))],
        out_specs=pl.BlockSpec((tm, D), lambda i: (i, 0)),
        out_shape=jax.ShapeDtypeStruct((T, D), F32),
        compiler_params=_params(("arbitrary",)),
        name="merge_out",
    )(xt, mod4, attn, zbig, zbig, zbig, zbig, zbig, zbig, zbig,
      pool_w, pool_scale.reshape(1, POOL_WIDTH), w_o_pool, w_o_mla, w_out, final_g.reshape(1, D))


def _split_up_proj(w_uq, w_ukv):
    wq = w_uq.reshape(Q_LORA, N_HEADS, QK_HEAD)
    nope = wq[:, :, :QK_NOPE].reshape(Q_LORA, N_HEADS * QK_NOPE)
    rope = wq[:, :, QK_NOPE:]
    rope_sw = jnp.concatenate([rope[:, :, QK_ROPE // 2:], rope[:, :, :QK_ROPE // 2]], axis=2)
    wq_t = jnp.concatenate([nope, rope.reshape(Q_LORA, -1), rope_sw.reshape(Q_LORA, -1)], axis=1).T
    wkv = w_ukv.reshape(KV_LORA, N_HEADS, QK_NOPE + V_HEAD)
    wk = wkv[:, :, :QK_NOPE].reshape(KV_LORA, N_HEADS * QK_NOPE)
    wv_t = wkv[:, :, QK_NOPE:].reshape(KV_LORA, MLA_WIDTH).T
    return wq_t.astype(BF16), wk.astype(BF16), wv_t.astype(BF16)


def kernel(x, c, positions, ada_w, ada_b, norm_g, w_in, q_norm_g, w_uq, kv_norm_g, w_ukv, w_o_mla,
           pool_w, pool_scale, w_o_pool, w_out, final_g):
    B, S, D = x.shape
    depth = ada_w.shape[0]
    inv_freq = 1.0 / (ROPE_THETA ** (jnp.arange(0, QK_ROPE, 2, dtype=F32) / QK_ROPE))
    inv_signed = jnp.concatenate([-inv_freq, inv_freq])
    posr = positions.reshape(B, 1, S)
    posc = positions.reshape(B, S, 1)
    q_scale = QK_HEAD ** -0.5 * math.log2(math.e)

    xt = x.reshape(B * S, D)
    for l in range(depth):
        mod4 = _adaln(c, ada_w[l], ada_b[l]).reshape(B, 3, 1, D)
        wq_t, wk, wv_t = _split_up_proj(w_uq[l], w_ukv[l])

        w_in_t = w_in[l].T
        h, zs = _norm_proj(xt, mod4, norm_g[l], w_in_t, S)
        zbig = _gate_proj(h, w_in_t)
        qt, k, vt = _mla_prep(zs, posr, posc, inv_signed, q_norm_g[l], kv_norm_g[l],
                              wq_t, wk, wv_t, B, S, q_scale)
        attn = _attention(qt, k, vt, B, S).reshape(B * S, MLA_WIDTH)
        xt = _merge_out(xt, mod4, attn, zbig, pool_w[l].astype(BF16), pool_scale[l],
                        w_o_pool[l].astype(BF16), w_o_mla[l].astype(BF16), w_out[l].astype(BF16),
                        final_g, S, final_norm=(l == depth - 1))
    return xt.reshape(B, S, D)
```

```python
import functools
import math

import jax
import jax.numpy as jnp
from jax import lax
from jax.experimental import pallas as pl
from jax.experimental.pallas import tpu as pltpu

EPS = 1e-6
N_HEADS = 16
QK_NOPE = 128
QK_ROPE = 64
QK_HEAD = QK_NOPE + QK_ROPE
V_HEAD = 128
Q_LORA = 512
KV_LORA = 512
MLA_WIDTH = N_HEADS * V_HEAD
ROPE_THETA = 10000.0
POOL_WINDOWS = (2, 4, 8, 16)
POOL_GROUPS = len(POOL_WINDOWS)
POOL_GROUP_DIM = 256
POOL_WIDTH = POOL_GROUPS * POOL_GROUP_DIM
POOL_HALO = 16
SMALL_WIDTH = Q_LORA + KV_LORA + 2 * QK_ROPE

V7X_VMEM_LIMIT = 56 * 1024 * 1024
BF16_SUBLANES = 16

F32 = jnp.float32
BF16 = jnp.bfloat16
NT_DIMS = (((1,), (1,)), ((), ()))


def _sigmoid(v):
    return 1.0 / (1.0 + jnp.exp(-v))


def _params(semantics, vmem=V7X_VMEM_LIMIT, flags=None):
    return pltpu.CompilerParams(dimension_semantics=semantics, vmem_limit_bytes=vmem, flags=flags)


def _adaln_kernel(ct_ref, w_ref, b_ref, o_ref):
    w = w_ref[...]
    for b in range(ct_ref.shape[1]):
        cb = ct_ref[:, b:b + 1]
        act = cb * _sigmoid(cb)
        o_ref[b:b + 1, :] = jnp.sum(w * act, axis=0, keepdims=True) + b_ref[...]


def _adaln(c, w, bias):
    B, D = c.shape
    n = w.shape[1]
    tn = 512
    return pl.pallas_call(
        _adaln_kernel,
        grid=(n // tn,),
        in_specs=[pl.BlockSpec((D, B), lambda j: (0, 0)),
                  pl.BlockSpec((D, tn), lambda j: (0, j)),
                  pl.BlockSpec((1, tn), lambda j: (0, j))],
        out_specs=pl.BlockSpec((B, tn), lambda j: (0, j)),
        out_shape=jax.ShapeDtypeStruct((B, n), F32),
        compiler_params=_params(("arbitrary",)),
        name="adaln",
    )(c.T, w, bias.reshape(1, n))


def _norm_proj_kernel(x_ref, shift_ref, scale_ref, g_ref, ws_ref, h_ref, zs_ref, ws_bf16):
    @pl.when(pl.program_id(0) == 0)
    def _():
        ws_bf16[...] = ws_ref[...].astype(BF16)

    x = x_ref[...]
    y = x * lax.rsqrt(jnp.mean(x * x, axis=-1, keepdims=True) + EPS) * g_ref[...]
    h = (y * (1.0 + scale_ref[...]) + shift_ref[...]).astype(BF16)
    h_ref[...] = h
    zs_ref[...] = lax.dot_general(h, ws_bf16[...], NT_DIMS, preferred_element_type=F32)


def _norm_proj(xt, mod4, norm_g, w_in_t, seq):
    T, D = xt.shape
    tm = 512
    per_b = seq // tm
    return pl.pallas_call(
        _norm_proj_kernel,
        grid=(T // tm,),
        in_specs=[pl.BlockSpec((tm, D), lambda i: (i, 0)),
                  pl.BlockSpec((None, None, 1, D), lambda i: (i // per_b, 0, 0, 0)),
                  pl.BlockSpec((None, None, 1, D), lambda i: (i // per_b, 1, 0, 0)),
                  pl.BlockSpec((1, D), lambda i: (0, 0)),
                  pl.BlockSpec((SMALL_WIDTH, D), lambda i: (0, 0), pipeline_mode=pl.Buffered(1))],
        out_specs=[pl.BlockSpec((tm, D), lambda i: (i, 0)),
                   pl.BlockSpec((tm, SMALL_WIDTH), lambda i: (i, 0))],
        out_shape=[jax.ShapeDtypeStruct((T, D), BF16),
                   jax.ShapeDtypeStruct((T, SMALL_WIDTH), F32)],
        scratch_shapes=[pltpu.VMEM((SMALL_WIDTH, D), BF16)],
        compiler_params=_params(("arbitrary",)),
        name="norm_proj",
    )(xt, mod4, mod4, norm_g.reshape(1, D), w_in_t)


_ACTIVATIONS = {
    "silu": lambda a: a * _sigmoid(a),
    "linear": lambda a: a,
    "sigmoid": _sigmoid,
}


def _gate_proj_kernel(h_ref, wa_ref, wb_ref, o_ref, w_bf16, *, tile_kinds, shift):
    j = pl.program_id(0)
    tn = w_bf16.shape[0]

    @pl.when(pl.program_id(1) == 0)
    def _():
        w_bf16[0:tn - shift, :] = wa_ref[shift:, :].astype(BF16)
        w_bf16[tn - shift:, :] = wb_ref[...].astype(BF16)

    for kind, act in _ACTIVATIONS.items():
        tiles = [t for t, k in enumerate(tile_kinds) if k == kind]
        cond = functools.reduce(jnp.logical_or, [j == t for t in tiles])

        @pl.when(cond)
        def _(act=act):
            acc = lax.dot_general(h_ref[...], w_bf16[...], NT_DIMS, preferred_element_type=F32)
            o_ref[...] = act(acc).astype(o_ref.dtype)


def _gate_proj(h, w_in_t):
    T, D = h.shape
    tm, tn = 1024, 1024
    start = Q_LORA + KV_LORA + QK_ROPE
    n = w_in_t.shape[0] - start
    first_blk, shift = divmod(start, tn)
    assert n % tn == 0 and tn % shift == 0 and shift % BF16_SUBLANES == 0
    tile_kinds = (("silu",) * (MLA_WIDTH // tn) + ("linear",) * (POOL_WIDTH // tn)
                  + ("silu",) * (POOL_WIDTH // tn) + ("sigmoid",) * (2 * D // tn))
    kern = functools.partial(_gate_proj_kernel, tile_kinds=tile_kinds, shift=shift)
    return pl.pallas_call(
        kern,
        grid=(n // tn, T // tm),
        in_specs=[pl.BlockSpec((tm, D), lambda j, i: (i, 0)),
                  pl.BlockSpec((tn, D), lambda j, i: (first_blk + j, 0)),
                  pl.BlockSpec((shift, D), lambda j, i: ((first_blk + j + 1) * (tn // shift), 0))],
        out_specs=pl.BlockSpec((tm, tn), lambda j, i: (i, j)),
        out_shape=jax.ShapeDtypeStruct((T, n), BF16),
        scratch_shapes=[pltpu.VMEM((tn, D), BF16)],
        compiler_params=_params(("arbitrary", "arbitrary")),
        name="gate_proj",
    )(h, w_in_t, w_in_t)


def _mla_prep_kernel(zs_ref, posr_ref, posc_ref, invr_ref, invc_ref, qg_ref, kvg_ref,
                     wq_ref, wk_ref, wv_ref, qt_ref, k_ref, vt_ref, *, q_scale):
    def rms(v, g):
        return (v * lax.rsqrt(jnp.mean(v * v, axis=-1, keepdims=True) + EPS) * g).astype(BF16)

    cqn = rms(zs_ref[:, 0:Q_LORA], qg_ref[...])
    ckvn = rms(zs_ref[:, Q_LORA:Q_LORA + KV_LORA], kvg_ref[...])
    kr = zs_ref[:, Q_LORA + KV_LORA:Q_LORA + KV_LORA + QK_ROPE]
    kr_sw = jnp.concatenate([kr[:, QK_ROPE // 2:], kr[:, :QK_ROPE // 2]], axis=1)

    ang_t = invc_ref[...] * posr_ref[...].astype(F32)
    cos_t, sin_t = jnp.cos(ang_t), jnp.sin(ang_t)
    ang = posc_ref[...].astype(F32) * invr_ref[...]
    cos, sin = jnp.cos(ang), jnp.sin(ang)

    qf = lax.dot_general(wq_ref[...], cqn, NT_DIMS, preferred_element_type=F32)
    rope0 = N_HEADS * QK_NOPE
    swap0 = rope0 + N_HEADS * QK_ROPE
    for h in range(N_HEADS):
        qt_ref[h * QK_HEAD:h * QK_HEAD + QK_NOPE, :] = (
            qf[h * QK_NOPE:(h + 1) * QK_NOPE] * q_scale).astype(BF16)
        rot = (qf[rope0 + h * QK_ROPE:rope0 + (h + 1) * QK_ROPE] * cos_t
               + qf[swap0 + h * QK_ROPE:swap0 + (h + 1) * QK_ROPE] * sin_t)
        qt_ref[h * QK_HEAD + QK_NOPE:(h + 1) * QK_HEAD, :] = (rot * q_scale).astype(BF16)

    kn = jnp.dot(ckvn, wk_ref[...], preferred_element_type=F32)
    k_rot = (kr * cos + kr_sw * sin).astype(BF16)
    for h in range(N_HEADS):
        k_ref[h, :, 0:QK_NOPE] = kn[:, h * QK_NOPE:(h + 1) * QK_NOPE].astype(BF16)
        k_ref[h, :, QK_NOPE:QK_HEAD] = k_rot

    vt_ref[...] = lax.dot_general(wv_ref[...], ckvn, NT_DIMS,
                                  preferred_element_type=F32).astype(BF16)


def _mla_prep(zs, posr, posc, inv_signed, q_norm_g, kv_norm_g, wq_t, wk, wv_t, batch, seq, q_scale):
    tm = 256
    per_b = seq // tm
    const = lambda i: (0, 0)
    kern = functools.partial(_mla_prep_kernel, q_scale=q_scale)
    return pl.pallas_call(
        kern,
        grid=(batch * per_b,),
        in_specs=[pl.BlockSpec((tm, SMALL_WIDTH), lambda i: (i, 0)),
                  pl.BlockSpec((None, 1, tm), lambda i: (i // per_b, 0, i % per_b)),
                  pl.BlockSpec((None, tm, 1), lambda i: (i // per_b, i % per_b, 0)),
                  pl.BlockSpec((1, QK_ROPE), const),
                  pl.BlockSpec((QK_ROPE, 1), const),
                  pl.BlockSpec((1, Q_LORA), const),
                  pl.BlockSpec((1, KV_LORA), const),
                  pl.BlockSpec(wq_t.shape, const),
                  pl.BlockSpec(wk.shape, const),
                  pl.BlockSpec(wv_t.shape, const)],
        out_specs=[pl.BlockSpec((None, N_HEADS * QK_HEAD, tm), lambda i: (i // per_b, 0, i % per_b)),
                   pl.BlockSpec((None, N_HEADS, tm, QK_HEAD), lambda i: (i // per_b, 0, i % per_b, 0)),
                   pl.BlockSpec((None, MLA_WIDTH, tm), lambda i: (i // per_b, 0, i % per_b))],
        out_shape=[jax.ShapeDtypeStruct((batch, N_HEADS * QK_HEAD, seq), BF16),
                   jax.ShapeDtypeStruct((batch, N_HEADS, seq, QK_HEAD), BF16),
                   jax.ShapeDtypeStruct((batch, MLA_WIDTH, seq), BF16)],
        compiler_params=_params(("arbitrary",)),
        name="mla_prep",
    )(zs, posr, posc, inv_signed.reshape(1, QK_ROPE), inv_signed.reshape(QK_ROPE, 1),
      q_norm_g.reshape(1, Q_LORA), kv_norm_g.reshape(1, KV_LORA), wq_t, wk, wv_t)


def _attention_kernel(qt_ref, k_ref, vt_ref, o_ref, s_a, m_a, s_b, m_b, *, tq):
    nq = qt_ref.shape[1] // tq

    def scores(i, s_ref, m_ref):
        off = pl.multiple_of(i * tq, tq)
        s = jnp.dot(k_ref[...], qt_ref[:, pl.ds(off, tq)], preferred_element_type=F32)
        s_ref[...] = s
        m_ref[...] = jnp.max(s, axis=0, keepdims=True)

    def finish(i, s_ref, m_ref):
        off = pl.multiple_of(i * tq, tq)
        p = jnp.exp2(s_ref[...] - m_ref[...])
        l = jnp.sum(p, axis=0, keepdims=True)
        ot = jnp.dot(vt_ref[...], p.astype(BF16), preferred_element_type=F32)
        o_ref[pl.ds(off, tq), :] = (ot / l).T.astype(o_ref.dtype)

    scores(0, s_a, m_a)

    def step(i, carry):
        @pl.when(i % 2 == 1)
        def _():
            scores(i, s_b, m_b)
            finish(i - 1, s_a, m_a)

        @pl.when(i % 2 == 0)
        def _():
            scores(i, s_a, m_a)
            finish(i - 1, s_b, m_b)

        return carry

    lax.fori_loop(1, nq, step, 0)
    finish(nq - 1, s_b, m_b)


def _attention(qt, k, vt, batch, seq):
    tq = 256
    assert seq % (2 * tq) == 0
    return pl.pallas_call(
        functools.partial(_attention_kernel, tq=tq),
        grid=(batch, N_HEADS),
        in_specs=[pl.BlockSpec((None, QK_HEAD, seq), lambda b, h: (b, h, 0)),
                  pl.BlockSpec((None, None, seq, QK_HEAD), lambda b, h: (b, h, 0, 0)),
                  pl.BlockSpec((None, V_HEAD, seq), lambda b, h: (b, h, 0))],
        out_specs=pl.BlockSpec((None, seq, V_HEAD), lambda b, h: (b, 0, h)),
        out_shape=jax.ShapeDtypeStruct((batch, seq, MLA_WIDTH), BF16),
        scratch_shapes=[pltpu.VMEM((seq, tq), F32), pltpu.VMEM((1, tq), F32),
                        pltpu.VMEM((seq, tq), F32), pltpu.VMEM((1, tq), F32)],
        compiler_params=_params(("arbitrary", "arbitrary")),
        name="attention",
    )(qt, k, vt)


def _merge_out_kernel(x_ref, gate_ref, attn_ref, gm_ref, gp_ref, vp_ref, vprev_ref, vnext_ref,
                      mm_ref, mp_ref, pw_ref, ps_ref, wop_ref, wom_ref, wout_ref, fg_ref, o_ref,
                      *, seq, final_norm):
    tm = x_ref.shape[0]
    t0 = (pl.program_id(0) % (seq // tm)) * tm

    cur = vp_ref[...].astype(F32)
    prev = jnp.where(t0 > 0, vprev_ref[...].astype(F32), 0.0)
    nxt = jnp.where(t0 + tm < seq, vnext_ref[...].astype(F32), 0.0)
    ext = jnp.concatenate([prev, cur, nxt], axis=0)
    n_ext = tm + 2 * POOL_HALO
    tok = t0 + lax.broadcasted_iota(jnp.int32, (tm, 1), 0)
    mixed = []
    for g, w in enumerate(POOL_WINDOWS):
        acc = ext[:, g * POOL_GROUP_DIM:(g + 1) * POOL_GROUP_DIM]
        acc = acc + pltpu.roll(acc, 1, axis=0)
        half = 1
        while 2 * half < w:
            acc = pltpu.roll(acc, half, axis=0) + pltpu.roll(acc, n_ext - half, axis=0)
            half *= 2
        wsum = acc[POOL_HALO:POOL_HALO + tm]
        count = (jnp.minimum(tok + w // 2, seq) - jnp.maximum(tok - w // 2, 0)).astype(F32)
        pooled = wsum / count - cur[:, g * POOL_GROUP_DIM:(g + 1) * POOL_GROUP_DIM]
        mixed.append(jnp.dot(pooled.astype(BF16), pw_ref[g], preferred_element_type=F32))
    mixed = jnp.concatenate(mixed, axis=1)
    u = (mixed * ps_ref[...] * gp_ref[...].astype(F32)).astype(BF16)
    p_pool = jnp.dot(u, wop_ref[...], preferred_element_type=F32)

    gated = (attn_ref[...].astype(F32) * gm_ref[...].astype(F32)).astype(BF16)
    p_mla = jnp.dot(gated, wom_ref[...], preferred_element_type=F32)

    y = mm_ref[...].astype(F32) * p_mla + mp_ref[...].astype(F32) * p_pool
    r = jnp.dot(y.astype(BF16), wout_ref[...], preferred_element_type=F32)
    xo = x_ref[...] + gate_ref[...] * r
    if final_norm:
        xo = xo * lax.rsqrt(jnp.mean(xo * xo, axis=-1, keepdims=True) + EPS) * fg_ref[...]
    o_ref[...] = xo


def _merge_out(xt, mod4, attn, zbig, pool_w, pool_scale, w_o_pool, w_o_mla, w_out, final_g,
               seq, final_norm):
    T, D = xt.shape
    tm = 256
    per_b = seq // tm
    halo_per_tile = tm // POOL_HALO
    n_halo = T // POOL_HALO
    vp_blk = MLA_WIDTH // POOL_WIDTH
    gp_blk = vp_blk + 1
    mm_blk = (MLA_WIDTH + 2 * POOL_WIDTH) // D
    resident = functools.partial(pl.BlockSpec, pipeline_mode=pl.Buffered(1))
    kern = functools.partial(_merge_out_kernel, seq=seq, final_norm=final_norm)
    return pl.pallas_call(
        kern,
        grid=(T // tm,),
        in_specs=[pl.BlockSpec((tm, D), lambda i: (i, 0)),
                  pl.BlockSpec((None, None, 1, D), lambda i: (i // per_b, 2, 0, 0)),
                  pl.BlockSpec((tm, MLA_WIDTH), lambda i: (i, 0)),
                  pl.BlockSpec((tm, MLA_WIDTH), lambda i: (i, 0)),
                  pl.BlockSpec((tm, POOL_WIDTH), lambda i: (i, gp_blk)),
                  pl.BlockSpec((tm, POOL_WIDTH), lambda i: (i, vp_blk)),
                  pl.BlockSpec((POOL_HALO, POOL_WIDTH),
                               lambda i: (jnp.maximum(i * halo_per_tile - 1, 0), vp_blk)),
                  pl.BlockSpec((POOL_HALO, POOL_WIDTH),
                               lambda i: (jnp.minimum((i + 1) * halo_per_tile, n_halo - 1), vp_blk)),
                  pl.BlockSpec((tm, D), lambda i: (i, mm_blk)),
                  pl.BlockSpec((tm, D), lambda i: (i, mm_blk + 1)),
                  resident(pool_w.shape, lambda i: (0, 0, 0)),
                  resident((1, POOL_WIDTH), lambda i: (0, 0)),
                  resident(w_o_pool.shape, lambda i: (0, 0)),
                  resident(w_o_mla.shape, lambda i: (0, 0)),
                  resident(w_out.shape, lambda i: (0, 0)),
                  resident((1, D), lambda i: (0, 0))],
        out_specs=pl.BlockSpec((tm, D), lambda i: (i, 0)),
        out_shape=jax.ShapeDtypeStruct((T, D), F32),
        compiler_params=_params(("arbitrary",)),
        name="merge_out",
    )(xt, mod4, attn, zbig, zbig, zbig, zbig, zbig, zbig, zbig,
      pool_w, pool_scale.reshape(1, POOL_WIDTH), w_o_pool, w_o_mla, w_out, final_g.reshape(1, D))


def _split_up_proj(w_uq, w_ukv):
    wq = w_uq.reshape(Q_LORA, N_HEADS, QK_HEAD)
    nope = wq[:, :, :QK_NOPE].reshape(Q_LORA, N_HEADS * QK_NOPE)
    rope = wq[:, :, QK_NOPE:]
    rope_sw = jnp.concatenate([rope[:, :, QK_ROPE // 2:], rope[:, :, :QK_ROPE // 2]], axis=2)
    wq_t = jnp.concatenate([nope, rope.reshape(Q_LORA, -1), rope_sw.reshape(Q_LORA, -1)], axis=1).T
    wkv = w_ukv.reshape(KV_LORA, N_HEADS, QK_NOPE + V_HEAD)
    wk = wkv[:, :, :QK_NOPE].reshape(KV_LORA, N_HEADS * QK_NOPE)
    wv_t = wkv[:, :, QK_NOPE:].reshape(KV_LORA, MLA_WIDTH).T
    return wq_t.astype(BF16), wk.astype(BF16), wv_t.astype(BF16)


def kernel(x, c, positions, ada_w, ada_b, norm_g, w_in, q_norm_g, w_uq, kv_norm_g, w_ukv, w_o_mla,
           pool_w, pool_scale, w_o_pool, w_out, final_g):
    B, S, D = x.shape
    depth = ada_w.shape[0]
    inv_freq = 1.0 / (ROPE_THETA ** (jnp.arange(0, QK_ROPE, 2, dtype=F32) / QK_ROPE))
    inv_signed = jnp.concatenate([-inv_freq, inv_freq])
    posr = positions.reshape(B, 1, S)
    posc = positions.reshape(B, S, 1)
    q_scale = QK_HEAD ** -0.5 * math.log2(math.e)

    xt = x.reshape(B * S, D)
    for l in range(depth):
        mod4 = _adaln(c, ada_w[l], ada_b[l]).reshape(B, 3, 1, D)
        wq_t, wk, wv_t = _split_up_proj(w_uq[l], w_ukv[l])

        w_in_t = w_in[l].T
        h, zs = _norm_proj(xt, mod4, norm_g[l], w_in_t, S)
        zbig = _gate_proj(h, w_in_t)
        qt, k, vt = _mla_prep(zs, posr, posc, inv_signed, q_norm_g[l], kv_norm_g[l],
                              wq_t, wk, wv_t, B, S, q_scale)
        attn = _attention(qt, k, vt, B, S).reshape(B * S, MLA_WIDTH)
        xt = _merge_out(xt, mod4, attn, zbig, pool_w[l].astype(BF16), pool_scale[l],
                        w_o_pool[l].astype(BF16), w_o_mla[l].astype(BF16), w_out[l].astype(BF16),
                        final_g, S, final_norm=(l == depth - 1))
    return xt.reshape(B, S, D)
```

```python
import functools
import math

import jax
import jax.numpy as jnp
from jax import lax
from jax.experimental import pallas as pl
from jax.experimental.pallas import tpu as pltpu

EPS = 1e-6
N_HEADS = 16
QK_NOPE = 128
QK_ROPE = 64
QK_HEAD = QK_NOPE + QK_ROPE
V_HEAD = 128
Q_LORA = 512
KV_LORA = 512
MLA_WIDTH = N_HEADS * V_HEAD
ROPE_THETA = 10000.0
POOL_WINDOWS = (2, 4, 8, 16)
POOL_GROUPS = len(POOL_WINDOWS)
POOL_GROUP_DIM = 256
POOL_WIDTH = POOL_GROUPS * POOL_GROUP_DIM
POOL_HALO = 16
SMALL_WIDTH = Q_LORA + KV_LORA + 2 * QK_ROPE

V7X_VMEM_LIMIT = 56 * 1024 * 1024
BF16_SUBLANES = 16
MIN_SOFTMAX_MASS = 2.0 ** -60

F32 = jnp.float32
BF16 = jnp.bfloat16
NT_DIMS = (((1,), (1,)), ((), ()))


def _sigmoid(v):
    return 1.0 / (1.0 + jnp.exp(-v))


def _params(semantics, vmem=V7X_VMEM_LIMIT, flags=None):
    return pltpu.CompilerParams(dimension_semantics=semantics, vmem_limit_bytes=vmem, flags=flags)


def _adaln_kernel(ct_ref, w_ref, b_ref, o_ref):
    w = w_ref[...]
    for b in range(ct_ref.shape[1]):
        cb = ct_ref[:, b:b + 1]
        act = cb * _sigmoid(cb)
        o_ref[b:b + 1, :] = jnp.sum(w * act, axis=0, keepdims=True) + b_ref[...]


def _adaln(c, w, bias):
    B, D = c.shape
    n = w.shape[1]
    tn = 512
    return pl.pallas_call(
        _adaln_kernel,
        grid=(n // tn,),
        in_specs=[pl.BlockSpec((D, B), lambda j: (0, 0)),
                  pl.BlockSpec((D, tn), lambda j: (0, j)),
                  pl.BlockSpec((1, tn), lambda j: (0, j))],
        out_specs=pl.BlockSpec((B, tn), lambda j: (0, j)),
        out_shape=jax.ShapeDtypeStruct((B, n), F32),
        compiler_params=_params(("arbitrary",)),
        name="adaln",
    )(c.T, w, bias.reshape(1, n))


def _norm_proj_kernel(x_ref, shift_ref, scale_ref, g_ref, ws_ref, h_ref, zs_ref, ws_bf16):
    @pl.when(pl.program_id(0) == 0)
    def _():
        ws_bf16[...] = ws_ref[...].astype(BF16)

    x = x_ref[...]
    y = x * lax.rsqrt(jnp.mean(x * x, axis=-1, keepdims=True) + EPS) * g_ref[...]
    h = (y * (1.0 + scale_ref[...]) + shift_ref[...]).astype(BF16)
    h_ref[...] = h
    zs_ref[...] = lax.dot_general(h, ws_bf16[...], NT_DIMS, preferred_element_type=F32)


def _norm_proj(xt, mod4, norm_g, w_in_t, seq):
    T, D = xt.shape
    tm = 512
    per_b = seq // tm
    return pl.pallas_call(
        _norm_proj_kernel,
        grid=(T // tm,),
        in_specs=[pl.BlockSpec((tm, D), lambda i: (i, 0)),
                  pl.BlockSpec((None, None, 1, D), lambda i: (i // per_b, 0, 0, 0)),
                  pl.BlockSpec((None, None, 1, D), lambda i: (i // per_b, 1, 0, 0)),
                  pl.BlockSpec((1, D), lambda i: (0, 0)),
                  pl.BlockSpec((SMALL_WIDTH, D), lambda i: (0, 0), pipeline_mode=pl.Buffered(1))],
        out_specs=[pl.BlockSpec((tm, D), lambda i: (i, 0)),
                   pl.BlockSpec((tm, SMALL_WIDTH), lambda i: (i, 0))],
        out_shape=[jax.ShapeDtypeStruct((T, D), BF16),
                   jax.ShapeDtypeStruct((T, SMALL_WIDTH), F32)],
        scratch_shapes=[pltpu.VMEM((SMALL_WIDTH, D), BF16)],
        compiler_params=_params(("arbitrary",)),
        name="norm_proj",
    )(xt, mod4, mod4, norm_g.reshape(1, D), w_in_t)


_ACTIVATIONS = {
    "silu": lambda a: a * _sigmoid(a),
    "linear": lambda a: a,
    "sigmoid": _sigmoid,
}


def _gate_proj_kernel(h_ref, wa_ref, wb_ref, o_ref, w_bf16, *, tile_kinds, shift):
    j = pl.program_id(0)
    tn = w_bf16.shape[0]

    @pl.when(pl.program_id(1) == 0)
    def _():
        w_bf16[0:tn - shift, :] = wa_ref[shift:, :].astype(BF16)
        w_bf16[tn - shift:, :] = wb_ref[...].astype(BF16)

    for kind, act in _ACTIVATIONS.items():
        tiles = [t for t, k in enumerate(tile_kinds) if k == kind]
        cond = functools.reduce(jnp.logical_or, [j == t for t in tiles])

        @pl.when(cond)
        def _(act=act):
            acc = lax.dot_general(h_ref[...], w_bf16[...], NT_DIMS, preferred_element_type=F32)
            o_ref[...] = act(acc).astype(o_ref.dtype)


def _gate_proj(h, w_in_t):
    T, D = h.shape
    tm, tn = 1024, 1024
    start = Q_LORA + KV_LORA + QK_ROPE
    n = w_in_t.shape[0] - start
    first_blk, shift = divmod(start, tn)
    assert n % tn == 0 and tn % shift == 0 and shift % BF16_SUBLANES == 0
    tile_kinds = (("silu",) * (MLA_WIDTH // tn) + ("linear",) * (POOL_WIDTH // tn)
                  + ("silu",) * (POOL_WIDTH // tn) + ("sigmoid",) * (2 * D // tn))
    kern = functools.partial(_gate_proj_kernel, tile_kinds=tile_kinds, shift=shift)
    return pl.pallas_call(
        kern,
        grid=(n // tn, T // tm),
        in_specs=[pl.BlockSpec((tm, D), lambda j, i: (i, 0)),
                  pl.BlockSpec((tn, D), lambda j, i: (first_blk + j, 0)),
                  pl.BlockSpec((shift, D), lambda j, i: ((first_blk + j + 1) * (tn // shift), 0))],
        out_specs=pl.BlockSpec((tm, tn), lambda j, i: (i, j)),
        out_shape=jax.ShapeDtypeStruct((T, n), BF16),
        scratch_shapes=[pltpu.VMEM((tn, D), BF16)],
        compiler_params=_params(("arbitrary", "arbitrary")),
        name="gate_proj",
    )(h, w_in_t, w_in_t)


def _mla_prep_kernel(zs_ref, posr_ref, posc_ref, invr_ref, invc_ref, qg_ref, kvg_ref,
                     wq_ref, wk_ref, wv_ref, qt_ref, k_ref, vt_ref, *, q_scale):
    def rms(v, g):
        return (v * lax.rsqrt(jnp.mean(v * v, axis=-1, keepdims=True) + EPS) * g).astype(BF16)

    cqn = rms(zs_ref[:, 0:Q_LORA], qg_ref[...])
    ckvn = rms(zs_ref[:, Q_LORA:Q_LORA + KV_LORA], kvg_ref[...])
    kr = zs_ref[:, Q_LORA + KV_LORA:Q_LORA + KV_LORA + QK_ROPE]
    kr_sw = jnp.concatenate([kr[:, QK_ROPE // 2:], kr[:, :QK_ROPE // 2]], axis=1)

    ang_t = invc_ref[...] * posr_ref[...].astype(F32)
    cos_t, sin_t = jnp.cos(ang_t), jnp.sin(ang_t)
    ang = posc_ref[...].astype(F32) * invr_ref[...]
    cos, sin = jnp.cos(ang), jnp.sin(ang)

    qf = lax.dot_general(wq_ref[...], cqn, NT_DIMS, preferred_element_type=F32)
    rope0 = N_HEADS * QK_NOPE
    swap0 = rope0 + N_HEADS * QK_ROPE
    for h in range(N_HEADS):
        qt_ref[h * QK_HEAD:h * QK_HEAD + QK_NOPE, :] = (
            qf[h * QK_NOPE:(h + 1) * QK_NOPE] * q_scale).astype(BF16)
        rot = (qf[rope0 + h * QK_ROPE:rope0 + (h + 1) * QK_ROPE] * cos_t
               + qf[swap0 + h * QK_ROPE:swap0 + (h + 1) * QK_ROPE] * sin_t)
        qt_ref[h * QK_HEAD + QK_NOPE:(h + 1) * QK_HEAD, :] = (rot * q_scale).astype(BF16)

    kn = jnp.dot(ckvn, wk_ref[...], preferred_element_type=F32)
    k_rot = (kr * cos + kr_sw * sin).astype(BF16)
    for h in range(N_HEADS):
        k_ref[h, :, 0:QK_NOPE] = kn[:, h * QK_NOPE:(h + 1) * QK_NOPE].astype(BF16)
        k_ref[h, :, QK_NOPE:QK_HEAD] = k_rot

    vt_ref[...] = lax.dot_general(wv_ref[...], ckvn, NT_DIMS,
                                  preferred_element_type=F32).astype(BF16)


def _mla_prep(zs, posr, posc, inv_signed, q_norm_g, kv_norm_g, wq_t, wk, wv_t, batch, seq, q_scale):
    tm = 256
    per_b = seq // tm
    const = lambda i: (0, 0)
    kern = functools.partial(_mla_prep_kernel, q_scale=q_scale)
    return pl.pallas_call(
        kern,
        grid=(batch * per_b,),
        in_specs=[pl.BlockSpec((tm, SMALL_WIDTH), lambda i: (i, 0)),
                  pl.BlockSpec((None, 1, tm), lambda i: (i // per_b, 0, i % per_b)),
                  pl.BlockSpec((None, tm, 1), lambda i: (i // per_b, i % per_b, 0)),
                  pl.BlockSpec((1, QK_ROPE), const),
                  pl.BlockSpec((QK_ROPE, 1), const),
                  pl.BlockSpec((1, Q_LORA), const),
                  pl.BlockSpec((1, KV_LORA), const),
                  pl.BlockSpec(wq_t.shape, const),
                  pl.BlockSpec(wk.shape, const),
                  pl.BlockSpec(wv_t.shape, const)],
        out_specs=[pl.BlockSpec((None, N_HEADS * QK_HEAD, tm), lambda i: (i // per_b, 0, i % per_b)),
                   pl.BlockSpec((None, N_HEADS, tm, QK_HEAD), lambda i: (i // per_b, 0, i % per_b, 0)),
                   pl.BlockSpec((None, MLA_WIDTH, tm), lambda i: (i // per_b, 0, i % per_b))],
        out_shape=[jax.ShapeDtypeStruct((batch, N_HEADS * QK_HEAD, seq), BF16),
                   jax.ShapeDtypeStruct((batch, N_HEADS, seq, QK_HEAD), BF16),
                   jax.ShapeDtypeStruct((batch, MLA_WIDTH, seq), BF16)],
        compiler_params=_params(("arbitrary",)),
        name="mla_prep",
    )(zs, posr, posc, inv_signed.reshape(1, QK_ROPE), inv_signed.reshape(QK_ROPE, 1),
      q_norm_g.reshape(1, Q_LORA), kv_norm_g.reshape(1, KV_LORA), wq_t, wk, wv_t)


def _attention_kernel(qt_ref, k_ref, vt_ref, o_ref, s_a, m_a, s_b, m_b, lmin_ref):
    tq = lmin_ref.shape[1]
    nq = qt_ref.shape[1] // tq

    kf = k_ref[...].astype(F32)
    k_norm = jnp.sqrt(jnp.max(jnp.sum(kf * kf, axis=1, keepdims=True), axis=0, keepdims=True))
    lmin_ref[...] = jnp.full(lmin_ref.shape, jnp.inf, F32)

    def fast_tile(i, carry):
        off = pl.multiple_of(i * tq, tq)
        qt = qt_ref[:, pl.ds(off, tq)]
        qf = qt.astype(F32)
        shift = jnp.sqrt(jnp.sum(qf * qf, axis=0, keepdims=True)) * k_norm
        s = jnp.dot(k_ref[...], qt, preferred_element_type=F32)
        p = jnp.exp2(s - shift)
        l = jnp.sum(p, axis=0, keepdims=True)
        ot = jnp.dot(vt_ref[...], p.astype(BF16), preferred_element_type=F32)
        o_ref[pl.ds(off, tq), :] = (ot / l).T.astype(o_ref.dtype)
        lmin_ref[...] = jnp.minimum(lmin_ref[...], l)
        return carry

    lax.fori_loop(0, nq, fast_tile, 0, unroll=4)
    trusted = jnp.min(lmin_ref[...]) >= MIN_SOFTMAX_MASS

    @pl.when(jnp.logical_not(trusted))
    def _():
        _attention_exact(qt_ref, k_ref, vt_ref, o_ref, s_a, m_a, s_b, m_b)


def _attention_exact(qt_ref, k_ref, vt_ref, o_ref, s_a, m_a, s_b, m_b):
    tq = s_a.shape[1]
    nq = qt_ref.shape[1] // tq

    def scores(i, s_ref, m_ref):
        off = pl.multiple_of(i * tq, tq)
        s = jnp.dot(k_ref[...], qt_ref[:, pl.ds(off, tq)], preferred_element_type=F32)
        s_ref[...] = s
        m_ref[...] = jnp.max(s, axis=0, keepdims=True)

    def finish(i, s_ref, m_ref):
        off = pl.multiple_of(i * tq, tq)
        p = jnp.exp2(s_ref[...] - m_ref[...])
        l = jnp.sum(p, axis=0, keepdims=True)
        ot = jnp.dot(vt_ref[...], p.astype(BF16), preferred_element_type=F32)
        o_ref[pl.ds(off, tq), :] = (ot / l).T.astype(o_ref.dtype)

    scores(0, s_a, m_a)

    def step(i, carry):
        @pl.when(i % 2 == 1)
        def _():
            scores(i, s_b, m_b)
            finish(i - 1, s_a, m_a)

        @pl.when(i % 2 == 0)
        def _():
            scores(i, s_a, m_a)
            finish(i - 1, s_b, m_b)

        return carry

    lax.fori_loop(1, nq, step, 0)
    finish(nq - 1, s_b, m_b)


def _attention(qt, k, vt, batch, seq):
    tq = 256
    tq_fast = 512
    assert seq % (2 * tq) == 0 and seq % tq_fast == 0
    return pl.pallas_call(
        _attention_kernel,
        grid=(batch, N_HEADS),
        in_specs=[pl.BlockSpec((None, QK_HEAD, seq), lambda b, h: (b, h, 0)),
                  pl.BlockSpec((None, None, seq, QK_HEAD), lambda b, h: (b, h, 0, 0)),
                  pl.BlockSpec((None, V_HEAD, seq), lambda b, h: (b, h, 0))],
        out_specs=pl.BlockSpec((None, seq, V_HEAD), lambda b, h: (b, 0, h)),
        out_shape=jax.ShapeDtypeStruct((batch, seq, MLA_WIDTH), BF16),
        scratch_shapes=[pltpu.VMEM((seq, tq), F32), pltpu.VMEM((1, tq), F32),
                        pltpu.VMEM((seq, tq), F32), pltpu.VMEM((1, tq), F32),
                        pltpu.VMEM((1, tq_fast), F32)],
        compiler_params=_params(("arbitrary", "arbitrary")),
        name="attention",
    )(qt, k, vt)


def _merge_out_kernel(x_ref, gate_ref, attn_ref, gm_ref, gp_ref, vp_ref, vprev_ref, vnext_ref,
                      mm_ref, mp_ref, pw_ref, ps_ref, wop_ref, wom_ref, wout_ref, fg_ref, o_ref,
                      *, seq, final_norm):
    tm = x_ref.shape[0]
    t0 = (pl.program_id(0) % (seq // tm)) * tm

    cur = vp_ref[...].astype(F32)
    prev = jnp.where(t0 > 0, vprev_ref[...].astype(F32), 0.0)
    nxt = jnp.where(t0 + tm < seq, vnext_ref[...].astype(F32), 0.0)
    ext = jnp.concatenate([prev, cur, nxt], axis=0)
    n_ext = tm + 2 * POOL_HALO
    tok = t0 + lax.broadcasted_iota(jnp.int32, (tm, 1), 0)
    mixed = []
    for g, w in enumerate(POOL_WINDOWS):
        acc = ext[:, g * POOL_GROUP_DIM:(g + 1) * POOL_GROUP_DIM]
        acc = acc + pltpu.roll(acc, 1, axis=0)
        half = 1
        while 2 * half < w:
            acc = pltpu.roll(acc, half, axis=0) + pltpu.roll(acc, n_ext - half, axis=0)
            half *= 2
        wsum = acc[POOL_HALO:POOL_HALO + tm]
        count = (jnp.minimum(tok + w // 2, seq) - jnp.maximum(tok - w // 2, 0)).astype(F32)
        pooled = wsum / count - cur[:, g * POOL_GROUP_DIM:(g + 1) * POOL_GROUP_DIM]
        mixed.append(jnp.dot(pooled.astype(BF16), pw_ref[g], preferred_element_type=F32))
    mixed = jnp.concatenate(mixed, axis=1)
    u = (mixed * ps_ref[...] * gp_ref[...].astype(F32)).astype(BF16)
    p_pool = jnp.dot(u, wop_ref[...], preferred_element_type=F32)

    gated = (attn_ref[...].astype(F32) * gm_ref[...].astype(F32)).astype(BF16)
    p_mla = jnp.dot(gated, wom_ref[...], preferred_element_type=F32)

    y = mm_ref[...].astype(F32) * p_mla + mp_ref[...].astype(F32) * p_pool
    r = jnp.dot(y.astype(BF16), wout_ref[...], preferred_element_type=F32)
    xo = x_ref[...] + gate_ref[...] * r
    if final_norm:
        xo = xo * lax.rsqrt(jnp.mean(xo * xo, axis=-1, keepdims=True) + EPS) * fg_ref[...]
    o_ref[...] = xo


def _merge_out(xt, mod4, attn, zbig, pool_w, pool_scale, w_o_pool, w_o_mla, w_out, final_g,
               seq, final_norm):
    T, D = xt.shape
    tm = 256
    per_b = seq // tm
    halo_per_tile = tm // POOL_HALO
    n_halo = T // POOL_HALO
    vp_blk = MLA_WIDTH // POOL_WIDTH
    gp_blk = vp_blk + 1
    mm_blk = (MLA_WIDTH + 2 * POOL_WIDTH) // D
    resident = functools.partial(pl.BlockSpec, pipeline_mode=pl.Buffered(1))
    kern = functools.partial(_merge_out_kernel, seq=seq, final_norm=final_norm)
    return pl.pallas_call(
        kern,
        grid=(T // tm,),
        in_specs=[pl.BlockSpec((tm, D), lambda i: (i, 0)),
                  pl.BlockSpec((None, None, 1, D), lambda i: (i // per_b, 2, 0, 0)),
                  pl.BlockSpec((tm, MLA_WIDTH), lambda i: (i, 0)),
                  pl.BlockSpec((tm, MLA_WIDTH), lambda i: (i, 0)),
                  pl.BlockSpec((tm, POOL_WIDTH), lambda i: (i, gp_blk)),
                  pl.BlockSpec((tm, POOL_WIDTH), lambda i: (i, vp_blk)),
                  pl.BlockSpec((POOL_HALO, POOL_WIDTH),
                               lambda i: (jnp.maximum(i * halo_per_tile - 1, 0), vp_blk)),
                  pl.BlockSpec((POOL_HALO, POOL_WIDTH),
                               lambda i: (jnp.minimum((i + 1) * halo_per_tile, n_halo - 1), vp_blk)),
                  pl.BlockSpec((tm, D), lambda i: (i, mm_blk)),
                  pl.BlockSpec((tm, D), lambda i: (i, mm_blk + 1)),
                  resident(pool_w.shape, lambda i: (0, 0, 0)),
                  resident((1, POOL_WIDTH), lambda i: (0, 0)),
                  resident(w_o_pool.shape, lambda i: (0, 0)),
                  resident(w_o_mla.shape, lambda i: (0, 0)),
                  resident(w_out.shape, lambda i: (0, 0)),
                  resident((1, D), lambda i: (0, 0))],
        out_specs=pl.BlockSpec((tm, D), lambda i: (i, 0)),
        out_shape=jax.ShapeDtypeStruct((T, D), F32),
        compiler_params=_params(("arbitrary",)),
        name="merge_out",
    )(xt, mod4, attn, zbig, zbig, zbig, zbig, zbig, zbig, zbig,
      pool_w, pool_scale.reshape(1, POOL_WIDTH), w_o_pool, w_o_mla, w_out, final_g.reshape(1, D))


def _split_up_proj(w_uq, w_ukv):
    wq = w_uq.reshape(Q_LORA, N_HEADS, QK_HEAD)
    nope = wq[:, :, :QK_NOPE].reshape(Q_LORA, N_HEADS * QK_NOPE)
    rope = wq[:, :, QK_NOPE:]
    rope_sw = jnp.concatenate([rope[:, :, QK_ROPE // 2:], rope[:, :, :QK_ROPE // 2]], axis=2)
    wq_t = jnp.concatenate([nope, rope.reshape(Q_LORA, -1), rope_sw.reshape(Q_LORA, -1)], axis=1).T
    wkv = w_ukv.reshape(KV_LORA, N_HEADS, QK_NOPE + V_HEAD)
    wk = wkv[:, :, :QK_NOPE].reshape(KV_LORA, N_HEADS * QK_NOPE)
    wv_t = wkv[:, :, QK_NOPE:].reshape(KV_LORA, MLA_WIDTH).T
    return wq_t.astype(BF16), wk.astype(BF16), wv_t.astype(BF16)


def kernel(x, c, positions, ada_w, ada_b, norm_g, w_in, q_norm_g, w_uq, kv_norm_g, w_ukv, w_o_mla,
           pool_w, pool_scale, w_o_pool, w_out, final_g):
    B, S, D = x.shape
    depth = ada_w.shape[0]
    inv_freq = 1.0 / (ROPE_THETA ** (jnp.arange(0, QK_ROPE, 2, dtype=F32) / QK_ROPE))
    inv_signed = jnp.concatenate([-inv_freq, inv_freq])
    posr = positions.reshape(B, 1, S)
    posc = positions.reshape(B, S, 1)
    q_scale = QK_HEAD ** -0.5 * math.log2(math.e)

    xt = x.reshape(B * S, D)
    for l in range(depth):
        mod4 = _adaln(c, ada_w[l], ada_b[l]).reshape(B, 3, 1, D)
        wq_t, wk, wv_t = _split_up_proj(w_uq[l], w_ukv[l])

        w_in_t = w_in[l].T
        h, zs = _norm_proj(xt, mod4, norm_g[l], w_in_t, S)
        zbig = _gate_proj(h, w_in_t)
        qt, k, vt = _mla_prep(zs, posr, posc, inv_signed, q_norm_g[l], kv_norm_g[l],
                              wq_t, wk, wv_t, B, S, q_scale)
        attn = _attention(qt, k, vt, B, S).reshape(B * S, MLA_WIDTH)
        xt = _merge_out(xt, mod4, attn, zbig, pool_w[l].astype(BF16), pool_scale[l],
                        w_o_pool[l].astype(BF16), w_o_mla[l].astype(BF16), w_out[l].astype(BF16),
                        final_g, S, final_norm=(l == depth - 1))
    return xt.reshape(B, S, D)
```

```python
import functools
import math

import jax
import jax.numpy as jnp
from jax import lax
from jax.experimental import pallas as pl
from jax.experimental.pallas import tpu as pltpu

EPS = 1e-6
N_HEADS = 16
QK_NOPE = 128
QK_ROPE = 64
QK_HEAD = QK_NOPE + QK_ROPE
V_HEAD = 128
Q_LORA = 512
KV_LORA = 512
MLA_WIDTH = N_HEADS * V_HEAD
ROPE_THETA = 10000.0
POOL_WINDOWS = (2, 4, 8, 16)
POOL_GROUPS = len(POOL_WINDOWS)
POOL_GROUP_DIM = 256
POOL_WIDTH = POOL_GROUPS * POOL_GROUP_DIM
POOL_HALO = 16
SMALL_WIDTH = Q_LORA + KV_LORA + 2 * QK_ROPE

V7X_VMEM_LIMIT = 56 * 1024 * 1024
BF16_SUBLANES = 16
MIN_SOFTMAX_MASS = 2.0 ** -60

F32 = jnp.float32
BF16 = jnp.bfloat16
NT_DIMS = (((1,), (1,)), ((), ()))


def _sigmoid(v):
    return 1.0 / (1.0 + jnp.exp(-v))


def _params(semantics, vmem=V7X_VMEM_LIMIT, flags=None):
    return pltpu.CompilerParams(dimension_semantics=semantics, vmem_limit_bytes=vmem, flags=flags)


def _adaln_kernel(ct_ref, w_ref, b_ref, o_ref):
    w = w_ref[...]
    for b in range(ct_ref.shape[1]):
        cb = ct_ref[:, b:b + 1]
        act = cb * _sigmoid(cb)
        o_ref[b:b + 1, :] = jnp.sum(w * act, axis=0, keepdims=True) + b_ref[...]


def _adaln(c, w, bias):
    B, D = c.shape
    n = w.shape[1]
    tn = 1024
    return pl.pallas_call(
        _adaln_kernel,
        grid=(n // tn,),
        in_specs=[pl.BlockSpec((D, B), lambda j: (0, 0)),
                  pl.BlockSpec((D, tn), lambda j: (0, j)),
                  pl.BlockSpec((1, tn), lambda j: (0, j))],
        out_specs=pl.BlockSpec((B, tn), lambda j: (0, j)),
        out_shape=jax.ShapeDtypeStruct((B, n), F32),
        compiler_params=_params(("arbitrary",)),
        name="adaln",
    )(c.T, w, bias.reshape(1, n))


def _norm_proj_kernel(x_ref, shift_ref, scale_ref, g_ref, ws_ref, h_ref, zs_ref, ws_bf16):
    @pl.when(pl.program_id(0) == 0)
    def _():
        ws_bf16[...] = ws_ref[...].astype(BF16)

    x = x_ref[...]
    y = x * lax.rsqrt(jnp.mean(x * x, axis=-1, keepdims=True) + EPS) * g_ref[...]
    h = (y * (1.0 + scale_ref[...]) + shift_ref[...]).astype(BF16)
    h_ref[...] = h
    zs_ref[...] = lax.dot_general(h, ws_bf16[...], NT_DIMS, preferred_element_type=F32)


def _norm_proj(xt, mod4, norm_g, w_in_t, seq):
    T, D = xt.shape
    tm = 512
    per_b = seq // tm
    return pl.pallas_call(
        _norm_proj_kernel,
        grid=(T // tm,),
        in_specs=[pl.BlockSpec((tm, D), lambda i: (i, 0)),
                  pl.BlockSpec((None, None, 1, D), lambda i: (i // per_b, 0, 0, 0)),
                  pl.BlockSpec((None, None, 1, D), lambda i: (i // per_b, 1, 0, 0)),
                  pl.BlockSpec((1, D), lambda i: (0, 0)),
                  pl.BlockSpec((SMALL_WIDTH, D), lambda i: (0, 0), pipeline_mode=pl.Buffered(1))],
        out_specs=[pl.BlockSpec((tm, D), lambda i: (i, 0)),
                   pl.BlockSpec((tm, SMALL_WIDTH), lambda i: (i, 0))],
        out_shape=[jax.ShapeDtypeStruct((T, D), BF16),
                   jax.ShapeDtypeStruct((T, SMALL_WIDTH), F32)],
        scratch_shapes=[pltpu.VMEM((SMALL_WIDTH, D), BF16)],
        compiler_params=_params(("arbitrary",)),
        name="norm_proj",
    )(xt, mod4, mod4, norm_g.reshape(1, D), w_in_t)


_ACTIVATIONS = {
    "silu": lambda a: a * _sigmoid(a),
    "linear": lambda a: a,
    "sigmoid": _sigmoid,
}


def _gate_proj_kernel(h_ref, wa_ref, wb_ref, o_ref, w_bf16, *, tile_kinds, shift):
    j = pl.program_id(0)
    tn = w_bf16.shape[0]

    @pl.when(pl.program_id(1) == 0)
    def _():
        w_bf16[0:tn - shift, :] = wa_ref[shift:, :].astype(BF16)
        w_bf16[tn - shift:, :] = wb_ref[...].astype(BF16)

    for kind, act in _ACTIVATIONS.items():
        tiles = [t for t, k in enumerate(tile_kinds) if k == kind]
        cond = functools.reduce(jnp.logical_or, [j == t for t in tiles])

        @pl.when(cond)
        def _(act=act):
            acc = lax.dot_general(h_ref[...], w_bf16[...], NT_DIMS, preferred_element_type=F32)
            o_ref[...] = act(acc).astype(o_ref.dtype)


def _gate_proj(h, w_in_t):
    T, D = h.shape
    tm, tn = 1024, 1024
    start = Q_LORA + KV_LORA + QK_ROPE
    n = w_in_t.shape[0] - start
    first_blk, shift = divmod(start, tn)
    assert n % tn == 0 and tn % shift == 0 and shift % BF16_SUBLANES == 0
    tile_kinds = (("silu",) * (MLA_WIDTH // tn) + ("linear",) * (POOL_WIDTH // tn)
                  + ("silu",) * (POOL_WIDTH // tn) + ("sigmoid",) * (2 * D // tn))
    kern = functools.partial(_gate_proj_kernel, tile_kinds=tile_kinds, shift=shift)
    return pl.pallas_call(
        kern,
        grid=(n // tn, T // tm),
        in_specs=[pl.BlockSpec((tm, D), lambda j, i: (i, 0)),
                  pl.BlockSpec((tn, D), lambda j, i: (first_blk + j, 0)),
                  pl.BlockSpec((shift, D), lambda j, i: ((first_blk + j + 1) * (tn // shift), 0))],
        out_specs=pl.BlockSpec((tm, tn), lambda j, i: (i, j)),
        out_shape=jax.ShapeDtypeStruct((T, n), BF16),
        scratch_shapes=[pltpu.VMEM((tn, D), BF16)],
        compiler_params=_params(("arbitrary", "arbitrary")),
        name="gate_proj",
    )(h, w_in_t, w_in_t)


def _mla_prep_kernel(zs_ref, posr_ref, invc_ref, qg_ref, kvg_ref,
                     wq_ref, wk_ref, wv_ref, qt_ref, k_ref, vt_ref, *, q_scale):
    def rms(v, g):
        return (v * lax.rsqrt(jnp.mean(v * v, axis=-1, keepdims=True) + EPS) * g).astype(BF16)

    cqn = rms(zs_ref[:, 0:Q_LORA], qg_ref[...])
    ckvn = rms(zs_ref[:, Q_LORA:Q_LORA + KV_LORA], kvg_ref[...])
    kr = zs_ref[:, Q_LORA + KV_LORA:Q_LORA + KV_LORA + QK_ROPE]
    kr_sw = jnp.concatenate([kr[:, QK_ROPE // 2:], kr[:, :QK_ROPE // 2]], axis=1)

    ang_t = invc_ref[...] * posr_ref[...].astype(F32)
    cos_t, sin_t = jnp.cos(ang_t), jnp.sin(ang_t)
    cos, sin = cos_t.T, sin_t.T

    qf = lax.dot_general(wq_ref[...], cqn, NT_DIMS, preferred_element_type=F32)
    half = QK_ROPE // 2
    for h in range(N_HEADS):
        r0 = h * QK_HEAD + QK_NOPE
        qt_ref[h * QK_HEAD:r0, :] = (qf[h * QK_HEAD:r0] * q_scale).astype(BF16)
        rope = qf[r0:r0 + QK_ROPE]
        rope_sw = jnp.concatenate([rope[half:], rope[:half]], axis=0)
        qt_ref[r0:r0 + QK_ROPE, :] = ((rope * cos_t + rope_sw * sin_t) * q_scale).astype(BF16)

    kn = jnp.dot(ckvn, wk_ref[...], preferred_element_type=F32)
    k_rot = (kr * cos + kr_sw * sin).astype(BF16)
    for h in range(N_HEADS):
        k_ref[h, :, 0:QK_NOPE] = kn[:, h * QK_NOPE:(h + 1) * QK_NOPE].astype(BF16)
        k_ref[h, :, QK_NOPE:QK_HEAD] = k_rot

    vt_ref[...] = lax.dot_general(wv_ref[...], ckvn, NT_DIMS,
                                  preferred_element_type=F32).astype(BF16)


def _mla_prep(zs, posr, inv_signed, q_norm_g, kv_norm_g, wq_t, wk, wv_t, batch, seq, q_scale):
    tm = 256
    per_b = seq // tm
    const = lambda i: (0, 0)
    kern = functools.partial(_mla_prep_kernel, q_scale=q_scale)
    return pl.pallas_call(
        kern,
        grid=(batch * per_b,),
        in_specs=[pl.BlockSpec((tm, SMALL_WIDTH), lambda i: (i, 0)),
                  pl.BlockSpec((None, 1, tm), lambda i: (i // per_b, 0, i % per_b)),
                  pl.BlockSpec((QK_ROPE, 1), const),
                  pl.BlockSpec((1, Q_LORA), const),
                  pl.BlockSpec((1, KV_LORA), const),
                  pl.BlockSpec(wq_t.shape, const),
                  pl.BlockSpec(wk.shape, const),
                  pl.BlockSpec(wv_t.shape, const)],
        out_specs=[pl.BlockSpec((None, N_HEADS * QK_HEAD, tm), lambda i: (i // per_b, 0, i % per_b)),
                   pl.BlockSpec((None, N_HEADS, tm, QK_HEAD), lambda i: (i // per_b, 0, i % per_b, 0)),
                   pl.BlockSpec((None, MLA_WIDTH, tm), lambda i: (i // per_b, 0, i % per_b))],
        out_shape=[jax.ShapeDtypeStruct((batch, N_HEADS * QK_HEAD, seq), BF16),
                   jax.ShapeDtypeStruct((batch, N_HEADS, seq, QK_HEAD), BF16),
                   jax.ShapeDtypeStruct((batch, MLA_WIDTH, seq), BF16)],
        compiler_params=_params(("arbitrary",)),
        name="mla_prep",
    )(zs, posr, inv_signed.reshape(QK_ROPE, 1),
      q_norm_g.reshape(1, Q_LORA), kv_norm_g.reshape(1, KV_LORA), wq_t, wk, wv_t)


def _attention_kernel(qt_ref, k_ref, vt_ref, o_ref, s_a, m_a, s_b, m_b, lmin_ref):
    tq = lmin_ref.shape[1]
    nq = qt_ref.shape[1] // tq

    kf = k_ref[...].astype(F32)
    k_norm = jnp.sqrt(jnp.max(jnp.sum(kf * kf, axis=1, keepdims=True), axis=0, keepdims=True))
    lmin_ref[...] = jnp.full(lmin_ref.shape, jnp.inf, F32)

    def fast_tile(i, carry):
        off = pl.multiple_of(i * tq, tq)
        qt = qt_ref[:, pl.ds(off, tq)]
        qf = qt.astype(F32)
        shift = jnp.sqrt(jnp.sum(qf * qf, axis=0, keepdims=True)) * k_norm
        s = jnp.dot(k_ref[...], qt, preferred_element_type=F32)
        p = jnp.exp2(s - shift)
        l = jnp.sum(p, axis=0, keepdims=True)
        ot = jnp.dot(vt_ref[...], p.astype(BF16), preferred_element_type=F32)
        o_ref[pl.ds(off, tq), :] = (ot / l).T.astype(o_ref.dtype)
        lmin_ref[...] = jnp.minimum(lmin_ref[...], l)
        return carry

    lax.fori_loop(0, nq, fast_tile, 0, unroll=4)
    trusted = jnp.min(lmin_ref[...]) >= MIN_SOFTMAX_MASS

    @pl.when(jnp.logical_not(trusted))
    def _():
        _attention_exact(qt_ref, k_ref, vt_ref, o_ref, s_a, m_a, s_b, m_b)


def _attention_exact(qt_ref, k_ref, vt_ref, o_ref, s_a, m_a, s_b, m_b):
    tq = s_a.shape[1]
    nq = qt_ref.shape[1] // tq

    def scores(i, s_ref, m_ref):
        off = pl.multiple_of(i * tq, tq)
        s = jnp.dot(k_ref[...], qt_ref[:, pl.ds(off, tq)], preferred_element_type=F32)
        s_ref[...] = s
        m_ref[...] = jnp.max(s, axis=0, keepdims=True)

    def finish(i, s_ref, m_ref):
        off = pl.multiple_of(i * tq, tq)
        p = jnp.exp2(s_ref[...] - m_ref[...])
        l = jnp.sum(p, axis=0, keepdims=True)
        ot = jnp.dot(vt_ref[...], p.astype(BF16), preferred_element_type=F32)
        o_ref[pl.ds(off, tq), :] = (ot / l).T.astype(o_ref.dtype)

    scores(0, s_a, m_a)

    def step(i, carry):
        @pl.when(i % 2 == 1)
        def _():
            scores(i, s_b, m_b)
            finish(i - 1, s_a, m_a)

        @pl.when(i % 2 == 0)
        def _():
            scores(i, s_a, m_a)
            finish(i - 1, s_b, m_b)

        return carry

    lax.fori_loop(1, nq, step, 0)
    finish(nq - 1, s_b, m_b)


def _attention(qt, k, vt, batch, seq):
    tq = 256
    tq_fast = 512
    assert seq % (2 * tq) == 0 and seq % tq_fast == 0
    return pl.pallas_call(
        _attention_kernel,
        grid=(batch, N_HEADS),
        in_specs=[pl.BlockSpec((None, QK_HEAD, seq), lambda b, h: (b, h, 0)),
                  pl.BlockSpec((None, None, seq, QK_HEAD), lambda b, h: (b, h, 0, 0)),
                  pl.BlockSpec((None, V_HEAD, seq), lambda b, h: (b, h, 0))],
        out_specs=pl.BlockSpec((None, seq, V_HEAD), lambda b, h: (b, 0, h)),
        out_shape=jax.ShapeDtypeStruct((batch, seq, MLA_WIDTH), BF16),
        scratch_shapes=[pltpu.VMEM((seq, tq), F32), pltpu.VMEM((1, tq), F32),
                        pltpu.VMEM((seq, tq), F32), pltpu.VMEM((1, tq), F32),
                        pltpu.VMEM((1, tq_fast), F32)],
        compiler_params=_params(("arbitrary", "arbitrary")),
        name="attention",
    )(qt, k, vt)


def _merge_out_kernel(x_ref, gate_ref, attn_ref, gm_ref, gp_ref, vp_ref, vprev_ref, vnext_ref,
                      mm_ref, mp_ref, pw_ref, ps_ref, wop_ref, wom_ref, wout_ref, fg_ref, o_ref,
                      *, seq, final_norm):
    tm = x_ref.shape[0]
    t0 = (pl.program_id(0) % (seq // tm)) * tm

    gated = attn_ref[...] * gm_ref[...]
    p_mla = jnp.dot(gated, wom_ref[...], preferred_element_type=F32)

    cur = vp_ref[...].astype(F32)
    prev = jnp.where(t0 > 0, vprev_ref[...].astype(F32), 0.0)
    nxt = jnp.where(t0 + tm < seq, vnext_ref[...].astype(F32), 0.0)
    ext = jnp.concatenate([prev, cur, nxt], axis=0)
    n_ext = tm + 2 * POOL_HALO
    tok = t0 + lax.broadcasted_iota(jnp.int32, (tm, 1), 0)
    mixed = []
    for g, w in enumerate(POOL_WINDOWS):
        acc = ext[:, g * POOL_GROUP_DIM:(g + 1) * POOL_GROUP_DIM]
        acc = acc + pltpu.roll(acc, 1, axis=0)
        half = 1
        while 2 * half < w:
            acc = pltpu.roll(acc, half, axis=0) + pltpu.roll(acc, n_ext - half, axis=0)
            half *= 2
        wsum = acc[POOL_HALO:POOL_HALO + tm]
        count = (jnp.minimum(tok + w // 2, seq) - jnp.maximum(tok - w // 2, 0)).astype(F32)
        pooled = wsum / count - cur[:, g * POOL_GROUP_DIM:(g + 1) * POOL_GROUP_DIM]
        mixed.append(jnp.dot(pooled.astype(BF16), pw_ref[g], preferred_element_type=F32))
    mixed = jnp.concatenate(mixed, axis=1)
    u = (mixed * ps_ref[...] * gp_ref[...].astype(F32)).astype(BF16)
    p_pool = jnp.dot(u, wop_ref[...], preferred_element_type=F32)

    y =mm_ref[...].astype(F32) * p_mla + mp_ref[...].astype(F32) * p_pool
    r = jnp.dot(y.astype(BF16), wout_ref[...], preferred_element_type=F32)
    xo = x_ref[...] + gate_ref[...] * r
    if final_norm:
        xo = xo * lax.rsqrt(jnp.mean(xo * xo, axis=-1, keepdims=True) + EPS) * fg_ref[...]
    o_ref[...] = xo


def _merge_out(xt, mod4, attn, zbig, pool_w, pool_scale, w_o_pool, w_o_mla, w_out, final_g,
               seq, final_norm):
    T, D = xt.shape
    tm = 256
    per_b = seq // tm
    halo_per_tile = tm // POOL_HALO
    n_halo = T // POOL_HALO
    vp_blk = MLA_WIDTH // POOL_WIDTH
    gp_blk = vp_blk + 1
    mm_blk = (MLA_WIDTH + 2 * POOL_WIDTH) // D
    resident = functools.partial(pl.BlockSpec, pipeline_mode=pl.Buffered(1))
    kern = functools.partial(_merge_out_kernel, seq=seq, final_norm=final_norm)
    return pl.pallas_call(
        kern,
        grid=(T // tm,),
        in_specs=[pl.BlockSpec((tm, D), lambda i: (i, 0)),
                  pl.BlockSpec((None, None, 1, D), lambda i: (i // per_b, 2, 0, 0)),
                  pl.BlockSpec((tm, MLA_WIDTH), lambda i: (i, 0)),
                  pl.BlockSpec((tm, MLA_WIDTH), lambda i: (i, 0)),
                  pl.BlockSpec((tm, POOL_WIDTH), lambda i: (i, gp_blk)),
                  pl.BlockSpec((tm, POOL_WIDTH), lambda i: (i, vp_blk)),
                  pl.BlockSpec((POOL_HALO, POOL_WIDTH),
                               lambda i: (jnp.maximum(i * halo_per_tile - 1, 0), vp_blk)),
                  pl.BlockSpec((POOL_HALO, POOL_WIDTH),
                               lambda i: (jnp.minimum((i + 1) * halo_per_tile, n_halo - 1), vp_blk)),
                  pl.BlockSpec((tm, D), lambda i: (i, mm_blk)),
                  pl.BlockSpec((tm, D), lambda i: (i, mm_blk + 1)),
                  resident(pool_w.shape, lambda i: (0, 0, 0)),
                  resident((1, POOL_WIDTH), lambda i: (0, 0)),
                  resident(w_o_pool.shape, lambda i: (0, 0)),
                  resident(w_o_mla.shape, lambda i: (0, 0)),
                  resident(w_out.shape, lambda i: (0, 0)),
                  resident((1, D), lambda i: (0, 0))],
        out_specs=pl.BlockSpec((tm, D), lambda i: (i, 0)),
        out_shape=jax.ShapeDtypeStruct((T, D), F32),
        compiler_params=_params(("arbitrary",)),
        name="merge_out",
    )(xt, mod4, attn, zbig, zbig, zbig, zbig, zbig, zbig, zbig,
      pool_w, pool_scale.reshape(1, POOL_WIDTH), w_o_pool, w_o_mla, w_out, final_g.reshape(1, D))


def _split_up_proj(w_uq, w_ukv):
    wq_t = w_uq.T
    wkv = w_ukv.reshape(KV_LORA, N_HEADS, QK_NOPE + V_HEAD)
    wk = wkv[:, :, :QK_NOPE].reshape(KV_LORA, N_HEADS * QK_NOPE)
    wv_t = wkv[:, :, QK_NOPE:].reshape(KV_LORA, MLA_WIDTH).T
    return wq_t.astype(BF16), wk.astype(BF16), wv_t.astype(BF16)


def kernel(x, c, positions, ada_w, ada_b, norm_g, w_in, q_norm_g, w_uq, kv_norm_g, w_ukv, w_o_mla,
           pool_w, pool_scale, w_o_pool, w_out, final_g):
    B, S, D = x.shape
    depth = ada_w.shape[0]
    inv_freq = 1.0 / (ROPE_THETA ** (jnp.arange(0, QK_ROPE, 2, dtype=F32) / QK_ROPE))
    inv_signed = jnp.concatenate([-inv_freq, inv_freq])
    posr = positions.reshape(B, 1, S)
    q_scale = QK_HEAD ** -0.5 * math.log2(math.e)

    xt = x.reshape(B * S, D)
    for l in range(depth):
        mod4 = _adaln(c, ada_w[l], ada_b[l]).reshape(B, 3, 1, D)
        wq_t, wk, wv_t = _split_up_proj(w_uq[l], w_ukv[l])

        w_in_t = w_in[l].T
        h, zs = _norm_proj(xt, mod4, norm_g[l], w_in_t, S)
        zbig = _gate_proj(h, w_in_t)
        qt, k, vt = _mla_prep(zs, posr, inv_signed, q_norm_g[l], kv_norm_g[l],
                              wq_t, wk, wv_t, B, S, q_scale)
        attn = _attention(qt, k, vt, B, S).reshape(B * S, MLA_WIDTH)
        xt = _merge_out(xt, mod4, attn, zbig, pool_w[l].astype(BF16), pool_scale[l],
                        w_o_pool[l].astype(BF16), w_o_mla[l].astype(BF16), w_out[l].astype(BF16),
                        final_g, S, final_norm=(l == depth - 1))
    return xt.reshape(B, S, D)
```

```python
import functools
import math

import jax
import jax.numpy as jnp
from jax import lax
from jax.experimental import pallas as pl
from jax.experimental.pallas import tpu as pltpu

EPS = 1e-6
N_HEADS = 16
QK_NOPE = 128
QK_ROPE = 64
QK_HEAD = QK_NOPE + QK_ROPE
V_HEAD = 128
Q_LORA = 512
KV_LORA = 512
MLA_WIDTH = N_HEADS * V_HEAD
ROPE_THETA = 10000.0
POOL_WINDOWS = (2, 4, 8, 16)
POOL_GROUPS = len(POOL_WINDOWS)
POOL_GROUP_DIM = 256
POOL_WIDTH = POOL_GROUPS * POOL_GROUP_DIM
POOL_HALO = 16
SMALL_WIDTH = Q_LORA + KV_LORA + 2 * QK_ROPE

V7X_VMEM_LIMIT = 56 * 1024 * 1024
BF16_SUBLANES = 16
MIN_SOFTMAX_MASS = 2.0 ** -60

F32 = jnp.float32
BF16 = jnp.bfloat16
NT_DIMS = (((1,), (1,)), ((), ()))


def _sigmoid(v):
    return 0.5 * jnp.tanh(0.5 * v) + 0.5


def _params(semantics, vmem=V7X_VMEM_LIMIT, flags=None):
    return pltpu.CompilerParams(dimension_semantics=semantics, vmem_limit_bytes=vmem, flags=flags)


def _adaln_kernel(ct_ref, w_ref, b_ref, o_ref):
    w = w_ref[...]
    for b in range(ct_ref.shape[1]):
        cb = ct_ref[:, b:b + 1]
        act = cb * _sigmoid(cb)
        o_ref[b:b + 1, :] = jnp.sum(w * act, axis=0, keepdims=True) + b_ref[...]


def _adaln(c, w, bias):
    B, D = c.shape
    n = w.shape[1]
    tn = 1024
    return pl.pallas_call(
        _adaln_kernel,
        grid=(n // tn,),
        in_specs=[pl.BlockSpec((D, B), lambda j: (0, 0)),
                  pl.BlockSpec((D, tn), lambda j: (0, j)),
                  pl.BlockSpec((1, tn), lambda j: (0, j))],
        out_specs=pl.BlockSpec((B, tn), lambda j: (0, j)),
        out_shape=jax.ShapeDtypeStruct((B, n), F32),
        compiler_params=_params(("arbitrary",)),
        name="adaln",
    )(c.T, w, bias.reshape(1, n))


def _norm_proj_kernel(x_ref, shift_ref, scale_ref, g_ref, ws_ref, h_ref, zs_ref, ws_bf16):
    @pl.when(pl.program_id(0) == 0)
    def _():
        ws_bf16[...] = ws_ref[...].astype(BF16)

    x = x_ref[...]
    y = x * lax.rsqrt(jnp.mean(x * x, axis=-1, keepdims=True) + EPS) * g_ref[...]
    h = (y * (1.0 + scale_ref[...]) + shift_ref[...]).astype(BF16)
    h_ref[...] = h
    zs_ref[...] = lax.dot_general(h, ws_bf16[...], NT_DIMS, preferred_element_type=F32)


def _norm_proj(xt, mod4, norm_g, w_in_t, seq):
    T, D = xt.shape
    tm = 512
    per_b = seq // tm
    return pl.pallas_call(
        _norm_proj_kernel,
        grid=(T // tm,),
        in_specs=[pl.BlockSpec((tm, D), lambda i: (i, 0)),
                  pl.BlockSpec((None, None, 1, D), lambda i: (i // per_b, 0, 0, 0)),
                  pl.BlockSpec((None, None, 1, D), lambda i: (i // per_b, 1, 0, 0)),
                  pl.BlockSpec((1, D), lambda i: (0, 0)),
                  pl.BlockSpec((SMALL_WIDTH, D), lambda i: (0, 0), pipeline_mode=pl.Buffered(1))],
        out_specs=[pl.BlockSpec((tm, D), lambda i: (i, 0)),
                   pl.BlockSpec((tm, SMALL_WIDTH), lambda i: (i, 0))],
        out_shape=[jax.ShapeDtypeStruct((T, D), BF16),
                   jax.ShapeDtypeStruct((T, SMALL_WIDTH), F32)],
        scratch_shapes=[pltpu.VMEM((SMALL_WIDTH, D), BF16)],
        compiler_params=_params(("arbitrary",)),
        name="norm_proj",
    )(xt, mod4, mod4, norm_g.reshape(1, D), w_in_t)


GATE_ROW_CHUNKS = 4

_ACTIVATIONS = {
    "silu": lambda a: a * _sigmoid(a),
    "linear": lambda a: a,
    "sigmoid": _sigmoid,
}


def _gate_proj_kernel(h_ref, wa_ref, wb_ref, o_ref, w_bf16, *, tile_kinds, shift):
    j = pl.program_id(0)
    tn = w_bf16.shape[0]

    @pl.when(pl.program_id(1) == 0)
    def _():
        w_bf16[0:tn - shift, :] = wa_ref[shift:, :].astype(BF16)
        w_bf16[tn - shift:, :] = wb_ref[...].astype(BF16)

    for kind, act in _ACTIVATIONS.items():
        tiles = [t for t, k in enumerate(tile_kinds) if k == kind]
        cond = functools.reduce(jnp.logical_or, [j == t for t in tiles])

        @pl.when(cond)
        def _(act=act):
            rows = h_ref.shape[0] // GATE_ROW_CHUNKS
            for c in range(GATE_ROW_CHUNKS):
                acc = lax.dot_general(h_ref[c * rows:(c + 1) * rows, :], w_bf16[...], NT_DIMS,
                                      preferred_element_type=F32)
                o_ref[c * rows:(c + 1) * rows, :] = act(acc).astype(o_ref.dtype)


def _gate_proj(h, w_in_t):
    T, D = h.shape
    tm, tn = 1024, 1024
    start = Q_LORA + KV_LORA + QK_ROPE
    n = w_in_t.shape[0] - start
    first_blk, shift = divmod(start, tn)
    assert n % tn == 0 and tn % shift == 0 and shift % BF16_SUBLANES == 0
    tile_kinds = (("silu",) * (MLA_WIDTH // tn) + ("linear",) * (POOL_WIDTH // tn)
                  + ("silu",) * (POOL_WIDTH // tn) + ("sigmoid",) * (2 * D // tn))
    kern = functools.partial(_gate_proj_kernel, tile_kinds=tile_kinds, shift=shift)
    return pl.pallas_call(
        kern,
        grid=(n // tn, T // tm),
        in_specs=[pl.BlockSpec((tm, D), lambda j, i: (i, 0)),
                  pl.BlockSpec((tn, D), lambda j, i: (first_blk + j, 0)),
                  pl.BlockSpec((shift, D), lambda j, i: ((first_blk + j + 1) * (tn // shift), 0))],
        out_specs=pl.BlockSpec((tm, tn), lambda j, i: (i, j)),
        out_shape=jax.ShapeDtypeStruct((T, n), BF16),
        scratch_shapes=[pltpu.VMEM((tn, D), BF16)],
        compiler_params=_params(("arbitrary", "arbitrary")),
        name="gate_proj",
    )(h, w_in_t, w_in_t)


def _mla_prep_kernel(zs_ref, posr_ref, invc_ref, qg_ref, kvg_ref,
                     wq_ref, wk_ref, wv_ref, qt_ref, k_ref, vt_ref, *, q_scale):
    def rms(v, g):
        return (v * lax.rsqrt(jnp.mean(v * v, axis=-1, keepdims=True) + EPS) * g).astype(BF16)

    cqn = rms(zs_ref[:, 0:Q_LORA], qg_ref[...])
    ckvn = rms(zs_ref[:, Q_LORA:Q_LORA + KV_LORA], kvg_ref[...])
    kr = zs_ref[:, Q_LORA + KV_LORA:Q_LORA + KV_LORA + QK_ROPE]
    kr_sw = jnp.concatenate([kr[:, QK_ROPE // 2:], kr[:, :QK_ROPE // 2]], axis=1)

    ang_t = invc_ref[...] * posr_ref[...].astype(F32)
    cos_t, sin_t = jnp.cos(ang_t), jnp.sin(ang_t)
    cos, sin = cos_t.T, sin_t.T

    qf = lax.dot_general(wq_ref[...], cqn, NT_DIMS, preferred_element_type=F32)
    half = QK_ROPE // 2
    for h in range(N_HEADS):
        r0 = h * QK_HEAD + QK_NOPE
        qt_ref[h * QK_HEAD:r0, :] = (qf[h * QK_HEAD:r0] * q_scale).astype(BF16)
        rope = qf[r0:r0 + QK_ROPE]
        rope_sw = jnp.concatenate([rope[half:], rope[:half]], axis=0)
        qt_ref[r0:r0 + QK_ROPE, :] = ((rope * cos_t + rope_sw * sin_t) * q_scale).astype(BF16)

    kn = jnp.dot(ckvn, wk_ref[...], preferred_element_type=F32)
    k_rot = (kr * cos + kr_sw * sin).astype(BF16)
    for h in range(N_HEADS):
        k_ref[h, :, 0:QK_NOPE] = kn[:, h * QK_NOPE:(h + 1) * QK_NOPE].astype(BF16)
        k_ref[h, :, QK_NOPE:QK_HEAD] = k_rot

    vt_ref[...] = lax.dot_general(wv_ref[...], ckvn, NT_DIMS,
                                  preferred_element_type=F32).astype(BF16)


def _mla_prep(zs, posr, inv_signed, q_norm_g, kv_norm_g, wq_t, wk, wv_t, batch, seq, q_scale):
    tm = 256
    per_b = seq // tm
    const = lambda i: (0, 0)
    kern = functools.partial(_mla_prep_kernel, q_scale=q_scale)
    return pl.pallas_call(
        kern,
        grid=(batch * per_b,),
        in_specs=[pl.BlockSpec((tm, SMALL_WIDTH), lambda i: (i, 0)),
                  pl.BlockSpec((None, 1, tm), lambda i: (i // per_b, 0, i % per_b)),
                  pl.BlockSpec((QK_ROPE, 1), const),
                  pl.BlockSpec((1, Q_LORA), const),
                  pl.BlockSpec((1, KV_LORA), const),
                  pl.BlockSpec(wq_t.shape, const),
                  pl.BlockSpec(wk.shape, const),
                  pl.BlockSpec(wv_t.shape, const)],
        out_specs=[pl.BlockSpec((None, N_HEADS * QK_HEAD, tm), lambda i: (i // per_b, 0, i % per_b)),
                   pl.BlockSpec((None, N_HEADS, tm, QK_HEAD), lambda i: (i // per_b, 0, i % per_b, 0)),
                   pl.BlockSpec((None, MLA_WIDTH, tm), lambda i: (i // per_b, 0, i % per_b))],
        out_shape=[jax.ShapeDtypeStruct((batch, N_HEADS * QK_HEAD, seq), BF16),
                   jax.ShapeDtypeStruct((batch, N_HEADS, seq, QK_HEAD), BF16),
                   jax.ShapeDtypeStruct((batch, MLA_WIDTH, seq), BF16)],
        compiler_params=_params(("arbitrary",)),
        name="mla_prep",
    )(zs, posr, inv_signed.reshape(QK_ROPE, 1),
      q_norm_g.reshape(1, Q_LORA), kv_norm_g.reshape(1, KV_LORA), wq_t, wk, wv_t)


def _attention_kernel(qt_ref, k_ref, vt_ref, o_ref, s_a, m_a, s_b, m_b, lmin_ref):
    tq = lmin_ref.shape[1]
    nq = qt_ref.shape[1] // tq

    kf = k_ref[...].astype(F32)
    k_norm = jnp.sqrt(jnp.max(jnp.sum(kf * kf, axis=1, keepdims=True), axis=0, keepdims=True))
    lmin_ref[...] = jnp.full(lmin_ref.shape, jnp.inf, F32)

    def fast_tile(i, carry):
        off = pl.multiple_of(i * tq, tq)
        qt = qt_ref[:, pl.ds(off, tq)]
        qf = qt.astype(F32)
        shift = jnp.sqrt(jnp.sum(qf * qf, axis=0, keepdims=True)) * k_norm
        s = jnp.dot(k_ref[...], qt, preferred_element_type=F32)
        p = jnp.exp2(s - shift)
        l = jnp.sum(p, axis=0, keepdims=True)
        ot = jnp.dot(vt_ref[...], p.astype(BF16), preferred_element_type=F32)
        o_ref[pl.ds(off, tq), :] = (ot / l).T.astype(o_ref.dtype)
        lmin_ref[...] = jnp.minimum(lmin_ref[...], l)
        return carry

    lax.fori_loop(0, nq, fast_tile, 0, unroll=4)
    trusted = jnp.min(lmin_ref[...]) >= MIN_SOFTMAX_MASS

    @pl.when(jnp.logical_not(trusted))
    def _():
        _attention_exact(qt_ref, k_ref, vt_ref, o_ref, s_a, m_a, s_b, m_b)


def _attention_exact(qt_ref, k_ref, vt_ref, o_ref, s_a, m_a, s_b, m_b):
    tq = s_a.shape[1]
    nq = qt_ref.shape[1] // tq

    def scores(i, s_ref, m_ref):
        off = pl.multiple_of(i * tq, tq)
        s = jnp.dot(k_ref[...], qt_ref[:, pl.ds(off, tq)], preferred_element_type=F32)
        s_ref[...] = s
        m_ref[...] = jnp.max(s, axis=0, keepdims=True)

    def finish(i, s_ref, m_ref):
        off = pl.multiple_of(i * tq, tq)
        p = jnp.exp2(s_ref[...] - m_ref[...])
        l = jnp.sum(p, axis=0, keepdims=True)
        ot = jnp.dot(vt_ref[...], p.astype(BF16), preferred_element_type=F32)
        o_ref[pl.ds(off, tq), :] = (ot / l).T.astype(o_ref.dtype)

    scores(0, s_a, m_a)

    def step(i, carry):
        @pl.when(i % 2 == 1)
        def _():
            scores(i, s_b, m_b)
            finish(i - 1, s_a, m_a)

        @pl.when(i % 2 == 0)
        def _():
            scores(i, s_a, m_a)
            finish(i - 1, s_b, m_b)

        return carry

    lax.fori_loop(1, nq, step, 0)
    finish(nq - 1, s_b, m_b)


def _attention(qt, k, vt, batch, seq):
    tq = 256
    tq_fast = 512
    assert seq % (2 * tq) == 0 and seq % tq_fast == 0
    return pl.pallas_call(
        _attention_kernel,
        grid=(batch, N_HEADS),
        in_specs=[pl.BlockSpec((None, QK_HEAD, seq), lambda b, h: (b, h, 0)),
                  pl.BlockSpec((None, None, seq, QK_HEAD), lambda b, h: (b, h, 0, 0)),
                  pl.BlockSpec((None, V_HEAD, seq), lambda b, h: (b, h, 0))],
        out_specs=pl.BlockSpec((None, seq, V_HEAD), lambda b, h: (b, 0, h)),
        out_shape=jax.ShapeDtypeStruct((batch, seq, MLA_WIDTH), BF16),
        scratch_shapes=[pltpu.VMEM((seq, tq), F32), pltpu.VMEM((1, tq), F32),
                        pltpu.VMEM((seq, tq), F32), pltpu.VMEM((1, tq), F32),
                        pltpu.VMEM((1, tq_fast), F32)],
        compiler_params=_params(("arbitrary", "arbitrary")),
        name="attention",
    )(qt, k, vt)


def _merge_out_kernel(x_ref, gate_ref, attn_ref, gm_ref, gp_ref, vp_ref, vprev_ref, vnext_ref,
                      mm_ref, mp_ref, pw_ref, ps_ref, wop_ref, wom_ref, wout_ref, fg_ref, o_ref,
                      *, seq, final_norm):
    tm = x_ref.shape[0]
    t0 = (pl.program_id(0) % (seq // tm)) * tm

    gated = attn_ref[...] * gm_ref[...]

    cur = vp_ref[...].astype(F32)
    prev = jnp.where(t0 > 0, vprev_ref[...].astype(F32), 0.0)
    nxt = jnp.where(t0 + tm < seq, vnext_ref[...].astype(F32), 0.0)
    ext = jnp.concatenate([prev, cur, nxt], axis=0)
    n_ext = tm + 2 * POOL_HALO
    tok = t0 + lax.broadcasted_iota(jnp.int32, (tm, 1), 0)
    mixed = []
    p_mla = []
    mla_cols = wom_ref.shape[1] // POOL_GROUPS
    for g, w in enumerate(POOL_WINDOWS):
        p_mla.append(jnp.dot(gated, wom_ref[:, g * mla_cols:(g + 1) * mla_cols],
                             preferred_element_type=F32))
        acc = ext[:, g * POOL_GROUP_DIM:(g + 1) * POOL_GROUP_DIM]
        acc = acc + pltpu.roll(acc, 1, axis=0)
        half = 1
        while 2 * half < w:
            acc = pltpu.roll(acc, half, axis=0) + pltpu.roll(acc, n_ext - half, axis=0)
            half *= 2
        wsum = acc[POOL_HALO:POOL_HALO + tm]
        count = (jnp.minimum(tok + w // 2, seq) - jnp.maximum(tok - w // 2, 0)).astype(F32)
        pooled = wsum / count - cur[:, g * POOL_GROUP_DIM:(g + 1) * POOL_GROUP_DIM]
        mixed.append(jnp.dot(pooled.astype(BF16), pw_ref[g], preferred_element_type=F32))
    mixed = jnp.concatenate(mixed, axis=1)
    p_mla = jnp.concatenate(p_mla, axis=1)
    u = (mixed * ps_ref[...] * gp_ref[...].astype(F32)).astype(BF16)
    p_pool = jnp.dot(u, wop_ref[...], preferred_element_type=F32)

    y = mm_ref[...].astype(F32) * p_mla + mp_ref[...].astype(F32) * p_pool
    r = jnp.dot(y.astype(BF16), wout_ref[...], preferred_element_type=F32)
    xo = x_ref[...] + gate_ref[...] * r
    if final_norm:
        xo = xo * lax.rsqrt(jnp.mean(xo * xo, axis=-1, keepdims=True) + EPS) * fg_ref[...]
    o_ref[...] = xo


def _merge_out(xt, mod4, attn, zbig, pool_w, pool_scale, w_o_pool, w_o_mla, w_out, final_g,
               seq, final_norm):
    T, D = xt.shape
    tm = 256
    per_b = seq // tm
    halo_per_tile = tm // POOL_HALO
    n_halo = T // POOL_HALO
    vp_blk = MLA_WIDTH // POOL_WIDTH
    gp_blk = vp_blk + 1
    mm_blk = (MLA_WIDTH + 2 * POOL_WIDTH) // D
    resident = functools.partial(pl.BlockSpec, pipeline_mode=pl.Buffered(1))
    kern = functools.partial(_merge_out_kernel, seq=seq, final_norm=final_norm)
    return pl.pallas_call(
        kern,
        grid=(T // tm,),
        in_specs=[pl.BlockSpec((tm, D), lambda i: (i, 0)),
                  pl.BlockSpec((None, None, 1, D), lambda i: (i // per_b, 2, 0, 0)),
                  pl.BlockSpec((tm, MLA_WIDTH), lambda i: (i, 0)),
                  pl.BlockSpec((tm, MLA_WIDTH), lambda i: (i, 0)),
                  pl.BlockSpec((tm, POOL_WIDTH), lambda i: (i, gp_blk)),
                  pl.BlockSpec((tm, POOL_WIDTH), lambda i: (i, vp_blk)),
                  pl.BlockSpec((POOL_HALO, POOL_WIDTH),
                               lambda i: (jnp.maximum(i * halo_per_tile - 1, 0), vp_blk)),
                  pl.BlockSpec((POOL_HALO, POOL_WIDTH),
                               lambda i: (jnp.minimum((i + 1) * halo_per_tile, n_halo - 1), vp_blk)),
                  pl.BlockSpec((tm, D), lambda i: (i, mm_blk)),
                  pl.BlockSpec((tm, D), lambda i: (i, mm_blk + 1)),
                  resident(pool_w.shape, lambda i: (0, 0, 0)),
                  resident((1, POOL_WIDTH), lambda i: (0, 0)),
                  resident(w_o_pool.shape, lambda i: (0, 0)),
                  resident(w_o_mla.shape, lambda i: (0, 0)),
                  resident(w_out.shape, lambda i: (0, 0)),
                  resident((1, D), lambda i: (0, 0))],
        out_specs=pl.BlockSpec((tm, D), lambda i: (i, 0)),
        out_shape=jax.ShapeDtypeStruct((T, D), F32),
        compiler_params=_params(("arbitrary",)),
        name="merge_out",
    )(xt, mod4, attn, zbig, zbig, zbig, zbig, zbig, zbig, zbig,
      pool_w, pool_scale.reshape(1, POOL_WIDTH), w_o_pool, w_o_mla, w_out, final_g.reshape(1, D))


def _split_up_proj(w_uq, w_ukv):
    wq_t = w_uq.T
    wkv = w_ukv.reshape(KV_LORA, N_HEADS, QK_NOPE + V_HEAD)
    wk = wkv[:, :, :QK_NOPE].reshape(KV_LORA, N_HEADS * QK_NOPE)
    wv_t = wkv[:, :, QK_NOPE:].reshape(KV_LORA, MLA_WIDTH).T
    return wq_t.astype(BF16), wk.astype(BF16), wv_t.astype(BF16)


def kernel(x, c, positions, ada_w, ada_b, norm_g, w_in, q_norm_g, w_uq, kv_norm_g, w_ukv, w_o_mla,
           pool_w, pool_scale, w_o_pool, w_out, final_g):
    B, S, D = x.shape
    depth = ada_w.shape[0]
    inv_freq = 1.0 / (ROPE_THETA ** (jnp.arange(0, QK_ROPE, 2, dtype=F32) / QK_ROPE))
    inv_signed = jnp.concatenate([-inv_freq, inv_freq])
    posr = positions.reshape(B, 1, S)
    q_scale = QK_HEAD ** -0.5 * math.log2(math.e)

    xt = x.reshape(B * S, D)
    for l in range(depth):
        mod4 = _adaln(c, ada_w[l], ada_b[l]).reshape(B, 3, 1, D)
        wq_t, wk, wv_t = _split_up_proj(w_uq[l], w_ukv[l])

        w_in_t = w_in[l].T
        h, zs = _norm_proj(xt, mod4, norm_g[l], w_in_t, S)
        zbig = _gate_proj(h, w_in_t)
        qt, k, vt = _mla_prep(zs, posr, inv_signed, q_norm_g[l], kv_norm_g[l],
                              wq_t, wk, wv_t, B, S, q_scale)
        attn = _attention(qt, k, vt, B, S).reshape(B * S, MLA_WIDTH)
        xt = _merge_out(xt, mod4, attn, zbig, pool_w[l].astype(BF16), pool_scale[l],
                        w_o_pool[l].astype(BF16), w_o_mla[l].astype(BF16), w_out[l].astype(BF16),
                        final_g, S, final_norm=(l == depth - 1))
    return xt.reshape(B, S, D)
```

```python
import functools
import math

import jax
import jax.numpy as jnp
from jax import lax
from jax.experimental import pallas as pl
from jax.experimental.pallas import tpu as pltpu

EPS = 1e-6
N_HEADS = 16
QK_NOPE = 128
QK_ROPE = 64
QK_HEAD = QK_NOPE + QK_ROPE
V_HEAD = 128
Q_LORA = 512
KV_LORA = 512
MLA_WIDTH = N_HEADS * V_HEAD
ROPE_THETA = 10000.0
POOL_WINDOWS = (2, 4, 8, 16)
POOL_GROUPS = len(POOL_WINDOWS)
POOL_GROUP_DIM = 256
POOL_WIDTH = POOL_GROUPS * POOL_GROUP_DIM
POOL_HALO = 16
SMALL_WIDTH = Q_LORA + KV_LORA + 2 * QK_ROPE

V7X_VMEM_LIMIT = 56 * 1024 * 1024
BF16_SUBLANES = 16
MIN_SOFTMAX_MASS = 2.0 ** -60

F32 = jnp.float32
BF16 = jnp.bfloat16
NT_DIMS = (((1,), (1,)), ((), ()))


def _sigmoid(v):
    return 0.5 * jnp.tanh(0.5 * v) + 0.5


def _params(semantics, vmem=V7X_VMEM_LIMIT, flags=None):
    return pltpu.CompilerParams(dimension_semantics=semantics, vmem_limit_bytes=vmem, flags=flags)


def _adaln_kernel(ct_ref, w_ref, b_ref, o_ref):
    w = w_ref[...]
    for b in range(ct_ref.shape[1]):
        cb = ct_ref[:, b:b + 1]
        act = cb * _sigmoid(cb)
        o_ref[b:b + 1, :] = jnp.sum(w * act, axis=0, keepdims=True) + b_ref[...]


def _adaln(c, w, bias):
    B, D = c.shape
    n = w.shape[1]
    tn = 1024
    return pl.pallas_call(
        _adaln_kernel,
        grid=(n // tn,),
        in_specs=[pl.BlockSpec((D, B), lambda j: (0, 0)),
                  pl.BlockSpec((D, tn), lambda j: (0, j)),
                  pl.BlockSpec((1, tn), lambda j: (0, j))],
        out_specs=pl.BlockSpec((B, tn), lambda j: (0, j)),
        out_shape=jax.ShapeDtypeStruct((B, n), F32),
        compiler_params=_params(("arbitrary",)),
        name="adaln",
    )(c.T, w, bias.reshape(1, n))


def _norm_proj_kernel(x_ref, shift_ref, scale_ref, g_ref, ws_ref, h_ref, zs_ref, ws_bf16):
    @pl.when(pl.program_id(0) == 0)
    def _():
        ws_bf16[...] = ws_ref[...].astype(BF16)

    x = x_ref[...]
    y = x * lax.rsqrt(jnp.mean(x * x, axis=-1, keepdims=True) + EPS) * g_ref[...]
    h = (y * (1.0 + scale_ref[...]) + shift_ref[...]).astype(BF16)
    h_ref[...] = h
    zs_ref[...] = lax.dot_general(h, ws_bf16[...], NT_DIMS, preferred_element_type=F32)


def _norm_proj(xt, mod4, norm_g, w_in_t, seq):
    T, D = xt.shape
    tm = 512
    per_b = seq // tm
    return pl.pallas_call(
        _norm_proj_kernel,
        grid=(T // tm,),
        in_specs=[pl.BlockSpec((tm, D), lambda i: (i, 0)),
                  pl.BlockSpec((None, None, 1, D), lambda i: (i // per_b, 0, 0, 0)),
                  pl.BlockSpec((None, None, 1, D), lambda i: (i // per_b, 1, 0, 0)),
                  pl.BlockSpec((1, D), lambda i: (0, 0)),
                  pl.BlockSpec((SMALL_WIDTH, D), lambda i: (0, 0), pipeline_mode=pl.Buffered(1))],
        out_specs=[pl.BlockSpec((tm, D), lambda i: (i, 0)),
                   pl.BlockSpec((tm, SMALL_WIDTH), lambda i: (i, 0))],
        out_shape=[jax.ShapeDtypeStruct((T, D), BF16),
                   jax.ShapeDtypeStruct((T, SMALL_WIDTH), F32)],
        scratch_shapes=[pltpu.VMEM((SMALL_WIDTH, D), BF16)],
        compiler_params=_params(("arbitrary",)),
        name="norm_proj",
    )(xt, mod4, mod4, norm_g.reshape(1, D), w_in_t)


_ACTIVATIONS = {
    "silu": lambda a: a * _sigmoid(a),
    "linear": lambda a: a,
    "sigmoid": _sigmoid,
}


def _gate_proj_kernel(h_ref, wa_ref, wb_ref, o_ref, w_bf16, *, tile_kinds, shift):
    j = pl.program_id(0)
    tn = w_bf16.shape[0]

    @pl.when(pl.program_id(1) == 0)
    def _():
        w_bf16[0:tn - shift, :] = wa_ref[shift:, :].astype(BF16)
        w_bf16[tn - shift:, :] = wb_ref[...].astype(BF16)

    for kind, act in _ACTIVATIONS.items():
        tiles = [t for t, k in enumerate(tile_kinds) if k == kind]
        cond = functools.reduce(jnp.logical_or, [j == t for t in tiles])

        @pl.when(cond)
        def _(act=act):
            acc = lax.dot_general(h_ref[...], w_bf16[...], NT_DIMS, preferred_element_type=F32)
            o_ref[...] = act(acc).astype(o_ref.dtype)


def _gate_proj(h, w_in_t):
    T, D = h.shape
    tm, tn = 1024, 1024
    start = Q_LORA + KV_LORA + QK_ROPE
    n = w_in_t.shape[0] - start
    first_blk, shift = divmod(start, tn)
    assert n % tn == 0 and tn % shift == 0 and shift % BF16_SUBLANES == 0
    tile_kinds = (("silu",) * (MLA_WIDTH // tn) + ("linear",) * (POOL_WIDTH // tn)
                  + ("silu",) * (POOL_WIDTH // tn) + ("sigmoid",) * (2 * D // tn))
    kern = functools.partial(_gate_proj_kernel, tile_kinds=tile_kinds, shift=shift)
    return pl.pallas_call(
        kern,
        grid=(n // tn, T // tm),
        in_specs=[pl.BlockSpec((tm, D), lambda j, i: (i, 0)),
                  pl.BlockSpec((tn, D), lambda j, i: (first_blk + j, 0)),
                  pl.BlockSpec((shift, D), lambda j, i: ((first_blk + j + 1) * (tn // shift), 0))],
        out_specs=pl.BlockSpec((tm, tn), lambda j, i: (i, j)),
        out_shape=jax.ShapeDtypeStruct((T, n), BF16),
        scratch_shapes=[pltpu.VMEM((tn, D), BF16)],
        compiler_params=_params(("arbitrary", "arbitrary")),
        name="gate_proj",
    )(h, w_in_t, w_in_t)


def _mla_prep_kernel(zs_ref, posr_ref, invc_ref, qg_ref, kvg_ref,
                     wq_ref, wk_ref, wv_ref, qt_ref, k_ref, vt_ref, *, q_scale):
    def rms(v, g):
        return (v * lax.rsqrt(jnp.mean(v * v, axis=-1, keepdims=True) + EPS) * g).astype(BF16)

    cqn = rms(zs_ref[:, 0:Q_LORA], qg_ref[...])
    ckvn = rms(zs_ref[:, Q_LORA:Q_LORA + KV_LORA], kvg_ref[...])
    kr = zs_ref[:, Q_LORA + KV_LORA:Q_LORA + KV_LORA + QK_ROPE]
    kr_sw = jnp.concatenate([kr[:, QK_ROPE // 2:], kr[:, :QK_ROPE // 2]], axis=1)

    ang_t = invc_ref[...] * posr_ref[...].astype(F32)
    cos_t, sin_t = jnp.cos(ang_t), jnp.sin(ang_t)
    cos, sin = cos_t.T, sin_t.T

    qf = lax.dot_general(wq_ref[...], cqn, NT_DIMS, preferred_element_type=F32)
    half = QK_ROPE // 2
    for h in range(N_HEADS):
        r0 = h * QK_HEAD + QK_NOPE
        qt_ref[h * QK_HEAD:r0, :] = (qf[h * QK_HEAD:r0] * q_scale).astype(BF16)
        rope = qf[r0:r0 + QK_ROPE]
        rope_sw = jnp.concatenate([rope[half:], rope[:half]], axis=0)
        qt_ref[r0:r0 + QK_ROPE, :] = ((rope * cos_t + rope_sw * sin_t) * q_scale).astype(BF16)

    kn = jnp.dot(ckvn, wk_ref[...], preferred_element_type=F32)
    k_rot = (kr * cos + kr_sw * sin).astype(BF16)
    for h in range(N_HEADS):
        k_ref[h, :, 0:QK_NOPE] = kn[:, h * QK_NOPE:(h + 1) * QK_NOPE].astype(BF16)
        k_ref[h, :, QK_NOPE:QK_HEAD] = k_rot

    vt_ref[...] = lax.dot_general(wv_ref[...], ckvn, NT_DIMS,
                                  preferred_element_type=F32).astype(BF16)


def _mla_prep(zs, posr, inv_signed, q_norm_g, kv_norm_g, wq_t, wk, wv_t, batch, seq, q_scale):
    tm = 256
    per_b = seq // tm
    const = lambda i: (0, 0)
    kern = functools.partial(_mla_prep_kernel, q_scale=q_scale)
    return pl.pallas_call(
        kern,
        grid=(batch * per_b,),
        in_specs=[pl.BlockSpec((tm, SMALL_WIDTH), lambda i: (i, 0)),
                  pl.BlockSpec((None, 1, tm), lambda i: (i // per_b, 0, i % per_b)),
                  pl.BlockSpec((QK_ROPE, 1), const),
                  pl.BlockSpec((1, Q_LORA), const),
                  pl.BlockSpec((1, KV_LORA), const),
                  pl.BlockSpec(wq_t.shape, const),
                  pl.BlockSpec(wk.shape, const),
                  pl.BlockSpec(wv_t.shape, const)],
        out_specs=[pl.BlockSpec((None, N_HEADS * QK_HEAD, tm), lambda i: (i // per_b, 0, i % per_b)),
                   pl.BlockSpec((None, N_HEADS, tm, QK_HEAD), lambda i: (i // per_b, 0, i % per_b, 0)),
                   pl.BlockSpec((None, MLA_WIDTH, tm), lambda i: (i // per_b, 0, i % per_b))],
        out_shape=[jax.ShapeDtypeStruct((batch, N_HEADS * QK_HEAD, seq), BF16),
                   jax.ShapeDtypeStruct((batch, N_HEADS, seq, QK_HEAD), BF16),
                   jax.ShapeDtypeStruct((batch, MLA_WIDTH, seq), BF16)],
        compiler_params=_params(("arbitrary",)),
        name="mla_prep",
    )(zs, posr, inv_signed.reshape(QK_ROPE, 1),
      q_norm_g.reshape(1, Q_LORA), kv_norm_g.reshape(1, KV_LORA), wq_t, wk, wv_t)


def _attention_kernel(qt_ref, k_ref, vt_ref, *refs, n_cast):
    cast_in, (o_ref, *cast_out) = refs[:n_cast], refs[n_cast:2 * n_cast + 1]
    s_a, m_a, s_b, m_b, lmin_ref = refs[2 * n_cast + 1:]
    for w_in, w_out in zip(cast_in, cast_out):
        w_out[...] = w_in[...].astype(w_out.dtype)

    tq = lmin_ref.shape[1]
    nq = qt_ref.shape[1] // tq

    kf = k_ref[...].astype(F32)
    k_norm = jnp.sqrt(jnp.max(jnp.sum(kf * kf, axis=1, keepdims=True), axis=0, keepdims=True))
    lmin_ref[...] = jnp.full(lmin_ref.shape, jnp.inf, F32)

    def fast_tile(i, carry):
        off = pl.multiple_of(i * tq, tq)
        qt = qt_ref[:, pl.ds(off, tq)]
        qf = qt.astype(F32)
        shift = jnp.sqrt(jnp.sum(qf * qf, axis=0, keepdims=True)) * k_norm
        s = jnp.dot(k_ref[...], qt, preferred_element_type=F32)
        p = jnp.exp2(s - shift)
        l = jnp.sum(p, axis=0, keepdims=True)
        ot = jnp.dot(vt_ref[...], p.astype(BF16), preferred_element_type=F32)
        o_ref[pl.ds(off, tq), :] = (ot / l).T.astype(o_ref.dtype)
        lmin_ref[...] = jnp.minimum(lmin_ref[...], l)
        return carry

    lax.fori_loop(0, nq, fast_tile, 0, unroll=4)
    trusted = jnp.min(lmin_ref[...]) >= MIN_SOFTMAX_MASS

    @pl.when(jnp.logical_not(trusted))
    def _():
        _attention_exact(qt_ref, k_ref, vt_ref, o_ref, s_a, m_a, s_b, m_b)


def _attention_exact(qt_ref, k_ref, vt_ref, o_ref, s_a, m_a, s_b, m_b):
    tq = s_a.shape[1]
    nq = qt_ref.shape[1] // tq

    def scores(i, s_ref, m_ref):
        off = pl.multiple_of(i * tq, tq)
        s = jnp.dot(k_ref[...], qt_ref[:, pl.ds(off, tq)], preferred_element_type=F32)
        s_ref[...] = s
        m_ref[...] = jnp.max(s, axis=0, keepdims=True)

    def finish(i, s_ref, m_ref):
        off = pl.multiple_of(i * tq, tq)
        p = jnp.exp2(s_ref[...] - m_ref[...])
        l = jnp.sum(p, axis=0, keepdims=True)
        ot = jnp.dot(vt_ref[...], p.astype(BF16), preferred_element_type=F32)
        o_ref[pl.ds(off, tq), :] = (ot / l).T.astype(o_ref.dtype)

    scores(0, s_a, m_a)

    def step(i, carry):
        @pl.when(i % 2 == 1)
        def _():
            scores(i, s_b, m_b)
            finish(i - 1, s_a, m_a)

        @pl.when(i % 2 == 0)
        def _():
            scores(i, s_a, m_a)
            finish(i - 1, s_b, m_b)

        return carry

    lax.fori_loop(1, nq, step, 0)
    finish(nq - 1, s_b, m_b)


def _attention(qt, k, vt, batch, seq, later_weights):
    tq = 256
    tq_fast = 512
    assert seq % (2 * tq) == 0 and seq % tq_fast == 0
    steps = batch * N_HEADS
    slabs = [w.shape[0] // steps for w in later_weights]
    assert all(s % BF16_SUBLANES == 0 and s * steps == w.shape[0]
               for s, w in zip(slabs, later_weights))
    step = lambda b, h: (b * N_HEADS + h, 0)
    cast_specs = [pl.BlockSpec((s, w.shape[1]), step) for s, w in zip(slabs, later_weights)]
    outs = pl.pallas_call(
        functools.partial(_attention_kernel, n_cast=len(later_weights)),
        grid=(batch, N_HEADS),
        in_specs=[pl.BlockSpec((None, QK_HEAD, seq), lambda b, h: (b, h, 0)),
                  pl.BlockSpec((None, None, seq, QK_HEAD), lambda b, h: (b, h, 0, 0)),
                  pl.BlockSpec((None, V_HEAD, seq), lambda b, h: (b, h, 0))] + cast_specs,
        out_specs=[pl.BlockSpec((None, seq, V_HEAD), lambda b, h: (b, 0, h))] + cast_specs,
        out_shape=[jax.ShapeDtypeStruct((batch, seq, MLA_WIDTH), BF16)]
                  + [jax.ShapeDtypeStruct(w.shape, BF16) for w in later_weights],
        scratch_shapes=[pltpu.VMEM((seq, tq), F32), pltpu.VMEM((1, tq), F32),
                        pltpu.VMEM((seq, tq), F32), pltpu.VMEM((1, tq), F32),
                        pltpu.VMEM((1, tq_fast), F32)],
        compiler_params=_params(("arbitrary", "arbitrary")),
        name="attention",
    )(qt, k, vt, *later_weights)
    return outs[0], outs[1:]


def _merge_out_kernel(x_ref, gate_ref, attn_ref, gm_ref, gp_ref, vp_ref, vprev_ref, vnext_ref,
                      mm_ref, mp_ref, pw_ref, ps_ref, wop_ref, wom_ref, wout_ref, fg_ref, o_ref,
                      *, seq, final_norm):
    tm = x_ref.shape[0]
    t0 = (pl.program_id(0) % (seq // tm)) * tm

    gated = attn_ref[...] * gm_ref[...]

    cur = vp_ref[...].astype(F32)
    prev = jnp.where(t0 > 0, vprev_ref[...].astype(F32), 0.0)
    nxt = jnp.where(t0 + tm < seq, vnext_ref[...].astype(F32), 0.0)
    ext = jnp.concatenate([prev, cur, nxt], axis=0)
    n_ext = tm + 2 * POOL_HALO
    tok = t0 + lax.broadcasted_iota(jnp.int32, (tm, 1), 0)
    mixed = []
    p_mla = []
    mla_cols = wom_ref.shape[1] // POOL_GROUPS
    for g, w in enumerate(POOL_WINDOWS):
        p_mla.append(jnp.dot(gated, wom_ref[:, g * mla_cols:(g + 1) * mla_cols],
                             preferred_element_type=F32))
        acc = ext[:, g * POOL_GROUP_DIM:(g + 1) * POOL_GROUP_DIM]
        acc = acc + pltpu.roll(acc, 1, axis=0)
        half = 1
        while 2 * half < w:
            acc = pltpu.roll(acc, half, axis=0) + pltpu.roll(acc, n_ext - half, axis=0)
            half *= 2
        wsum = acc[POOL_HALO:POOL_HALO + tm]
        count = (jnp.minimum(tok + w // 2, seq) - jnp.maximum(tok - w // 2, 0)).astype(F32)
        pooled = wsum / count - cur[:, g * POOL_GROUP_DIM:(g + 1) * POOL_GROUP_DIM]
        mixed.append(jnp.dot(pooled.astype(BF16), pw_ref[g], preferred_element_type=F32))
    mixed = jnp.concatenate(mixed, axis=1)
    p_mla = jnp.concatenate(p_mla, axis=1)
    u = (mixed * ps_ref[...] * gp_ref[...].astype(F32)).astype(BF16)
    p_pool = jnp.dot(u, wop_ref[...], preferred_element_type=F32)

    y = mm_ref[...].astype(F32) * p_mla + mp_ref[...].astype(F32) * p_pool
    r = jnp.dot(y.astype(BF16), wout_ref[...], preferred_element_type=F32)
    xo = x_ref[...] + gate_ref[...] * r
    if final_norm:
        xo = xo * lax.rsqrt(jnp.mean(xo * xo, axis=-1, keepdims=True) + EPS) * fg_ref[...]
    o_ref[...] = xo


def _merge_out(xt, mod4, attn, zbig, pool_w, pool_scale, w_o_pool, w_o_mla, w_out, final_g,
               seq, final_norm):
    T, D = xt.shape
    tm = 256
    per_b = seq // tm
    halo_per_tile = tm // POOL_HALO
    n_halo = T // POOL_HALO
    vp_blk = MLA_WIDTH // POOL_WIDTH
    gp_blk = vp_blk + 1
    mm_blk = (MLA_WIDTH + 2 * POOL_WIDTH) // D
    resident = functools.partial(pl.BlockSpec, pipeline_mode=pl.Buffered(1))
    kern = functools.partial(_merge_out_kernel, seq=seq, final_norm=final_norm)
    return pl.pallas_call(
        kern,
        grid=(T // tm,),
        in_specs=[pl.BlockSpec((tm, D), lambda i: (i, 0)),
                  pl.BlockSpec((None, None, 1, D), lambda i: (i // per_b, 2, 0, 0)),
                  pl.BlockSpec((tm, MLA_WIDTH), lambda i: (i, 0)),
                  pl.BlockSpec((tm, MLA_WIDTH), lambda i: (i, 0)),
                  pl.BlockSpec((tm, POOL_WIDTH), lambda i: (i, gp_blk)),
                  pl.BlockSpec((tm, POOL_WIDTH), lambda i: (i, vp_blk)),
                  pl.BlockSpec((POOL_HALO, POOL_WIDTH),
                               lambda i: (jnp.maximum(i * halo_per_tile - 1, 0), vp_blk)),
                  pl.BlockSpec((POOL_HALO, POOL_WIDTH),
                               lambda i: (jnp.minimum((i + 1) * halo_per_tile, n_halo - 1), vp_blk)),
                  pl.BlockSpec((tm, D), lambda i: (i, mm_blk)),
                  pl.BlockSpec((tm, D), lambda i: (i, mm_blk + 1)),
                  resident(pool_w.shape, lambda i: (0, 0, 0)),
                  resident((1, POOL_WIDTH), lambda i: (0, 0)),
                  resident(w_o_pool.shape, lambda i: (0, 0)),
                  resident(w_o_mla.shape, lambda i: (0, 0)),
                  resident(w_out.shape, lambda i: (0, 0)),
                  resident((1, D), lambda i: (0, 0))],
        out_specs=pl.BlockSpec((tm, D), lambda i: (i, 0)),
        out_shape=jax.ShapeDtypeStruct((T, D), F32),
        compiler_params=_params(("arbitrary",)),
        name="merge_out",
    )(xt, mod4, attn, zbig, zbig, zbig, zbig, zbig, zbig, zbig,
      pool_w, pool_scale.reshape(1, POOL_WIDTH), w_o_pool, w_o_mla, w_out, final_g.reshape(1, D))


def _split_up_proj(w_uq, w_ukv):
    wq_t = w_uq.T
    wkv = w_ukv.reshape(KV_LORA, N_HEADS, QK_NOPE + V_HEAD)
    wk = wkv[:, :, :QK_NOPE].reshape(KV_LORA, N_HEADS * QK_NOPE)
    wv_t = wkv[:, :, QK_NOPE:].reshape(KV_LORA, MLA_WIDTH).T
    return wq_t.astype(BF16), wk.astype(BF16), wv_t.astype(BF16)


def kernel(x, c, positions, ada_w, ada_b, norm_g, w_in, q_norm_g, w_uq, kv_norm_g, w_ukv, w_o_mla,
           pool_w, pool_scale, w_o_pool, w_out, final_g):
    B, S, D = x.shape
    depth = ada_w.shape[0]
    inv_freq = 1.0 / (ROPE_THETA ** (jnp.arange(0, QK_ROPE, 2, dtype=F32) / QK_ROPE))
    inv_signed = jnp.concatenate([-inv_freq, inv_freq])
    posr = positions.reshape(B, 1, S)
    q_scale = QK_HEAD ** -0.5 * math.log2(math.e)

    xt = x.reshape(B * S, D)
    for l in range(depth):
        mod4 = _adaln(c, ada_w[l], ada_b[l]).reshape(B, 3, 1, D)
        wq_t, wk, wv_t = _split_up_proj(w_uq[l], w_ukv[l])

        w_in_t = w_in[l].T
        h, zs = _norm_proj(xt, mod4, norm_g[l], w_in_t, S)
        zbig = _gate_proj(h, w_in_t)
        qt, k, vt = _mla_prep(zs, posr, inv_signed, q_norm_g[l], kv_norm_g[l],
                              wq_t, wk, wv_t, B, S, q_scale)
        attn, (pw, wop, wom, wout) = _attention(
            qt, k, vt, B, S,
            (pool_w[l].reshape(POOL_WIDTH, POOL_GROUP_DIM), w_o_pool[l], w_o_mla[l], w_out[l]))
        xt = _merge_out(xt, mod4, attn.reshape(B * S, MLA_WIDTH), zbig,
                        pw.reshape(POOL_GROUPS, POOL_GROUP_DIM, POOL_GROUP_DIM), pool_scale[l],
                        wop, wom, wout, final_g, S, final_norm=(l == depth - 1))
    return xt.reshape(B, S, D)
```

```python
import functools
import math

import jax
import jax.numpy as jnp
from jax import lax
from jax.experimental import pallas as pl
from jax.experimental.pallas import tpu as pltpu

EPS = 1e-6
N_HEADS = 16
QK_NOPE = 128
QK_ROPE = 64
QK_HEAD = QK_NOPE + QK_ROPE
V_HEAD = 128
Q_LORA = 512
KV_LORA = 512
MLA_WIDTH = N_HEADS * V_HEAD
ROPE_THETA = 10000.0
POOL_WINDOWS = (2, 4, 8, 16)
POOL_GROUPS = len(POOL_WINDOWS)
POOL_GROUP_DIM = 256
POOL_WIDTH = POOL_GROUPS * POOL_GROUP_DIM
POOL_HALO = 16
SMALL_WIDTH = Q_LORA + KV_LORA + 2 * QK_ROPE

V7X_VMEM_LIMIT = 56 * 1024 * 1024
BF16_SUBLANES = 16
MIN_SOFTMAX_MASS = 2.0 ** -60

F32 = jnp.float32
BF16 = jnp.bfloat16
NT_DIMS = (((1,), (1,)), ((), ()))


def _sigmoid(v):
    return 0.5 * jnp.tanh(0.5 * v) + 0.5


def _params(semantics, vmem=V7X_VMEM_LIMIT, flags=None):
    return pltpu.CompilerParams(dimension_semantics=semantics, vmem_limit_bytes=vmem, flags=flags)


def _adaln_kernel(ct_ref, w_ref, b_ref, o_ref):
    w = w_ref[...]
    for b in range(ct_ref.shape[1]):
        cb = ct_ref[:, b:b + 1]
        act = cb * _sigmoid(cb)
        o_ref[b:b + 1, :] = jnp.sum(w * act, axis=0, keepdims=True) + b_ref[...]


def _adaln(c, w, bias):
    B, D = c.shape
    n = w.shape[1]
    tn = 1024
    return pl.pallas_call(
        _adaln_kernel,
        grid=(n // tn,),
        in_specs=[pl.BlockSpec((D, B), lambda j: (0, 0)),
                  pl.BlockSpec((D, tn), lambda j: (0, j)),
                  pl.BlockSpec((1, tn), lambda j: (0, j))],
        out_specs=pl.BlockSpec((B, tn), lambda j: (0, j)),
        out_shape=jax.ShapeDtypeStruct((B, n), F32),
        compiler_params=_params(("arbitrary",)),
        name="adaln",
    )(c.T, w, bias.reshape(1, n))


NORM_ROW_CHUNKS = 4


def _norm_proj_kernel(x_ref, shift_ref, scale_ref, g_ref, ws_ref, h_ref, zs_ref, ws_bf16):
    @pl.when(pl.program_id(0) == 0)
    def _():
        ws_bf16[...] = ws_ref[...].T.astype(BF16)

    rows = x_ref.shape[0] // NORM_ROW_CHUNKS
    for c in range(NORM_ROW_CHUNKS):
        sl = slice(c * rows, (c + 1) * rows)
        x = x_ref[sl, :]
        y = x * lax.rsqrt(jnp.mean(x * x, axis=-1, keepdims=True) + EPS) * g_ref[...]
        h = (y * (1.0 + scale_ref[...]) + shift_ref[...]).astype(BF16)
        h_ref[sl, :] = h
        zs_ref[sl, :] = jnp.dot(h, ws_bf16[...], preferred_element_type=F32)


def _norm_proj(xt, mod4, norm_g, w_in_t, seq):
    T, D = xt.shape
    tm = 512
    per_b = seq // tm
    return pl.pallas_call(
        _norm_proj_kernel,
        grid=(T // tm,),
        in_specs=[pl.BlockSpec((tm, D), lambda i: (i, 0)),
                  pl.BlockSpec((None, None, 1, D), lambda i: (i // per_b, 0, 0, 0)),
                  pl.BlockSpec((None, None, 1, D), lambda i: (i // per_b, 1, 0, 0)),
                  pl.BlockSpec((1, D), lambda i: (0, 0)),
                  pl.BlockSpec((SMALL_WIDTH, D), lambda i: (0, 0), pipeline_mode=pl.Buffered(1))],
        out_specs=[pl.BlockSpec((tm, D), lambda i: (i, 0)),
                   pl.BlockSpec((tm, SMALL_WIDTH), lambda i: (i, 0))],
        out_shape=[jax.ShapeDtypeStruct((T, D), BF16),
                   jax.ShapeDtypeStruct((T, SMALL_WIDTH), F32)],
        scratch_shapes=[pltpu.VMEM((D, SMALL_WIDTH), BF16)],
        compiler_params=_params(("arbitrary",)),
        name="norm_proj",
    )(xt, mod4, mod4, norm_g.reshape(1, D), w_in_t)


_ACTIVATIONS = {
    "silu": lambda a: a * _sigmoid(a),
    "linear": lambda a: a,
    "sigmoid": _sigmoid,
}


def _gate_proj_kernel(h_ref, wa_ref, wb_ref, o_ref, w_bf16, *, tile_kinds, shift):
    j = pl.program_id(0)
    tn = w_bf16.shape[0]

    @pl.when(pl.program_id(1) == 0)
    def _():
        w_bf16[0:tn - shift, :] = wa_ref[shift:, :].astype(BF16)
        w_bf16[tn - shift:, :] = wb_ref[...].astype(BF16)

    for kind, act in _ACTIVATIONS.items():
        tiles = [t for t, k in enumerate(tile_kinds) if k == kind]
        cond = functools.reduce(jnp.logical_or, [j == t for t in tiles])

        @pl.when(cond)
        def _(act=act):
            acc = lax.dot_general(h_ref[...], w_bf16[...], NT_DIMS, preferred_element_type=F32)
            o_ref[...] = act(acc).astype(o_ref.dtype)


def _gate_proj(h, w_in_t):
    T, D = h.shape
    tm, tn = 1024, 1024
    start = Q_LORA + KV_LORA + QK_ROPE
    n = w_in_t.shape[0] - start
    first_blk, shift = divmod(start, tn)
    assert n % tn == 0 and tn % shift == 0 and shift % BF16_SUBLANES == 0
    tile_kinds = (("silu",) * (MLA_WIDTH // tn) + ("linear",) * (POOL_WIDTH // tn)
                  + ("silu",) * (POOL_WIDTH // tn) + ("sigmoid",) * (2 * D // tn))
    kern = functools.partial(_gate_proj_kernel, tile_kinds=tile_kinds, shift=shift)
    return pl.pallas_call(
        kern,
        grid=(n // tn, T // tm),
        in_specs=[pl.BlockSpec((tm, D), lambda j, i: (i, 0)),
                  pl.BlockSpec((tn, D), lambda j, i: (first_blk + j, 0)),
                  pl.BlockSpec((shift, D), lambda j, i: ((first_blk + j + 1) * (tn // shift), 0))],
        out_specs=pl.BlockSpec((tm, tn), lambda j, i: (i, j)),
        out_shape=jax.ShapeDtypeStruct((T, n), BF16),
        scratch_shapes=[pltpu.VMEM((tn, D), BF16)],
        compiler_params=_params(("arbitrary", "arbitrary")),
        name="gate_proj",
    )(h, w_in_t, w_in_t)


def _mla_prep_kernel(zs_ref, posr_ref, invc_ref, qg_ref, kvg_ref,
                     wq_ref, wk_ref, wv_ref, qt_ref, k_ref, vt_ref, *, q_scale):
    def rms(v, g):
        return (v * lax.rsqrt(jnp.mean(v * v, axis=-1, keepdims=True) + EPS) * g).astype(BF16)

    cqn = rms(zs_ref[:, 0:Q_LORA], qg_ref[...])
    ckvn = rms(zs_ref[:, Q_LORA:Q_LORA + KV_LORA], kvg_ref[...])
    kr = zs_ref[:, Q_LORA + KV_LORA:Q_LORA + KV_LORA + QK_ROPE]
    kr_sw = jnp.concatenate([kr[:, QK_ROPE // 2:], kr[:, :QK_ROPE // 2]], axis=1)

    ang_t = invc_ref[...] * posr_ref[...].astype(F32)
    cos_t, sin_t = jnp.cos(ang_t), jnp.sin(ang_t)
    cos, sin = cos_t.T, sin_t.T

    qf = lax.dot_general(wq_ref[...], cqn, NT_DIMS, preferred_element_type=F32)
    half = QK_ROPE // 2
    for h in range(N_HEADS):
        r0 = h * QK_HEAD + QK_NOPE
        qt_ref[h * QK_HEAD:r0, :] = (qf[h * QK_HEAD:r0] * q_scale).astype(BF16)
        rope = qf[r0:r0 + QK_ROPE]
        rope_sw = jnp.concatenate([rope[half:], rope[:half]], axis=0)
        qt_ref[r0:r0 + QK_ROPE, :] = ((rope * cos_t + rope_sw * sin_t) * q_scale).astype(BF16)

    kn = jnp.dot(ckvn, wk_ref[...], preferred_element_type=F32)
    k_rot = (kr * cos + kr_sw * sin).astype(BF16)
    for h in range(N_HEADS):
        k_ref[h, :, 0:QK_NOPE] = kn[:, h * QK_NOPE:(h + 1) * QK_NOPE].astype(BF16)
        k_ref[h, :, QK_NOPE:QK_HEAD] = k_rot

    vt_ref[...] = lax.dot_general(wv_ref[...], ckvn, NT_DIMS,
                                  preferred_element_type=F32).astype(BF16)


def _mla_prep(zs, posr, inv_signed, q_norm_g, kv_norm_g, wq_t, wk, wv_t, batch, seq, q_scale):
    tm = 512
    per_b = seq // tm
    const = lambda i: (0, 0)
    resident = functools.partial(pl.BlockSpec, pipeline_mode=pl.Buffered(1))
    kern = functools.partial(_mla_prep_kernel, q_scale=q_scale)
    return pl.pallas_call(
        kern,
        grid=(batch * per_b,),
        in_specs=[pl.BlockSpec((tm, SMALL_WIDTH), lambda i: (i, 0)),
                  pl.BlockSpec((None, 1, tm), lambda i: (i // per_b, 0, i % per_b)),
                  pl.BlockSpec((QK_ROPE, 1), const),
                  pl.BlockSpec((1, Q_LORA), const),
                  pl.BlockSpec((1, KV_LORA), const),
                  resident(wq_t.shape, const),
                  resident(wk.shape, const),
                  resident(wv_t.shape, const)],
        out_specs=[pl.BlockSpec((None, N_HEADS * QK_HEAD, tm), lambda i: (i // per_b, 0, i % per_b)),
                   pl.BlockSpec((None, N_HEADS, tm, QK_HEAD), lambda i: (i // per_b, 0, i % per_b, 0)),
                   pl.BlockSpec((None, MLA_WIDTH, tm), lambda i: (i // per_b, 0, i % per_b))],
        out_shape=[jax.ShapeDtypeStruct((batch, N_HEADS * QK_HEAD, seq), BF16),
                   jax.ShapeDtypeStruct((batch, N_HEADS, seq, QK_HEAD), BF16),
                   jax.ShapeDtypeStruct((batch, MLA_WIDTH, seq), BF16)],
        compiler_params=_params(("arbitrary",)),
        name="mla_prep",
    )(zs, posr, inv_signed.reshape(QK_ROPE, 1),
      q_norm_g.reshape(1, Q_LORA), kv_norm_g.reshape(1, KV_LORA), wq_t, wk, wv_t)


def _attention_kernel(qt_ref, k_ref, vt_ref, *refs, n_cast):
    cast_in, (o_ref, *cast_out) = refs[:n_cast], refs[n_cast:2 * n_cast + 1]
    s_a, m_a, s_b, m_b, lmin_ref = refs[2 * n_cast + 1:]
    for w_in, w_out in zip(cast_in, cast_out):
        w_out[...] = w_in[...].astype(w_out.dtype)

    tq = lmin_ref.shape[1]
    nq = qt_ref.shape[1] // tq

    kf = k_ref[...].astype(F32)
    k_norm = jnp.sqrt(jnp.max(jnp.sum(kf * kf, axis=1, keepdims=True), axis=0, keepdims=True))
    lmin_ref[...] = jnp.full(lmin_ref.shape, jnp.inf, F32)

    def fast_tile(i, carry):
        off = pl.multiple_of(i * tq, tq)
        qt = qt_ref[:, pl.ds(off, tq)]
        qf = qt.astype(F32)
        shift = jnp.sqrt(jnp.sum(qf * qf, axis=0, keepdims=True)) * k_norm
        s = jnp.dot(k_ref[...], qt, preferred_element_type=F32)
        p = jnp.exp2(s - shift)
        l = jnp.sum(p, axis=0, keepdims=True)
        ot = jnp.dot(vt_ref[...], p.astype(BF16), preferred_element_type=F32)
        o_ref[pl.ds(off, tq), :] = (ot / l).T.astype(o_ref.dtype)
        lmin_ref[...] = jnp.minimum(lmin_ref[...], l)
        return carry

    lax.fori_loop(0, nq, fast_tile, 0, unroll=4)
    trusted = jnp.min(lmin_ref[...]) >= MIN_SOFTMAX_MASS

    @pl.when(jnp.logical_not(trusted))
    def _():
        _attention_exact(qt_ref, k_ref, vt_ref, o_ref, s_a, m_a, s_b, m_b)


def _attention_exact(qt_ref, k_ref, vt_ref, o_ref, s_a, m_a, s_b, m_b):
    tq = s_a.shape[1]
    nq = qt_ref.shape[1] // tq

    def scores(i, s_ref, m_ref):
        off = pl.multiple_of(i * tq, tq)
        s = jnp.dot(k_ref[...], qt_ref[:, pl.ds(off, tq)], preferred_element_type=F32)
        s_ref[...] = s
        m_ref[...] = jnp.max(s, axis=0, keepdims=True)

    def finish(i, s_ref, m_ref):
        off = pl.multiple_of(i * tq, tq)
        p = jnp.exp2(s_ref[...] - m_ref[...])
        l = jnp.sum(p, axis=0, keepdims=True)
        ot = jnp.dot(vt_ref[...], p.astype(BF16), preferred_element_type=F32)
        o_ref[pl.ds(off, tq), :] = (ot / l).T.astype(o_ref.dtype)

    scores(0, s_a, m_a)

    def step(i, carry):
        @pl.when(i % 2 == 1)
        def _():
            scores(i, s_b, m_b)
            finish(i - 1, s_a, m_a)

        @pl.when(i % 2 == 0)
        def _():
            scores(i, s_a, m_a)
            finish(i - 1, s_b, m_b)

        return carry

    lax.fori_loop(1, nq, step, 0)
    finish(nq - 1, s_b, m_b)


def _attention(qt, k, vt, batch, seq, later_weights):
    tq = 256
    tq_fast = 512
    assert seq % (2 * tq) == 0 and seq % tq_fast == 0
    steps = batch * N_HEADS
    slabs = [w.shape[0] // steps for w in later_weights]
    assert all(s % BF16_SUBLANES == 0 and s * steps == w.shape[0]
               for s, w in zip(slabs, later_weights))
    step = lambda b, h: (b * N_HEADS + h, 0)
    cast_specs = [pl.BlockSpec((s, w.shape[1]), step) for s, w in zip(slabs, later_weights)]
    outs = pl.pallas_call(
        functools.partial(_attention_kernel, n_cast=len(later_weights)),
        grid=(batch, N_HEADS),
        in_specs=[pl.BlockSpec((None, QK_HEAD, seq), lambda b, h: (b, h, 0)),
                  pl.BlockSpec((None, None, seq, QK_HEAD), lambda b, h: (b, h, 0, 0)),
                  pl.BlockSpec((None, V_HEAD, seq), lambda b, h: (b, h, 0))] + cast_specs,
        out_specs=[pl.BlockSpec((None, seq, V_HEAD), lambda b, h: (b, 0, h))] + cast_specs,
        out_shape=[jax.ShapeDtypeStruct((batch, seq, MLA_WIDTH), BF16)]
                  + [jax.ShapeDtypeStruct(w.shape, BF16) for w in later_weights],
        scratch_shapes=[pltpu.VMEM((seq, tq), F32), pltpu.VMEM((1, tq), F32),
                        pltpu.VMEM((seq, tq), F32), pltpu.VMEM((1, tq), F32),
                        pltpu.VMEM((1, tq_fast), F32)],
        compiler_params=_params(("arbitrary", "arbitrary")),
        name="attention",
    )(qt, k, vt, *later_weights)
    return outs[0], outs[1:]


def _merge_out_kernel(x_ref, gate_ref, attn_ref, gm_ref, gp_ref, vp_ref, vprev_ref, vnext_ref,
                      mm_ref, mp_ref, pw_ref, ps_ref, wop_ref, wom_ref, wout_ref, fg_ref, o_ref,
                      *, seq, final_norm):
    tm = x_ref.shape[0]
    t0 = (pl.program_id(0) % (seq // tm)) * tm

    gated = attn_ref[...] * gm_ref[...]

    cur = vp_ref[...].astype(F32)
    prev = jnp.where(t0 > 0, vprev_ref[...].astype(F32), 0.0)
    nxt = jnp.where(t0 + tm < seq, vnext_ref[...].astype(F32), 0.0)
    ext = jnp.concatenate([prev, cur, nxt], axis=0)
    n_ext = tm + 2 * POOL_HALO
    tok = t0 + lax.broadcasted_iota(jnp.int32, (tm, 1), 0)
    mixed = []
    p_mla = []
    mla_cols = wom_ref.shape[1] // POOL_GROUPS
    for g, w in enumerate(POOL_WINDOWS):
        p_mla.append(jnp.dot(gated, wom_ref[:, g * mla_cols:(g + 1) * mla_cols],
                             preferred_element_type=F32))
        acc = ext[:, g * POOL_GROUP_DIM:(g + 1) * POOL_GROUP_DIM]
        acc = acc + pltpu.roll(acc, 1, axis=0)
        half = 1
        while 2 * half < w:
            acc = pltpu.roll(acc, half, axis=0) + pltpu.roll(acc, n_ext - half, axis=0)
            half *= 2
        wsum = acc[POOL_HALO:POOL_HALO + tm]
        count = (jnp.minimum(tok + w // 2, seq) - jnp.maximum(tok - w // 2, 0)).astype(F32)
        pooled = wsum / count - cur[:, g * POOL_GROUP_DIM:(g + 1) * POOL_GROUP_DIM]
        mixed.append(jnp.dot(pooled.astype(BF16), pw_ref[g], preferred_element_type=F32))
    mixed = jnp.concatenate(mixed, axis=1)
    p_mla = jnp.concatenate(p_mla, axis=1)
    u = (mixed * ps_ref[...] * gp_ref[...].astype(F32)).astype(BF16)
    p_pool = jnp.dot(u, wop_ref[...], preferred_element_type=F32)

    y = mm_ref[...].astype(F32) * p_mla + mp_ref[...].astype(F32) * p_pool
    r = jnp.dot(y.astype(BF16), wout_ref[...], preferred_element_type=F32)
    xo = x_ref[...] + gate_ref[...] * r
    if final_norm:
        xo = xo * lax.rsqrt(jnp.mean(xo * xo, axis=-1, keepdims=True) + EPS) * fg_ref[...]
    o_ref[...] = xo


def _merge_out(xt, mod4, attn, zbig, pool_w, pool_scale, w_o_pool, w_o_mla, w_out, final_g,
               seq, final_norm):
    T, D = xt.shape
    tm = 256
    per_b = seq // tm
    halo_per_tile = tm // POOL_HALO
    n_halo = T // POOL_HALO
    vp_blk = MLA_WIDTH // POOL_WIDTH
    gp_blk = vp_blk + 1
    mm_blk = (MLA_WIDTH + 2 * POOL_WIDTH) // D
    resident = functools.partial(pl.BlockSpec, pipeline_mode=pl.Buffered(1))
    kern = functools.partial(_merge_out_kernel, seq=seq, final_norm=final_norm)
    return pl.pallas_call(
        kern,
        grid=(T // tm,),
        in_specs=[pl.BlockSpec((tm, D), lambda i: (i, 0)),
                  pl.BlockSpec((None, None, 1, D), lambda i: (i // per_b, 2, 0, 0)),
                  pl.BlockSpec((tm, MLA_WIDTH), lambda i: (i, 0)),
                  pl.BlockSpec((tm, MLA_WIDTH), lambda i: (i, 0)),
                  pl.BlockSpec((tm, POOL_WIDTH), lambda i: (i, gp_blk)),
                  pl.BlockSpec((tm, POOL_WIDTH), lambda i: (i, vp_blk)),
                  pl.BlockSpec((POOL_HALO, POOL_WIDTH),
                               lambda i: (jnp.maximum(i * halo_per_tile - 1, 0), vp_blk)),
                  pl.BlockSpec((POOL_HALO, POOL_WIDTH),
                               lambda i: (jnp.minimum((i + 1) * halo_per_tile, n_halo - 1), vp_blk)),
                  pl.BlockSpec((tm, D), lambda i: (i, mm_blk)),
                  pl.BlockSpec((tm, D), lambda i: (i, mm_blk + 1)),
                  resident(pool_w.shape, lambda i: (0, 0, 0)),
                  resident((1, POOL_WIDTH), lambda i: (0, 0)),
                  resident(w_o_pool.shape, lambda i: (0, 0)),
                  resident(w_o_mla.shape, lambda i: (0, 0)),
                  resident(w_out.shape, lambda i: (0, 0)),
                  resident((1, D), lambda i: (0, 0))],
        out_specs=pl.BlockSpec((tm, D), lambda i: (i, 0)),
        out_shape=jax.ShapeDtypeStruct((T, D), F32),
        compiler_params=_params(("arbitrary",)),
        name="merge_out",
    )(xt, mod4, attn, zbig, zbig, zbig, zbig, zbig, zbig, zbig,
      pool_w, pool_scale.reshape(1, POOL_WIDTH), w_o_pool, w_o_mla, w_out, final_g.reshape(1, D))


def _split_up_proj(w_uq, w_ukv):
    wq_t = w_uq.T
    wkv = w_ukv.reshape(KV_LORA, N_HEADS, QK_NOPE + V_HEAD)
    wk = wkv[:, :, :QK_NOPE].reshape(KV_LORA, N_HEADS * QK_NOPE)
    wv_t = wkv[:, :, QK_NOPE:].reshape(KV_LORA, MLA_WIDTH).T
    return wq_t.astype(BF16), wk.astype(BF16), wv_t.astype(BF16)


def kernel(x, c, positions, ada_w, ada_b, norm_g, w_in, q_norm_g, w_uq, kv_norm_g, w_ukv, w_o_mla,
           pool_w, pool_scale, w_o_pool, w_out, final_g):
    B, S, D = x.shape
    depth = ada_w.shape[0]
    inv_freq = 1.0 / (ROPE_THETA ** (jnp.arange(0, QK_ROPE, 2, dtype=F32) / QK_ROPE))
    inv_signed = jnp.concatenate([-inv_freq, inv_freq])
    posr = positions.reshape(B, 1, S)
    q_scale = QK_HEAD ** -0.5 * math.log2(math.e)

    xt = x.reshape(B * S, D)
    for l in range(depth):
        mod4 = _adaln(c, ada_w[l], ada_b[l]).reshape(B, 3, 1, D)
        wq_t, wk, wv_t = _split_up_proj(w_uq[l], w_ukv[l])

        w_in_t = w_in[l].T
        h, zs = _norm_proj(xt, mod4, norm_g[l], w_in_t, S)
        zbig = _gate_proj(h, w_in_t)
        qt, k, vt = _mla_prep(zs, posr, inv_signed, q_norm_g[l], kv_norm_g[l],
                              wq_t, wk, wv_t, B, S, q_scale)
        attn, (pw, wop, wom, wout) = _attention(
            qt, k, vt, B, S,
            (pool_w[l].reshape(POOL_WIDTH, POOL_GROUP_DIM), w_o_pool[l], w_o_mla[l], w_out[l]))
        xt = _merge_out(xt, mod4, attn.reshape(B * S, MLA_WIDTH), zbig,
                        pw.reshape(POOL_GROUPS, POOL_GROUP_DIM, POOL_GROUP_DIM), pool_scale[l],
                        wop, wom, wout, final_g, S, final_norm=(l == depth - 1))
    return xt.reshape(B, S, D)
```

```python
import functools
import math

import jax
import jax.numpy as jnp
from jax import lax
from jax.experimental import pallas as pl
from jax.experimental.pallas import tpu as pltpu

EPS = 1e-6
N_HEADS = 16
QK_NOPE = 128
QK_ROPE = 64
QK_HEAD = QK_NOPE + QK_ROPE
V_HEAD = 128
Q_LORA = 512
KV_LORA = 512
MLA_WIDTH = N_HEADS * V_HEAD
ROPE_THETA = 10000.0
POOL_WINDOWS = (2, 4, 8, 16)
POOL_GROUPS = len(POOL_WINDOWS)
POOL_GROUP_DIM = 256
POOL_WIDTH = POOL_GROUPS * POOL_GROUP_DIM
POOL_HALO = 16
SMALL_WIDTH = Q_LORA + KV_LORA + 2 * QK_ROPE

V7X_VMEM_LIMIT = 56 * 1024 * 1024
BF16_SUBLANES = 16
MIN_SOFTMAX_MASS = 2.0 ** -60

F32 = jnp.float32
BF16 = jnp.bfloat16
NT_DIMS = (((1,), (1,)), ((), ()))


def _sigmoid(v):
    return 0.5 * jnp.tanh(0.5 * v) + 0.5


def _params(semantics, vmem=V7X_VMEM_LIMIT, flags=None):
    return pltpu.CompilerParams(dimension_semantics=semantics, vmem_limit_bytes=vmem, flags=flags)


def _adaln_kernel(ct_ref, w_ref, b_ref, o_ref):
    w = w_ref[...]
    for b in range(ct_ref.shape[1]):
        cb = ct_ref[:, b:b + 1]
        act = cb * _sigmoid(cb)
        o_ref[b:b + 1, :] = jnp.sum(w * act, axis=0, keepdims=True) + b_ref[...]


def _adaln(c, w, bias):
    B, D = c.shape
    n = w.shape[1]
    tn = 1024
    return pl.pallas_call(
        _adaln_kernel,
        grid=(n // tn,),
        in_specs=[pl.BlockSpec((D, B), lambda j: (0, 0)),
                  pl.BlockSpec((D, tn), lambda j: (0, j)),
                  pl.BlockSpec((1, tn), lambda j: (0, j))],
        out_specs=pl.BlockSpec((B, tn), lambda j: (0, j)),
        out_shape=jax.ShapeDtypeStruct((B, n), F32),
        compiler_params=_params(("arbitrary",)),
        name="adaln",
    )(c.T, w, bias.reshape(1, n))


NORM_ROW_CHUNKS = 4


def _norm_proj_kernel(x_ref, shift_ref, scale_ref, g_ref, ws_ref, h_ref, zs_ref, ws_bf16):
    @pl.when(pl.program_id(0) == 0)
    def _():
        ws_bf16[...] = ws_ref[...].T.astype(BF16)

    rows = x_ref.shape[0] // NORM_ROW_CHUNKS
    for c in range(NORM_ROW_CHUNKS):
        sl = slice(c * rows, (c + 1) * rows)
        x = x_ref[sl, :]
        y = x * lax.rsqrt(jnp.mean(x * x, axis=-1, keepdims=True) + EPS) * g_ref[...]
        h = (y * (1.0 + scale_ref[...]) + shift_ref[...]).astype(BF16)
        h_ref[sl, :] = h
        zs_ref[sl, :] = jnp.dot(h, ws_bf16[...], preferred_element_type=F32)


def _norm_proj(xt, mod4, norm_g, w_in_t, seq):
    T, D = xt.shape
    tm = 512
    per_b = seq // tm
    return pl.pallas_call(
        _norm_proj_kernel,
        grid=(T // tm,),
        in_specs=[pl.BlockSpec((tm, D), lambda i: (i, 0)),
                  pl.BlockSpec((None, None, 1, D), lambda i: (i // per_b, 0, 0, 0)),
                  pl.BlockSpec((None, None, 1, D), lambda i: (i // per_b, 1, 0, 0)),
                  pl.BlockSpec((1, D), lambda i: (0, 0)),
                  pl.BlockSpec((SMALL_WIDTH, D), lambda i: (0, 0), pipeline_mode=pl.Buffered(1))],
        out_specs=[pl.BlockSpec((tm, D), lambda i: (i, 0)),
                   pl.BlockSpec((tm, SMALL_WIDTH), lambda i: (i, 0))],
        out_shape=[jax.ShapeDtypeStruct((T, D), BF16),
                   jax.ShapeDtypeStruct((T, SMALL_WIDTH), F32)],
        scratch_shapes=[pltpu.VMEM((D, SMALL_WIDTH), BF16)],
        compiler_params=_params(("arbitrary",)),
        name="norm_proj",
    )(xt, mod4, mod4, norm_g.reshape(1, D), w_in_t)


_ACTIVATIONS = {
    "silu": lambda a: a * _sigmoid(a),
    "linear": lambda a: a,
    "sigmoid": _sigmoid,
}


def _gate_proj_kernel(h_ref, wa_ref, wb_ref, o_ref, w_bf16, *, tile_kinds, shift):
    j = pl.program_id(0)

    @pl.when(pl.program_id(1) == 0)
    def _():
        w_rows = jnp.concatenate([wa_ref[shift:, :], wb_ref[...]], axis=0)
        w_bf16[...] = w_rows.T.astype(BF16)

    for kind, act in _ACTIVATIONS.items():
        tiles = [t for t, k in enumerate(tile_kinds) if k == kind]
        cond = functools.reduce(jnp.logical_or, [j == t for t in tiles])

        @pl.when(cond)
        def _(act=act):
            acc = jnp.dot(h_ref[...], w_bf16[...], preferred_element_type=F32)
            o_ref[...] = act(acc).astype(o_ref.dtype)


def _gate_proj(h, w_in_t):
    T, D = h.shape
    tm, tn = 1024, 1024
    start = Q_LORA + KV_LORA + QK_ROPE
    n = w_in_t.shape[0] - start
    first_blk, shift = divmod(start, tn)
    assert n % tn == 0 and tn % shift == 0 and shift % BF16_SUBLANES == 0
    tile_kinds = (("silu",) * (MLA_WIDTH // tn) + ("linear",) * (POOL_WIDTH // tn)
                  + ("silu",) * (POOL_WIDTH // tn) + ("sigmoid",) * (2 * D // tn))
    kern = functools.partial(_gate_proj_kernel, tile_kinds=tile_kinds, shift=shift)
    return pl.pallas_call(
        kern,
        grid=(n // tn, T // tm),
        in_specs=[pl.BlockSpec((tm, D), lambda j, i: (i, 0)),
                  pl.BlockSpec((tn, D), lambda j, i: (first_blk + j, 0)),
                  pl.BlockSpec((shift, D), lambda j, i: ((first_blk + j + 1) * (tn // shift), 0))],
        out_specs=pl.BlockSpec((tm, tn), lambda j, i: (i, j)),
        out_shape=jax.ShapeDtypeStruct((T, n), BF16),
        scratch_shapes=[pltpu.VMEM((D, tn), BF16)],
        compiler_params=_params(("arbitrary", "arbitrary")),
        name="gate_proj",
    )(h, w_in_t, w_in_t)


def _mla_prep_kernel(zs_ref, posr_ref, invc_ref, qg_ref, kvg_ref,
                     wq_ref, wk_ref, wv_ref, qt_ref, k_ref, vt_ref, *, q_scale):
    def rms(v, g):
        return (v * lax.rsqrt(jnp.mean(v * v, axis=-1, keepdims=True) + EPS) * g).astype(BF16)

    cqn = rms(zs_ref[:, 0:Q_LORA], qg_ref[...])
    ckvn = rms(zs_ref[:, Q_LORA:Q_LORA + KV_LORA], kvg_ref[...])
    kr = zs_ref[:, Q_LORA + KV_LORA:Q_LORA + KV_LORA + QK_ROPE]
    kr_sw = jnp.concatenate([kr[:, QK_ROPE // 2:], kr[:, :QK_ROPE // 2]], axis=1)

    ang_t = invc_ref[...] * posr_ref[...].astype(F32)
    cos_t, sin_t = jnp.cos(ang_t), jnp.sin(ang_t)
    cos, sin = cos_t.T, sin_t.T

    qf = lax.dot_general(wq_ref[...], cqn, NT_DIMS, preferred_element_type=F32)
    half = QK_ROPE // 2
    for h in range(N_HEADS):
        r0 = h * QK_HEAD + QK_NOPE
        qt_ref[h * QK_HEAD:r0, :] = (qf[h * QK_HEAD:r0] * q_scale).astype(BF16)
        rope = qf[r0:r0 + QK_ROPE]
        rope_sw = jnp.concatenate([rope[half:], rope[:half]], axis=0)
        qt_ref[r0:r0 + QK_ROPE, :] = ((rope * cos_t + rope_sw * sin_t) * q_scale).astype(BF16)

    kn = jnp.dot(ckvn, wk_ref[...], preferred_element_type=F32)
    k_rot = (kr * cos + kr_sw * sin).astype(BF16)
    for h in range(N_HEADS):
        k_ref[h, :, 0:QK_NOPE] = kn[:, h * QK_NOPE:(h + 1) * QK_NOPE].astype(BF16)
        k_ref[h, :, QK_NOPE:QK_HEAD] = k_rot

    vt_ref[...] = lax.dot_general(wv_ref[...], ckvn, NT_DIMS,
                                  preferred_element_type=F32).astype(BF16)


def _mla_prep(zs, posr, inv_signed, q_norm_g, kv_norm_g, wq_t, wk, wv_t, batch, seq, q_scale):
    tm = 512
    per_b = seq // tm
    const = lambda i: (0, 0)
    resident = functools.partial(pl.BlockSpec, pipeline_mode=pl.Buffered(1))
    kern = functools.partial(_mla_prep_kernel, q_scale=q_scale)
    return pl.pallas_call(
        kern,
        grid=(batch * per_b,),
        in_specs=[pl.BlockSpec((tm, SMALL_WIDTH), lambda i: (i, 0)),
                  pl.BlockSpec((None, 1, tm), lambda i: (i // per_b, 0, i % per_b)),
                  pl.BlockSpec((QK_ROPE, 1), const),
                  pl.BlockSpec((1, Q_LORA), const),
                  pl.BlockSpec((1, KV_LORA), const),
                  resident(wq_t.shape, const),
                  resident(wk.shape, const),
                  resident(wv_t.shape, const)],
        out_specs=[pl.BlockSpec((None, N_HEADS * QK_HEAD, tm), lambda i: (i // per_b, 0, i % per_b)),
                   pl.BlockSpec((None, N_HEADS, tm, QK_HEAD), lambda i: (i // per_b, 0, i % per_b, 0)),
                   pl.BlockSpec((None, MLA_WIDTH, tm), lambda i: (i // per_b, 0, i % per_b))],
        out_shape=[jax.ShapeDtypeStruct((batch, N_HEADS * QK_HEAD, seq), BF16),
                   jax.ShapeDtypeStruct((batch, N_HEADS, seq, QK_HEAD), BF16),
                   jax.ShapeDtypeStruct((batch, MLA_WIDTH, seq), BF16)],
        compiler_params=_params(("arbitrary",)),
        name="mla_prep",
    )(zs, posr, inv_signed.reshape(QK_ROPE, 1),
      q_norm_g.reshape(1, Q_LORA), kv_norm_g.reshape(1, KV_LORA), wq_t, wk, wv_t)


def _attention_kernel(qt_ref, k_ref, vt_ref, *refs, n_cast):
    cast_in, (o_ref, *cast_out) = refs[:n_cast], refs[n_cast:2 * n_cast + 1]
    s_a, m_a, s_b, m_b, lmin_ref = refs[2 * n_cast + 1:]
    for w_in, w_out in zip(cast_in, cast_out):
        w_out[...] = w_in[...].astype(w_out.dtype)

    tq = lmin_ref.shape[1]
    nq = qt_ref.shape[1] // tq

    kf = k_ref[...].astype(F32)
    k_norm = jnp.sqrt(jnp.max(jnp.sum(kf * kf, axis=1, keepdims=True), axis=0, keepdims=True))
    lmin_ref[...] = jnp.full(lmin_ref.shape, jnp.inf, F32)

    def fast_tile(i, carry):
        off = pl.multiple_of(i * tq, tq)
        qt = qt_ref[:, pl.ds(off, tq)]
        qf = qt.astype(F32)
        shift = jnp.sqrt(jnp.sum(qf * qf, axis=0, keepdims=True)) * k_norm
        s = jnp.dot(k_ref[...], qt, preferred_element_type=F32)
        p = jnp.exp2(s - shift)
        l = jnp.sum(p, axis=0, keepdims=True)
        ot = jnp.dot(vt_ref[...], p.astype(BF16), preferred_element_type=F32)
        o_ref[pl.ds(off, tq), :] = (ot / l).T.astype(o_ref.dtype)
        lmin_ref[...] = jnp.minimum(lmin_ref[...], l)
        return carry

    lax.fori_loop(0, nq, fast_tile, 0, unroll=4)
    trusted = jnp.min(lmin_ref[...]) >= MIN_SOFTMAX_MASS

    @pl.when(jnp.logical_not(trusted))
    def _():
        _attention_exact(qt_ref, k_ref, vt_ref, o_ref, s_a, m_a, s_b, m_b)


def _attention_exact(qt_ref, k_ref, vt_ref, o_ref, s_a, m_a, s_b, m_b):
    tq = s_a.shape[1]
    nq = qt_ref.shape[1] // tq

    def scores(i, s_ref, m_ref):
        off = pl.multiple_of(i * tq, tq)
        s = jnp.dot(k_ref[...], qt_ref[:, pl.ds(off, tq)], preferred_element_type=F32)
        s_ref[...] = s
        m_ref[...] = jnp.max(s, axis=0, keepdims=True)

    def finish(i, s_ref, m_ref):
        off = pl.multiple_of(i * tq, tq)
        p = jnp.exp2(s_ref[...] - m_ref[...])
        l = jnp.sum(p, axis=0, keepdims=True)
        ot = jnp.dot(vt_ref[...], p.astype(BF16), preferred_element_type=F32)
        o_ref[pl.ds(off, tq), :] = (ot / l).T.astype(o_ref.dtype)

    scores(0, s_a, m_a)

    def step(i, carry):
        @pl.when(i % 2 == 1)
        def _():
            scores(i, s_b, m_b)
            finish(i - 1, s_a, m_a)

        @pl.when(i % 2 == 0)
        def _():
            scores(i, s_a, m_a)
            finish(i - 1, s_b, m_b)

        return carry

    lax.fori_loop(1, nq, step, 0)
    finish(nq - 1, s_b, m_b)


def _attention(qt, k, vt, batch, seq, later_weights):
    tq = 256
    tq_fast = 512
    assert seq % (2 * tq) == 0 and seq % tq_fast == 0
    steps = batch * N_HEADS
    slabs = [w.shape[0] // steps for w in later_weights]
    assert all(s % BF16_SUBLANES == 0 and s * steps == w.shape[0]
               for s, w in zip(slabs, later_weights))
    step = lambda b, h: (b * N_HEADS + h, 0)
    cast_specs = [pl.BlockSpec((s, w.shape[1]), step) for s, w in zip(slabs, later_weights)]
    outs = pl.pallas_call(
        functools.partial(_attention_kernel, n_cast=len(later_weights)),
        grid=(batch, N_HEADS),
        in_specs=[pl.BlockSpec((None, QK_HEAD, seq), lambda b, h: (b, h, 0)),
                  pl.BlockSpec((None, None, seq, QK_HEAD), lambda b, h: (b, h, 0, 0)),
                  pl.BlockSpec((None, V_HEAD, seq), lambda b, h: (b, h, 0))] + cast_specs,
        out_specs=[pl.BlockSpec((None, seq, V_HEAD), lambda b, h: (b, 0, h))] + cast_specs,
        out_shape=[jax.ShapeDtypeStruct((batch, seq, MLA_WIDTH), BF16)]
                  + [jax.ShapeDtypeStruct(w.shape, BF16) for w in later_weights],
        scratch_shapes=[pltpu.VMEM((seq, tq), F32), pltpu.VMEM((1, tq), F32),
                        pltpu.VMEM((seq, tq), F32), pltpu.VMEM((1, tq), F32),
                        pltpu.VMEM((1, tq_fast), F32)],
        compiler_params=_params(("arbitrary", "arbitrary")),
        name="attention",
    )(qt, k, vt, *later_weights)
    return outs[0], outs[1:]


def _merge_out_kernel(x_ref, gate_ref, attn_ref, gm_ref, gp_ref, vp_ref, vprev_ref, vnext_ref,
                      mm_ref, mp_ref, pw_ref, ps_ref, wop_ref, wom_ref, wout_ref, fg_ref, o_ref,
                      *, seq, final_norm):
    tm = x_ref.shape[0]
    t0 = (pl.program_id(0) % (seq // tm)) * tm

    gated = attn_ref[...] * gm_ref[...]

    cur = vp_ref[...].astype(F32)
    prev = jnp.where(t0 > 0, vprev_ref[...].astype(F32), 0.0)
    nxt = jnp.where(t0 + tm < seq, vnext_ref[...].astype(F32), 0.0)
    ext = jnp.concatenate([prev, cur, nxt], axis=0)
    n_ext = tm + 2 * POOL_HALO
    tok = t0 + lax.broadcasted_iota(jnp.int32, (tm, 1), 0)
    mixed = []
    p_mla = []
    mla_cols = wom_ref.shape[1] // POOL_GROUPS
    for g, w in enumerate(POOL_WINDOWS):
        p_mla.append(jnp.dot(gated, wom_ref[:, g * mla_cols:(g + 1) * mla_cols],
                             preferred_element_type=F32))
        acc = ext[:, g * POOL_GROUP_DIM:(g + 1) * POOL_GROUP_DIM]
        acc = acc + pltpu.roll(acc, 1, axis=0)
        half = 1
        while 2 * half < w:
            acc = pltpu.roll(acc, half, axis=0) + pltpu.roll(acc, n_ext - half, axis=0)
            half *= 2
        wsum = acc[POOL_HALO:POOL_HALO + tm]
        count = (jnp.minimum(tok + w // 2, seq) - jnp.maximum(tok - w // 2, 0)).astype(F32)
        pooled = wsum / count - cur[:, g * POOL_GROUP_DIM:(g + 1) * POOL_GROUP_DIM]
        mixed.append(jnp.dot(pooled.astype(BF16), pw_ref[g], preferred_element_type=F32))
    mixed = jnp.concatenate(mixed, axis=1)
    p_mla = jnp.concatenate(p_mla, axis=1)
    u = (mixed * ps_ref[...] * gp_ref[...].astype(F32)).astype(BF16)
    p_pool = jnp.dot(u, wop_ref[...], preferred_element_type=F32)

    y = mm_ref[...].astype(F32) * p_mla + mp_ref[...].astype(F32) * p_pool
    r = jnp.dot(y.astype(BF16), wout_ref[...], preferred_element_type=F32)
    xo = x_ref[...] + gate_ref[...] * r
    if final_norm:
        xo = xo * lax.rsqrt(jnp.mean(xo * xo, axis=-1, keepdims=True) + EPS) * fg_ref[...]
    o_ref[...] = xo


def _merge_out(xt, mod4, attn, zbig, pool_w, pool_scale, w_o_pool, w_o_mla, w_out, final_g,
               seq, final_norm):
    T, D = xt.shape
    tm = 256
    per_b = seq // tm
    halo_per_tile = tm // POOL_HALO
    n_halo = T // POOL_HALO
    vp_blk = MLA_WIDTH // POOL_WIDTH
    gp_blk = vp_blk + 1
    mm_blk = (MLA_WIDTH + 2 * POOL_WIDTH) // D
    resident = functools.partial(pl.BlockSpec, pipeline_mode=pl.Buffered(1))
    kern = functools.partial(_merge_out_kernel, seq=seq, final_norm=final_norm)
    return pl.pallas_call(
        kern,
        grid=(T // tm,),
        in_specs=[pl.BlockSpec((tm, D), lambda i: (i, 0)),
                  pl.BlockSpec((None, None, 1, D), lambda i: (i // per_b, 2, 0, 0)),
                  pl.BlockSpec((tm, MLA_WIDTH), lambda i: (i, 0)),
                  pl.BlockSpec((tm, MLA_WIDTH), lambda i: (i, 0)),
                  pl.BlockSpec((tm, POOL_WIDTH), lambda i: (i, gp_blk)),
                  pl.BlockSpec((tm, POOL_WIDTH), lambda i: (i, vp_blk)),
                  pl.BlockSpec((POOL_HALO, POOL_WIDTH),
                               lambda i: (jnp.maximum(i * halo_per_tile - 1, 0), vp_blk)),
                  pl.BlockSpec((POOL_HALO, POOL_WIDTH),
                               lambda i: (jnp.minimum((i + 1) * halo_per_tile, n_halo - 1), vp_blk)),
                  pl.BlockSpec((tm, D), lambda i: (i, mm_blk)),
                  pl.BlockSpec((tm, D), lambda i: (i, mm_blk + 1)),
                  resident(pool_w.shape, lambda i: (0, 0, 0)),
                  resident((1, POOL_WIDTH), lambda i: (0, 0)),
                  resident(w_o_pool.shape, lambda i: (0, 0)),
                  resident(w_o_mla.shape, lambda i: (0, 0)),
                  resident(w_out.shape, lambda i: (0, 0)),
                  resident((1, D), lambda i: (0, 0))],
        out_specs=pl.BlockSpec((tm, D), lambda i: (i, 0)),
        out_shape=jax.ShapeDtypeStruct((T, D), F32),
        compiler_params=_params(("arbitrary",)),
        name="merge_out",
    )(xt, mod4, attn, zbig, zbig, zbig, zbig, zbig, zbig, zbig,
      pool_w, pool_scale.reshape(1, POOL_WIDTH), w_o_pool, w_o_mla, w_out, final_g.reshape(1, D))


def _split_up_proj(w_uq, w_ukv):
    wq_t = w_uq.T
    wkv = w_ukv.reshape(KV_LORA, N_HEADS, QK_NOPE + V_HEAD)
    wk = wkv[:, :, :QK_NOPE].reshape(KV_LORA, N_HEADS * QK_NOPE)
    wv_t = wkv[:, :, QK_NOPE:].reshape(KV_LORA, MLA_WIDTH).T
    return wq_t.astype(BF16), wk.astype(BF16), wv_t.astype(BF16)


def kernel(x, c, positions, ada_w, ada_b, norm_g, w_in, q_norm_g, w_uq, kv_norm_g, w_ukv, w_o_mla,
           pool_w, pool_scale, w_o_pool, w_out, final_g):
    B, S, D = x.shape
    depth = ada_w.shape[0]
    inv_freq = 1.0 / (ROPE_THETA ** (jnp.arange(0, QK_ROPE, 2, dtype=F32) / QK_ROPE))
    inv_signed = jnp.concatenate([-inv_freq, inv_freq])
    posr = positions.reshape(B, 1, S)
    q_scale = QK_HEAD ** -0.5 * math.log2(math.e)

    xt = x.reshape(B * S, D)
    for l in range(depth):
        mod4 = _adaln(c, ada_w[l], ada_b[l]).reshape(B, 3, 1, D)
        wq_t, wk, wv_t = _split_up_proj(w_uq[l], w_ukv[l])

        w_in_t = w_in[l].T
        h, zs = _norm_proj(xt, mod4, norm_g[l], w_in_t, S)
        zbig = _gate_proj(h, w_in_t)
        qt, k, vt = _mla_prep(zs, posr, inv_signed, q_norm_g[l], kv_norm_g[l],
                              wq_t, wk, wv_t, B, S, q_scale)
        attn, (pw, wop, wom, wout) = _attention(
            qt, k, vt, B, S,
            (pool_w[l].reshape(POOL_WIDTH, POOL_GROUP_DIM), w_o_pool[l], w_o_mla[l], w_out[l]))
        xt = _merge_out(xt, mod4, attn.reshape(B * S, MLA_WIDTH), zbig,
                        pw.reshape(POOL_GROUPS, POOL_GROUP_DIM, POOL_GROUP_DIM), pool_scale[l],
                        wop, wom, wout, final_g, S, final_norm=(l == depth - 1))
    return xt.reshape(B, S, D)
```

```python
import functools
import math

import jax
import jax.numpy as jnp
from jax import lax
from jax.experimental import pallas as pl
from jax.experimental.pallas import tpu as pltpu

EPS = 1e-6
N_HEADS = 16
QK_NOPE = 128
QK_ROPE = 64
QK_HEAD = QK_NOPE + QK_ROPE
V_HEAD = 128
Q_LORA = 512
KV_LORA = 512
MLA_WIDTH = N_HEADS * V_HEAD
ROPE_THETA = 10000.0
POOL_WINDOWS = (2, 4, 8, 16)
POOL_GROUPS = len(POOL_WINDOWS)
POOL_GROUP_DIM = 256
POOL_WIDTH = POOL_GROUPS * POOL_GROUP_DIM
POOL_HALO = 16
SMALL_WIDTH = Q_LORA + KV_LORA + 2 * QK_ROPE

V7X_VMEM_LIMIT = 56 * 1024 * 1024
BF16_SUBLANES = 16
MIN_SOFTMAX_MASS = 2.0 ** -60

F32 = jnp.float32
BF16 = jnp.bfloat16
NT_DIMS = (((1,), (1,)), ((), ()))


def _sigmoid(v):
    return 0.5 * jnp.tanh(0.5 * v) + 0.5


def _params(semantics, vmem=V7X_VMEM_LIMIT, flags=None):
    return pltpu.CompilerParams(dimension_semantics=semantics, vmem_limit_bytes=vmem, flags=flags)


def _adaln_kernel(ct_ref, w_ref, b_ref, o_ref):
    w = w_ref[...]
    for b in range(ct_ref.shape[1]):
        cb = ct_ref[:, b:b + 1]
        act = cb * _sigmoid(cb)
        o_ref[b:b + 1, :] = jnp.sum(w * act, axis=0, keepdims=True) + b_ref[...]


def _adaln(c, w, bias):
    B, D = c.shape
    n = w.shape[1]
    tn = 1024
    return pl.pallas_call(
        _adaln_kernel,
        grid=(n // tn,),
        in_specs=[pl.BlockSpec((D, B), lambda j: (0, 0)),
                  pl.BlockSpec((D, tn), lambda j: (0, j)),
                  pl.BlockSpec((1, tn), lambda j: (0, j))],
        out_specs=pl.BlockSpec((B, tn), lambda j: (0, j)),
        out_shape=jax.ShapeDtypeStruct((B, n), F32),
        compiler_params=_params(("arbitrary",)),
        name="adaln",
    )(c.T, w, bias.reshape(1, n))


NORM_ROW_CHUNKS = 4


def _norm_proj_kernel(x_ref, shift_ref, scale_ref, g_ref, ws_ref, h_ref, zs_ref, ws_bf16):
    @pl.when(pl.program_id(0) == 0)
    def _():
        ws_bf16[...] = ws_ref[...].T.astype(BF16)

    rows = x_ref.shape[0] // NORM_ROW_CHUNKS
    for c in range(NORM_ROW_CHUNKS):
        sl = slice(c * rows, (c + 1) * rows)
        x = x_ref[sl, :]
        y = x * lax.rsqrt(jnp.mean(x * x, axis=-1, keepdims=True) + EPS) * g_ref[...]
        h = (y * (1.0 + scale_ref[...]) + shift_ref[...]).astype(BF16)
        h_ref[sl, :] = h
        zs_ref[sl, :] = jnp.dot(h, ws_bf16[...], preferred_element_type=F32)


def _norm_proj(xt, mod4, norm_g, w_in_t, seq):
    T, D = xt.shape
    tm = 512
    per_b = seq // tm
    return pl.pallas_call(
        _norm_proj_kernel,
        grid=(T // tm,),
        in_specs=[pl.BlockSpec((tm, D), lambda i: (i, 0)),
                  pl.BlockSpec((None, None, 1, D), lambda i: (i // per_b, 0, 0, 0)),
                  pl.BlockSpec((None, None, 1, D), lambda i: (i // per_b, 1, 0, 0)),
                  pl.BlockSpec((1, D), lambda i: (0, 0)),
                  pl.BlockSpec((SMALL_WIDTH, D), lambda i: (0, 0), pipeline_mode=pl.Buffered(1))],
        out_specs=[pl.BlockSpec((tm, D), lambda i: (i, 0)),
                   pl.BlockSpec((tm, SMALL_WIDTH), lambda i: (i, 0))],
        out_shape=[jax.ShapeDtypeStruct((T, D), BF16),
                   jax.ShapeDtypeStruct((T, SMALL_WIDTH), F32)],
        scratch_shapes=[pltpu.VMEM((D, SMALL_WIDTH), BF16)],
        compiler_params=_params(("arbitrary",)),
        name="norm_proj",
    )(xt, mod4, mod4, norm_g.reshape(1, D), w_in_t)


_ACTIVATIONS = {
    "silu": lambda a: a * _sigmoid(a),
    "linear": lambda a: a,
    "sigmoid": _sigmoid,
}


def _gate_proj_kernel(h_ref, wa_ref, wb_ref, o_ref, w_bf16, *, tile_kinds, shift):
    j = pl.program_id(0)
    tn = w_bf16.shape[0]

    @pl.when(pl.program_id(1) == 0)
    def _():
        w_bf16[0:tn - shift, :] = wa_ref[shift:, :].astype(BF16)
        w_bf16[tn - shift:, :] = wb_ref[...].astype(BF16)

    for kind, act in _ACTIVATIONS.items():
        tiles = [t for t, k in enumerate(tile_kinds) if k == kind]
        cond = functools.reduce(jnp.logical_or, [j == t for t in tiles])

        @pl.when(cond)
        def _(act=act):
            acc = lax.dot_general(h_ref[...], w_bf16[...], NT_DIMS, preferred_element_type=F32)
            o_ref[...] = act(acc).astype(o_ref.dtype)


def _gate_proj(h, w_in_t):
    T, D = h.shape
    tm, tn = 1024, 1024
    start = Q_LORA + KV_LORA + QK_ROPE
    n = w_in_t.shape[0] - start
    first_blk, shift = divmod(start, tn)
    assert n % tn == 0 and tn % shift == 0 and shift % BF16_SUBLANES == 0
    tile_kinds = (("silu",) * (MLA_WIDTH // tn) + ("linear",) * (POOL_WIDTH // tn)
                  + ("silu",) * (POOL_WIDTH // tn) + ("sigmoid",) * (2 * D // tn))
    kern = functools.partial(_gate_proj_kernel, tile_kinds=tile_kinds, shift=shift)
    return pl.pallas_call(
        kern,
        grid=(n // tn, T // tm),
        in_specs=[pl.BlockSpec((tm, D), lambda j, i: (i, 0)),
                  pl.BlockSpec((tn, D), lambda j, i: (first_blk + j, 0)),
                  pl.BlockSpec((shift, D), lambda j, i: ((first_blk + j + 1) * (tn // shift), 0))],
        out_specs=pl.BlockSpec((tm, tn), lambda j, i: (i, j)),
        out_shape=jax.ShapeDtypeStruct((T, n), BF16),
        scratch_shapes=[pltpu.VMEM((tn, D), BF16)],
        compiler_params=_params(("arbitrary", "arbitrary")),
        name="gate_proj",
    )(h, w_in_t, w_in_t)


def _mla_prep_kernel(zs_ref, posr_ref, invc_ref, qg_ref, kvg_ref, wuq_ref, wukv_ref,
                     qt_ref, k_ref, vt_ref, wq_ref, wk_ref, wv_ref, *, q_scale):
    @pl.when(pl.program_id(0) == 0)
    def _():
        wq_ref[...] = wuq_ref[...].T.astype(BF16)
        kv_head = QK_NOPE + V_HEAD
        for h in range(N_HEADS):
            wk_ref[:, h * QK_NOPE:(h + 1) * QK_NOPE] = (
                wukv_ref[:, h * kv_head:h * kv_head + QK_NOPE].astype(BF16))
            wv_ref[h * V_HEAD:(h + 1) * V_HEAD, :] = (
                wukv_ref[:, h * kv_head + QK_NOPE:(h + 1) * kv_head].T.astype(BF16))

    def rms(v, g):
        return (v * lax.rsqrt(jnp.mean(v * v, axis=-1, keepdims=True) + EPS) * g).astype(BF16)

    cqn = rms(zs_ref[:, 0:Q_LORA], qg_ref[...])
    ckvn = rms(zs_ref[:, Q_LORA:Q_LORA + KV_LORA], kvg_ref[...])
    kr = zs_ref[:, Q_LORA + KV_LORA:Q_LORA + KV_LORA + QK_ROPE]
    kr_sw = jnp.concatenate([kr[:, QK_ROPE // 2:], kr[:, :QK_ROPE // 2]], axis=1)

    ang_t = invc_ref[...] * posr_ref[...].astype(F32)
    cos_t, sin_t = jnp.cos(ang_t), jnp.sin(ang_t)
    cos, sin = cos_t.T, sin_t.T

    qf = lax.dot_general(wq_ref[...], cqn, NT_DIMS, preferred_element_type=F32)
    half = QK_ROPE // 2
    for h in range(N_HEADS):
        r0 = h * QK_HEAD + QK_NOPE
        qt_ref[h * QK_HEAD:r0, :] = (qf[h * QK_HEAD:r0] * q_scale).astype(BF16)
        rope = qf[r0:r0 + QK_ROPE]
        rope_sw = jnp.concatenate([rope[half:], rope[:half]], axis=0)
        qt_ref[r0:r0 + QK_ROPE, :] = ((rope * cos_t + rope_sw * sin_t) * q_scale).astype(BF16)

    kn = jnp.dot(ckvn, wk_ref[...], preferred_element_type=F32)
    k_rot = (kr * cos + kr_sw * sin).astype(BF16)
    for h in range(N_HEADS):
        k_ref[h, :, 0:QK_NOPE] = kn[:, h * QK_NOPE:(h + 1) * QK_NOPE].astype(BF16)
        k_ref[h, :, QK_NOPE:QK_HEAD] = k_rot

    vt_ref[...] = lax.dot_general(wv_ref[...], ckvn, NT_DIMS,
                                  preferred_element_type=F32).astype(BF16)


def _mla_prep(zs, posr, inv_signed, q_norm_g, kv_norm_g, w_uq, w_ukv, batch, seq, q_scale):
    tm = 512
    per_b = seq // tm
    const = lambda i: (0, 0)
    resident = functools.partial(pl.BlockSpec, pipeline_mode=pl.Buffered(1))
    kern = functools.partial(_mla_prep_kernel, q_scale=q_scale)
    return pl.pallas_call(
        kern,
        grid=(batch * per_b,),
        in_specs=[pl.BlockSpec((tm, SMALL_WIDTH), lambda i: (i, 0)),
                  pl.BlockSpec((None, 1, tm), lambda i: (i // per_b, 0, i % per_b)),
                  pl.BlockSpec((QK_ROPE, 1), const),
                  pl.BlockSpec((1, Q_LORA), const),
                  pl.BlockSpec((1, KV_LORA), const),
                  resident(w_uq.shape, const),
                  resident(w_ukv.shape, const)],
        out_specs=[pl.BlockSpec((None, N_HEADS * QK_HEAD, tm), lambda i: (i // per_b, 0, i % per_b)),
                   pl.BlockSpec((None, N_HEADS, tm, QK_HEAD), lambda i: (i // per_b, 0, i % per_b, 0)),
                   pl.BlockSpec((None, MLA_WIDTH, tm), lambda i: (i // per_b, 0, i % per_b))],
        out_shape=[jax.ShapeDtypeStruct((batch, N_HEADS * QK_HEAD, seq), BF16),
                   jax.ShapeDtypeStruct((batch, N_HEADS, seq, QK_HEAD), BF16),
                   jax.ShapeDtypeStruct((batch, MLA_WIDTH, seq), BF16)],
        scratch_shapes=[pltpu.VMEM((N_HEADS * QK_HEAD, Q_LORA), BF16),
                        pltpu.VMEM((KV_LORA, N_HEADS * QK_NOPE), BF16),
                        pltpu.VMEM((MLA_WIDTH, KV_LORA), BF16)],
        compiler_params=_params(("arbitrary",)),
        name="mla_prep",
    )(zs, posr, inv_signed.reshape(QK_ROPE, 1),
      q_norm_g.reshape(1, Q_LORA), kv_norm_g.reshape(1, KV_LORA), w_uq, w_ukv)


def _attention_kernel(qt_ref, k_ref, vt_ref, *refs, n_cast):
    cast_in, (o_ref, *cast_out) = refs[:n_cast], refs[n_cast:2 * n_cast + 1]
    s_a, m_a, s_b, m_b, lmin_ref = refs[2 * n_cast + 1:]
    for w_in, w_out in zip(cast_in, cast_out):
        w_out[...] = w_in[...].astype(w_out.dtype)

    tq = lmin_ref.shape[1]
    nq = qt_ref.shape[1] // tq

    kf = k_ref[...].astype(F32)
    k_norm = jnp.sqrt(jnp.max(jnp.sum(kf * kf, axis=1, keepdims=True), axis=0, keepdims=True))
    lmin_ref[...] = jnp.full(lmin_ref.shape, jnp.inf, F32)

    def fast_tile(i, carry):
        off = pl.multiple_of(i * tq, tq)
        qt = qt_ref[:, pl.ds(off, tq)]
        qf = qt.astype(F32)
        shift = jnp.sqrt(jnp.sum(qf * qf, axis=0, keepdims=True)) * k_norm
        s = jnp.dot(k_ref[...], qt, preferred_element_type=F32)
        p = jnp.exp2(s - shift)
        l = jnp.sum(p, axis=0, keepdims=True)
        ot = jnp.dot(vt_ref[...], p.astype(BF16), preferred_element_type=F32)
        o_ref[pl.ds(off, tq), :] = (ot / l).T.astype(o_ref.dtype)
        lmin_ref[...] = jnp.minimum(lmin_ref[...], l)
        return carry

    lax.fori_loop(0, nq, fast_tile, 0, unroll=4)
    trusted = jnp.min(lmin_ref[...]) >= MIN_SOFTMAX_MASS

    @pl.when(jnp.logical_not(trusted))
    def _():
        _attention_exact(qt_ref, k_ref, vt_ref, o_ref, s_a, m_a, s_b, m_b)


def _attention_exact(qt_ref, k_ref, vt_ref, o_ref, s_a, m_a, s_b, m_b):
    tq = s_a.shape[1]
    nq = qt_ref.shape[1] // tq

    def scores(i, s_ref, m_ref):
        off = pl.multiple_of(i * tq, tq)
        s = jnp.dot(k_ref[...], qt_ref[:, pl.ds(off, tq)], preferred_element_type=F32)
        s_ref[...] = s
        m_ref[...] = jnp.max(s, axis=0, keepdims=True)

    def finish(i, s_ref, m_ref):
        off = pl.multiple_of(i * tq, tq)
        p = jnp.exp2(s_ref[...] - m_ref[...])
        l = jnp.sum(p, axis=0, keepdims=True)
        ot = jnp.dot(vt_ref[...], p.astype(BF16), preferred_element_type=F32)
        o_ref[pl.ds(off, tq), :] = (ot / l).T.astype(o_ref.dtype)

    scores(0, s_a, m_a)

    def step(i, carry):
        @pl.when(i % 2 == 1)
        def _():
            scores(i, s_b, m_b)
            finish(i - 1, s_a, m_a)

        @pl.when(i % 2 == 0)
        def _():
            scores(i, s_a, m_a)
            finish(i - 1, s_b, m_b)

        return carry

    lax.fori_loop(1, nq, step, 0)
    finish(nq - 1, s_b, m_b)


def _attention(qt, k, vt, batch, seq, later_weights):
    tq = 256
    tq_fast = 512
    assert seq % (2 * tq) == 0 and seq % tq_fast == 0
    steps = batch * N_HEADS
    slabs = [w.shape[0] // steps for w in later_weights]
    assert all(s % BF16_SUBLANES == 0 and s * steps == w.shape[0]
               for s, w in zip(slabs, later_weights))
    step = lambda b, h: (b * N_HEADS + h, 0)
    cast_specs = [pl.BlockSpec((s, w.shape[1]), step) for s, w in zip(slabs, later_weights)]
    outs = pl.pallas_call(
        functools.partial(_attention_kernel, n_cast=len(later_weights)),
        grid=(batch, N_HEADS),
        in_specs=[pl.BlockSpec((None, QK_HEAD, seq), lambda b, h: (b, h, 0)),
                  pl.BlockSpec((None, None, seq, QK_HEAD), lambda b, h: (b, h, 0, 0)),
                  pl.BlockSpec((None, V_HEAD, seq), lambda b, h: (b, h, 0))] + cast_specs,
        out_specs=[pl.BlockSpec((None, seq, V_HEAD), lambda b, h: (b, 0, h))] + cast_specs,
        out_shape=[jax.ShapeDtypeStruct((batch, seq, MLA_WIDTH), BF16)]
                  + [jax.ShapeDtypeStruct(w.shape, BF16) for w in later_weights],
        scratch_shapes=[pltpu.VMEM((seq, tq), F32), pltpu.VMEM((1, tq), F32),
                        pltpu.VMEM((seq, tq), F32), pltpu.VMEM((1, tq), F32),
                        pltpu.VMEM((1, tq_fast), F32)],
        compiler_params=_params(("arbitrary", "arbitrary")),
        name="attention",
    )(qt, k, vt, *later_weights)
    return outs[0], outs[1:]


def _merge_out_kernel(x_ref, gate_ref, attn_ref, gm_ref, gp_ref, vp_ref, vprev_ref, vnext_ref,
                      mm_ref, mp_ref, pw_ref, ps_ref, wop_ref, wom_ref, wout_ref, fg_ref, o_ref,
                      *, seq, final_norm):
    tm = x_ref.shape[0]
    t0 = (pl.program_id(0) % (seq // tm)) * tm

    gated = attn_ref[...] * gm_ref[...]

    cur = vp_ref[...].astype(F32)
    prev = jnp.where(t0 > 0, vprev_ref[...].astype(F32), 0.0)
    nxt = jnp.where(t0 + tm < seq, vnext_ref[...].astype(F32), 0.0)
    ext = jnp.concatenate([prev, cur, nxt], axis=0)
    n_ext = tm + 2 * POOL_HALO
    tok = t0 + lax.broadcasted_iota(jnp.int32, (tm, 1), 0)
    mixed = []
    p_mla = []
    mla_cols = wom_ref.shape[1] // POOL_GROUPS
    for g, w in enumerate(POOL_WINDOWS):
        p_mla.append(jnp.dot(gated, wom_ref[:, g * mla_cols:(g + 1) * mla_cols],
                             preferred_element_type=F32))
        acc = ext[:, g * POOL_GROUP_DIM:(g + 1) * POOL_GROUP_DIM]
        acc = acc + pltpu.roll(acc, 1, axis=0)
        half = 1
        while 2 * half < w:
            acc = pltpu.roll(acc, half, axis=0) + pltpu.roll(acc, n_ext - half, axis=0)
            half *= 2
        wsum = acc[POOL_HALO:POOL_HALO + tm]
        count = (jnp.minimum(tok + w // 2, seq) - jnp.maximum(tok - w // 2, 0)).astype(F32)
        pooled = wsum / count - cur[:, g * POOL_GROUP_DIM:(g + 1) * POOL_GROUP_DIM]
        mixed.append(jnp.dot(pooled.astype(BF16), pw_ref[g], preferred_element_type=F32))
    mixed = jnp.concatenate(mixed, axis=1)
    p_mla = jnp.concatenate(p_mla, axis=1)
    u = (mixed * ps_ref[...] * gp_ref[...].astype(F32)).astype(BF16)
    p_pool = jnp.dot(u, wop_ref[...], preferred_element_type=F32)

    y = mm_ref[...].astype(F32) * p_mla + mp_ref[...].astype(F32) * p_pool
    r = jnp.dot(y.astype(BF16), wout_ref[...], preferred_element_type=F32)
    xo = x_ref[...] + gate_ref[...] * r
    if final_norm:
        xo = xo * lax.rsqrt(jnp.mean(xo * xo, axis=-1, keepdims=True) + EPS) * fg_ref[...]
    o_ref[...] = xo


def _merge_out(xt, mod4, attn, zbig, pool_w, pool_scale, w_o_pool, w_o_mla, w_out, final_g,
               seq, final_norm):
    T, D = xt.shape
    tm = 256
    per_b = seq // tm
    halo_per_tile = tm // POOL_HALO
    n_halo = T // POOL_HALO
    vp_blk = MLA_WIDTH // POOL_WIDTH
    gp_blk = vp_blk + 1
    mm_blk = (MLA_WIDTH + 2 * POOL_WIDTH) // D
    resident = functools.partial(pl.BlockSpec, pipeline_mode=pl.Buffered(1))
    kern = functools.partial(_merge_out_kernel, seq=seq, final_norm=final_norm)
    return pl.pallas_call(
        kern,
        grid=(T // tm,),
        in_specs=[pl.BlockSpec((tm, D), lambda i: (i, 0)),
                  pl.BlockSpec((None, None, 1, D), lambda i: (i // per_b, 2, 0, 0)),
                  pl.BlockSpec((tm, MLA_WIDTH), lambda i: (i, 0)),
                  pl.BlockSpec((tm, MLA_WIDTH), lambda i: (i, 0)),
                  pl.BlockSpec((tm, POOL_WIDTH), lambda i: (i, gp_blk)),
                  pl.BlockSpec((tm, POOL_WIDTH), lambda i: (i, vp_blk)),
                  pl.BlockSpec((POOL_HALO, POOL_WIDTH),
                               lambda i: (jnp.maximum(i * halo_per_tile - 1, 0), vp_blk)),
                  pl.BlockSpec((POOL_HALO, POOL_WIDTH),
                               lambda i: (jnp.minimum((i + 1) * halo_per_tile, n_halo - 1), vp_blk)),
                  pl.BlockSpec((tm, D), lambda i: (i, mm_blk)),
                  pl.BlockSpec((tm, D), lambda i: (i, mm_blk + 1)),
                  resident(pool_w.shape, lambda i: (0, 0, 0)),
                  resident((1, POOL_WIDTH), lambda i: (0, 0)),
                  resident(w_o_pool.shape, lambda i: (0, 0)),
                  resident(w_o_mla.shape, lambda i: (0, 0)),
                  resident(w_out.shape, lambda i: (0, 0)),
                  resident((1, D), lambda i: (0, 0))],
        out_specs=pl.BlockSpec((tm, D), lambda i: (i, 0)),
        out_shape=jax.ShapeDtypeStruct((T, D), F32),
        compiler_params=_params(("arbitrary",)),
        name="merge_out",
    )(xt, mod4, attn, zbig, zbig, zbig, zbig, zbig, zbig, zbig,
      pool_w, pool_scale.reshape(1, POOL_WIDTH), w_o_pool, w_o_mla, w_out, final_g.reshape(1, D))


def kernel(x, c, positions, ada_w, ada_b, norm_g, w_in, q_norm_g, w_uq, kv_norm_g, w_ukv, w_o_mla,
           pool_w, pool_scale, w_o_pool, w_out, final_g):
    B, S, D = x.shape
    depth = ada_w.shape[0]
    inv_freq = 1.0 / (ROPE_THETA ** (jnp.arange(0, QK_ROPE, 2, dtype=F32) / QK_ROPE))
    inv_signed = jnp.concatenate([-inv_freq, inv_freq])
    posr = positions.reshape(B, 1, S)
    q_scale = QK_HEAD ** -0.5 * math.log2(math.e)

    xt = x.reshape(B * S, D)
    for l in range(depth):
        mod4 = _adaln(c, ada_w[l], ada_b[l]).reshape(B, 3, 1, D)
        w_in_t = w_in[l].T
        h, zs = _norm_proj(xt, mod4, norm_g[l], w_in_t, S)
        zbig = _gate_proj(h, w_in_t)
        qt, k, vt = _mla_prep(zs, posr, inv_signed, q_norm_g[l], kv_norm_g[l],
                              w_uq[l], w_ukv[l], B, S, q_scale)
        attn, (pw, wop, wom, wout) = _attention(
            qt, k, vt, B, S,
            (pool_w[l].reshape(POOL_WIDTH, POOL_GROUP_DIM), w_o_pool[l], w_o_mla[l], w_out[l]))
        xt = _merge_out(xt, mod4, attn.reshape(B * S, MLA_WIDTH), zbig,
                        pw.reshape(POOL_GROUPS, POOL_GROUP_DIM, POOL_GROUP_DIM), pool_scale[l],
                        wop, wom, wout, final_g, S, final_norm=(l == depth - 1))
    return xt.reshape(B, S, D)
```

```python
import functools
import math

import jax
import jax.numpy as jnp
from jax import lax
from jax.experimental import pallas as pl
from jax.experimental.pallas import tpu as pltpu

EPS = 1e-6
N_HEADS = 16
QK_NOPE = 128
QK_ROPE = 64
QK_HEAD = QK_NOPE + QK_ROPE
V_HEAD = 128
Q_LORA = 512
KV_LORA = 512
MLA_WIDTH = N_HEADS * V_HEAD
ROPE_THETA = 10000.0
POOL_WINDOWS = (2, 4, 8, 16)
POOL_GROUPS = len(POOL_WINDOWS)
POOL_GROUP_DIM = 256
POOL_WIDTH = POOL_GROUPS * POOL_GROUP_DIM
POOL_HALO = 16
SMALL_WIDTH = Q_LORA + KV_LORA + 2 * QK_ROPE

V7X_VMEM_LIMIT = 56 * 1024 * 1024
BF16_SUBLANES = 16
MIN_SOFTMAX_MASS = 2.0 ** -60

F32 = jnp.float32
BF16 = jnp.bfloat16
NT_DIMS = (((1,), (1,)), ((), ()))


def _sigmoid(v):
    return 0.5 * jnp.tanh(0.5 * v) + 0.5


def _params(semantics, vmem=V7X_VMEM_LIMIT, flags=None):
    return pltpu.CompilerParams(dimension_semantics=semantics, vmem_limit_bytes=vmem, flags=flags)


def _adaln_kernel(ct_ref, w_ref, b_ref, o_ref):
    w = w_ref[...]
    for b in range(ct_ref.shape[1]):
        cb = ct_ref[:, b:b + 1]
        act = cb * _sigmoid(cb)
        o_ref[b:b + 1, :] = jnp.sum(w * act, axis=0, keepdims=True) + b_ref[...]


def _adaln(c, w, bias):
    B, D = c.shape
    n = w.shape[1]
    tn = 1024
    return pl.pallas_call(
        _adaln_kernel,
        grid=(n // tn,),
        in_specs=[pl.BlockSpec((D, B), lambda j: (0, 0)),
                  pl.BlockSpec((D, tn), lambda j: (0, j)),
                  pl.BlockSpec((1, tn), lambda j: (0, j))],
        out_specs=pl.BlockSpec((B, tn), lambda j: (0, j)),
        out_shape=jax.ShapeDtypeStruct((B, n), F32),
        compiler_params=_params(("arbitrary",)),
        name="adaln",
    )(c.T, w, bias.reshape(1, n))


NORM_ROW_CHUNKS = 4


def _norm_proj_kernel(x_ref, shift_ref, scale_ref, g_ref, ws_ref, h_ref, zs_ref, ws_bf16):
    @pl.when(pl.program_id(0) == 0)
    def _():
        ws_bf16[...] = ws_ref[...].T.astype(BF16)

    rows = x_ref.shape[0] // NORM_ROW_CHUNKS
    for c in range(NORM_ROW_CHUNKS):
        sl = slice(c * rows, (c + 1) * rows)
        x = x_ref[sl, :]
        y = x * lax.rsqrt(jnp.mean(x * x, axis=-1, keepdims=True) + EPS) * g_ref[...]
        h = (y * (1.0 + scale_ref[...]) + shift_ref[...]).astype(BF16)
        h_ref[sl, :] = h
        zs_ref[sl, :] = jnp.dot(h, ws_bf16[...], preferred_element_type=F32)


def _norm_proj(xt, mod4, norm_g, w_in_t, seq):
    T, D = xt.shape
    tm = 512
    per_b = seq // tm
    return pl.pallas_call(
        _norm_proj_kernel,
        grid=(T // tm,),
        in_specs=[pl.BlockSpec((tm, D), lambda i: (i, 0)),
                  pl.BlockSpec((None, None, 1, D), lambda i: (i // per_b, 0, 0, 0)),
                  pl.BlockSpec((None, None, 1, D), lambda i: (i // per_b, 1, 0, 0)),
                  pl.BlockSpec((1, D), lambda i: (0, 0)),
                  pl.BlockSpec((SMALL_WIDTH, D), lambda i: (0, 0), pipeline_mode=pl.Buffered(1))],
        out_specs=[pl.BlockSpec((tm, D), lambda i: (i, 0)),
                   pl.BlockSpec((tm, SMALL_WIDTH), lambda i: (i, 0))],
        out_shape=[jax.ShapeDtypeStruct((T, D), BF16),
                   jax.ShapeDtypeStruct((T, SMALL_WIDTH), F32)],
        scratch_shapes=[pltpu.VMEM((D, SMALL_WIDTH), BF16)],
        compiler_params=_params(("arbitrary",)),
        name="norm_proj",
    )(xt, mod4, mod4, norm_g.reshape(1, D), w_in_t)


_ACTIVATIONS = {
    "silu": lambda a: a * _sigmoid(a),
    "linear": lambda a: a,
    "sigmoid": _sigmoid,
}


def _gate_proj_kernel(h_ref, wa_ref, wb_ref, o_ref, w_bf16, *, tile_kinds, shift):
    j = pl.program_id(0)
    tn = w_bf16.shape[0]

    @pl.when(pl.program_id(1) == 0)
    def _():
        w_bf16[0:tn - shift, :] = wa_ref[shift:, :].astype(BF16)
        w_bf16[tn - shift:, :] = wb_ref[...].astype(BF16)

    for kind, act in _ACTIVATIONS.items():
        tiles = [t for t, k in enumerate(tile_kinds) if k == kind]
        cond = functools.reduce(jnp.logical_or, [j == t for t in tiles])

        @pl.when(cond)
        def _(act=act):
            acc = lax.dot_general(h_ref[...], w_bf16[...], NT_DIMS, preferred_element_type=F32)
            o_ref[...] = act(acc).astype(o_ref.dtype)


def _gate_proj(h, w_in_t):
    T, D = h.shape
    tm, tn = 1024, 1024
    start = Q_LORA + KV_LORA + QK_ROPE
    n = w_in_t.shape[0] - start
    first_blk, shift = divmod(start, tn)
    assert n % tn == 0 and tn % shift == 0 and shift % BF16_SUBLANES == 0
    tile_kinds = (("silu",) * (MLA_WIDTH // tn) + ("linear",) * (POOL_WIDTH // tn)
                  + ("silu",) * (POOL_WIDTH // tn) + ("sigmoid",) * (2 * D // tn))
    kern = functools.partial(_gate_proj_kernel, tile_kinds=tile_kinds, shift=shift)
    return pl.pallas_call(
        kern,
        grid=(n // tn, T // tm),
        in_specs=[pl.BlockSpec((tm, D), lambda j, i: (i, 0)),
                  pl.BlockSpec((tn, D), lambda j, i: (first_blk + j, 0)),
                  pl.BlockSpec((shift, D), lambda j, i: ((first_blk + j + 1) * (tn // shift), 0))],
        out_specs=pl.BlockSpec((tm, tn), lambda j, i: (i, j)),
        out_shape=jax.ShapeDtypeStruct((T, n), BF16),
        scratch_shapes=[pltpu.VMEM((tn, D), BF16)],
        compiler_params=_params(("arbitrary", "arbitrary")),
        name="gate_proj",
    )(h, w_in_t, w_in_t)


def _mla_prep_kernel(zs_ref, posr_ref, invc_ref, qg_ref, kvg_ref, wuq_ref, wukv_ref,
                     qt_ref, k_ref, vt_ref, wq_ref, wk_ref, wv_ref, *, q_scale):
    @pl.when(pl.program_id(0) == 0)
    def _():
        wq_ref[...] = wuq_ref[...].T.astype(BF16)
        kv_head = QK_NOPE + V_HEAD
        for h in range(N_HEADS):
            wk_ref[:, h * QK_NOPE:(h + 1) * QK_NOPE] = (
                wukv_ref[:, h * kv_head:h * kv_head + QK_NOPE].astype(BF16))
            wv_ref[h * V_HEAD:(h + 1) * V_HEAD, :] = (
                wukv_ref[:, h * kv_head + QK_NOPE:(h + 1) * kv_head].T.astype(BF16))

    def rms(v, g):
        return (v * lax.rsqrt(jnp.mean(v * v, axis=-1, keepdims=True) + EPS) * g).astype(BF16)

    cqn = rms(zs_ref[:, 0:Q_LORA], qg_ref[...])
    ckvn = rms(zs_ref[:, Q_LORA:Q_LORA + KV_LORA], kvg_ref[...])
    kr = zs_ref[:, Q_LORA + KV_LORA:Q_LORA + KV_LORA + QK_ROPE]
    kr_sw = jnp.concatenate([kr[:, QK_ROPE // 2:], kr[:, :QK_ROPE // 2]], axis=1)

    ang_t = invc_ref[...] * posr_ref[...].astype(F32)
    cos_t, sin_t = jnp.cos(ang_t), jnp.sin(ang_t)
    cos, sin = cos_t.T, sin_t.T

    qf = lax.dot_general(wq_ref[...], cqn, NT_DIMS, preferred_element_type=F32)
    half = QK_ROPE // 2
    for h in range(N_HEADS):
        r0 = h * QK_HEAD + QK_NOPE
        qt_ref[h * QK_HEAD:r0, :] = (qf[h * QK_HEAD:r0] * q_scale).astype(BF16)
        rope = qf[r0:r0 + QK_ROPE]
        rope_sw = jnp.concatenate([rope[half:], rope[:half]], axis=0)
        qt_ref[r0:r0 + QK_ROPE, :] = ((rope * cos_t + rope_sw * sin_t) * q_scale).astype(BF16)

    kn = jnp.dot(ckvn, wk_ref[...], preferred_element_type=F32)
    k_rot = (kr * cos + kr_sw * sin).astype(BF16)
    for h in range(N_HEADS):
        k_ref[h, :, 0:QK_NOPE] = kn[:, h * QK_NOPE:(h + 1) * QK_NOPE].astype(BF16)
        k_ref[h, :, QK_NOPE:QK_HEAD] = k_rot

    vt_ref[...] = lax.dot_general(wv_ref[...], ckvn, NT_DIMS,
                                  preferred_element_type=F32).astype(BF16)


def _mla_prep(zs, posr, inv_signed, q_norm_g, kv_norm_g, w_uq, w_ukv, batch, seq, q_scale):
    tm = 512
    per_b = seq // tm
    const = lambda i: (0, 0)
    resident = functools.partial(pl.BlockSpec, pipeline_mode=pl.Buffered(1))
    kern = functools.partial(_mla_prep_kernel, q_scale=q_scale)
    return pl.pallas_call(
        kern,
        grid=(batch * per_b,),
        in_specs=[pl.BlockSpec((tm, SMALL_WIDTH), lambda i: (i, 0)),
                  pl.BlockSpec((None, 1, tm), lambda i: (i // per_b, 0, i % per_b)),
                  pl.BlockSpec((QK_ROPE, 1), const),
                  pl.BlockSpec((1, Q_LORA), const),
                  pl.BlockSpec((1, KV_LORA), const),
                  resident(w_uq.shape, const),
                  resident(w_ukv.shape, const)],
        out_specs=[pl.BlockSpec((None, N_HEADS * QK_HEAD, tm), lambda i: (i // per_b, 0, i % per_b)),
                   pl.BlockSpec((None, N_HEADS, tm, QK_HEAD), lambda i: (i // per_b, 0, i % per_b, 0)),
                   pl.BlockSpec((None, MLA_WIDTH, tm), lambda i: (i // per_b, 0, i % per_b))],
        out_shape=[jax.ShapeDtypeStruct((batch, N_HEADS * QK_HEAD, seq), BF16),
                   jax.ShapeDtypeStruct((batch, N_HEADS, seq, QK_HEAD), BF16),
                   jax.ShapeDtypeStruct((batch, MLA_WIDTH, seq), BF16)],
        scratch_shapes=[pltpu.VMEM((N_HEADS * QK_HEAD, Q_LORA), BF16),
                        pltpu.VMEM((KV_LORA, N_HEADS * QK_NOPE), BF16),
                        pltpu.VMEM((MLA_WIDTH, KV_LORA), BF16)],
        compiler_params=_params(("arbitrary",)),
        name="mla_prep",
    )(zs, posr, inv_signed.reshape(QK_ROPE, 1),
      q_norm_g.reshape(1, Q_LORA), kv_norm_g.reshape(1, KV_LORA), w_uq, w_ukv)


def _attention_kernel(qt_ref, k_ref, knext_ref, vt_ref, *refs, n_cast):
    cast_in, (o_ref, *cast_out) = refs[:n_cast], refs[n_cast:2 * n_cast + 1]
    s_a, m_a, s_b, m_b, lmin_ref, ksq_ref = refs[2 * n_cast + 1:]
    for w_in, w_out in zip(cast_in, cast_out):
        w_out[...] = w_in[...].astype(w_out.dtype)

    def max_sq_norm(rows):
        rf = rows.astype(F32)
        return jnp.max(jnp.sum(rf * rf, axis=1, keepdims=True), axis=0, keepdims=True)

    tq = lmin_ref.shape[1]
    nq = qt_ref.shape[1] // tq

    @pl.when(jnp.logical_and(pl.program_id(0) == 0, pl.program_id(1) == 0))
    def _():
        ksq_ref[...] = max_sq_norm(k_ref[...])

    k_norm = jnp.sqrt(ksq_ref[...])
    lmin_ref[...] = jnp.full(lmin_ref.shape, jnp.inf, F32)

    def fast_tile(i, next_ksq):
        off = pl.multiple_of(i * tq, tq)
        next_ksq = jnp.maximum(next_ksq, max_sq_norm(knext_ref[pl.ds(off, tq), :]))
        qt = qt_ref[:, pl.ds(off, tq)]
        qf = qt.astype(F32)
        shift = jnp.sqrt(jnp.sum(qf * qf, axis=0, keepdims=True)) * k_norm
        s = jnp.dot(k_ref[...], qt, preferred_element_type=F32)
        p = jnp.exp2(s - shift)
        l = jnp.sum(p, axis=0, keepdims=True)
        ot = jnp.dot(vt_ref[...], p.astype(BF16), preferred_element_type=F32)
        o_ref[pl.ds(off, tq), :] = (ot / l).T.astype(o_ref.dtype)
        lmin_ref[...] = jnp.minimum(lmin_ref[...], l)
        return next_ksq

    ksq_ref[...] = lax.fori_loop(0, nq, fast_tile, jnp.zeros((1, 1), F32), unroll=4)
    trusted = jnp.min(lmin_ref[...]) >= MIN_SOFTMAX_MASS

    @pl.when(jnp.logical_not(trusted))
    def _():
        _attention_exact(qt_ref, k_ref, vt_ref, o_ref, s_a, m_a, s_b, m_b)


def _attention_exact(qt_ref, k_ref, vt_ref, o_ref, s_a, m_a, s_b, m_b):
    tq = s_a.shape[1]
    nq = qt_ref.shape[1] // tq

    def scores(i, s_ref, m_ref):
        off = pl.multiple_of(i * tq, tq)
        s = jnp.dot(k_ref[...], qt_ref[:, pl.ds(off, tq)], preferred_element_type=F32)
        s_ref[...] = s
        m_ref[...] = jnp.max(s, axis=0, keepdims=True)

    def finish(i, s_ref, m_ref):
        off = pl.multiple_of(i * tq, tq)
        p = jnp.exp2(s_ref[...] - m_ref[...])
        l = jnp.sum(p, axis=0, keepdims=True)
        ot = jnp.dot(vt_ref[...], p.astype(BF16), preferred_element_type=F32)
        o_ref[pl.ds(off, tq), :] = (ot / l).T.astype(o_ref.dtype)

    scores(0, s_a, m_a)

    def step(i, carry):
        @pl.when(i % 2 == 1)
        def _():
            scores(i, s_b, m_b)
            finish(i - 1, s_a, m_a)

        @pl.when(i % 2 == 0)
        def _():
            scores(i, s_a, m_a)
            finish(i - 1, s_b, m_b)

        return carry

    lax.fori_loop(1, nq, step, 0)
    finish(nq - 1, s_b, m_b)


def _attention(qt, k, vt, batch, seq, later_weights):
    tq = 256
    tq_fast = 512
    assert seq % (2 * tq) == 0 and seq % tq_fast == 0
    steps = batch * N_HEADS
    slabs = [w.shape[0] // steps for w in later_weights]
    assert all(s % BF16_SUBLANES == 0 and s * steps == w.shape[0]
               for s, w in zip(slabs, later_weights))
    step = lambda b, h: (b * N_HEADS + h, 0)
    cast_specs = [pl.BlockSpec((s, w.shape[1]), step) for s, w in zip(slabs, later_weights)]

    def next_head(b, h):
        t = jnp.minimum(b * N_HEADS + h + 1, steps - 1)
        return t // N_HEADS, t % N_HEADS, 0, 0

    outs = pl.pallas_call(
        functools.partial(_attention_kernel, n_cast=len(later_weights)),
        grid=(batch, N_HEADS),
        in_specs=[pl.BlockSpec((None, QK_HEAD, seq), lambda b, h: (b, h, 0)),
                  pl.BlockSpec((None, None, seq, QK_HEAD), lambda b, h: (b, h, 0, 0)),
                  pl.BlockSpec((None, None, seq, QK_HEAD), next_head),
                  pl.BlockSpec((None, V_HEAD, seq), lambda b, h: (b, h, 0))] + cast_specs,
        out_specs=[pl.BlockSpec((None, seq, V_HEAD), lambda b, h: (b, 0, h))] + cast_specs,
        out_shape=[jax.ShapeDtypeStruct((batch, seq, MLA_WIDTH), BF16)]
                  + [jax.ShapeDtypeStruct(w.shape, BF16) for w in later_weights],
        scratch_shapes=[pltpu.VMEM((seq, tq), F32), pltpu.VMEM((1, tq), F32),
                        pltpu.VMEM((seq, tq), F32), pltpu.VMEM((1, tq), F32),
                        pltpu.VMEM((1, tq_fast), F32), pltpu.VMEM((1, 1), F32)],
        compiler_params=_params(("arbitrary", "arbitrary")),
        name="attention",
    )(qt, k, k, vt, *later_weights)
    return outs[0], outs[1:]


def _merge_out_kernel(x_ref, gate_ref, attn_ref, gm_ref, gp_ref, vp_ref, vprev_ref, vnext_ref,
                      mm_ref, mp_ref, pw_ref, ps_ref, wop_ref, wom_ref, wout_ref, fg_ref, o_ref,
                      *, seq, final_norm):
    tm = x_ref.shape[0]
    t0 = (pl.program_id(0) % (seq // tm)) * tm

    gated = attn_ref[...] * gm_ref[...]

    cur = vp_ref[...].astype(F32)
    prev = jnp.where(t0 > 0, vprev_ref[...].astype(F32), 0.0)
    nxt = jnp.where(t0 + tm < seq, vnext_ref[...].astype(F32), 0.0)
    ext = jnp.concatenate([prev, cur, nxt], axis=0)
    n_ext = tm + 2 * POOL_HALO
    tok = t0 + lax.broadcasted_iota(jnp.int32, (tm, 1), 0)
    mixed = []
    p_mla = []
    mla_cols = wom_ref.shape[1] // POOL_GROUPS
    for g, w in enumerate(POOL_WINDOWS):
        p_mla.append(jnp.dot(gated, wom_ref[:, g * mla_cols:(g + 1) * mla_cols],
                             preferred_element_type=F32))
        acc = ext[:, g * POOL_GROUP_DIM:(g + 1) * POOL_GROUP_DIM]
        acc = acc + pltpu.roll(acc, 1, axis=0)
        half = 1
        while 2 * half < w:
            acc = pltpu.roll(acc, half, axis=0) + pltpu.roll(acc, n_ext - half, axis=0)
            half *= 2
        wsum = acc[POOL_HALO:POOL_HALO + tm]
        count = (jnp.minimum(tok + w // 2, seq) - jnp.maximum(tok - w // 2, 0)).astype(F32)
        pooled = wsum / count - cur[:, g * POOL_GROUP_DIM:(g + 1) * POOL_GROUP_DIM]
        mixed.append(jnp.dot(pooled.astype(BF16), pw_ref[g], preferred_element_type=F32))
    mixed = jnp.concatenate(mixed, axis=1)
    p_mla = jnp.concatenate(p_mla, axis=1)
    u = (mixed * ps_ref[...] * gp_ref[...].astype(F32)).astype(BF16)
    p_pool = jnp.dot(u, wop_ref[...], preferred_element_type=F32)

    y = mm_ref[...].astype(F32) * p_mla + mp_ref[...].astype(F32) * p_pool
    r = jnp.dot(y.astype(BF16), wout_ref[...], preferred_element_type=F32)
    xo = x_ref[...] + gate_ref[...] * r
    if final_norm:
        xo = xo * lax.rsqrt(jnp.mean(xo * xo, axis=-1, keepdims=True) + EPS) * fg_ref[...]
    o_ref[...] = xo


def _merge_out(xt, mod4, attn, zbig, pool_w, pool_scale, w_o_pool, w_o_mla, w_out, final_g,
               seq, final_norm):
    T, D = xt.shape
    tm = 256
    per_b = seq // tm
    halo_per_tile = tm // POOL_HALO
    n_halo = T // POOL_HALO
    vp_blk = MLA_WIDTH // POOL_WIDTH
    gp_blk = vp_blk + 1
    mm_blk = (MLA_WIDTH + 2 * POOL_WIDTH) // D
    resident = functools.partial(pl.BlockSpec, pipeline_mode=pl.Buffered(1))
    kern = functools.partial(_merge_out_kernel, seq=seq, final_norm=final_norm)
    return pl.pallas_call(
        kern,
        grid=(T // tm,),
        in_specs=[pl.BlockSpec((tm, D), lambda i: (i, 0)),
                  pl.BlockSpec((None, None, 1, D), lambda i: (i // per_b, 2, 0, 0)),
                  pl.BlockSpec((tm, MLA_WIDTH), lambda i: (i, 0)),
                  pl.BlockSpec((tm, MLA_WIDTH), lambda i: (i, 0)),
                  pl.BlockSpec((tm, POOL_WIDTH), lambda i: (i, gp_blk)),
                  pl.BlockSpec((tm, POOL_WIDTH), lambda i: (i, vp_blk)),
                  pl.BlockSpec((POOL_HALO, POOL_WIDTH),
                               lambda i: (jnp.maximum(i * halo_per_tile - 1, 0), vp_blk)),
                  pl.BlockSpec((POOL_HALO, POOL_WIDTH),
                               lambda i: (jnp.minimum((i + 1) * halo_per_tile, n_halo - 1), vp_blk)),
                  pl.BlockSpec((tm, D), lambda i: (i, mm_blk)),
                  pl.BlockSpec((tm, D), lambda i: (i, mm_blk + 1)),
                  resident(pool_w.shape, lambda i: (0, 0, 0)),
                  resident((1, POOL_WIDTH), lambda i: (0, 0)),
                  resident(w_o_pool.shape, lambda i: (0, 0)),
                  resident(w_o_mla.shape, lambda i: (0, 0)),
                  resident(w_out.shape, lambda i: (0, 0)),
                  resident((1, D), lambda i: (0, 0))],
        out_specs=pl.BlockSpec((tm, D), lambda i: (i, 0)),
        out_shape=jax.ShapeDtypeStruct((T, D), F32),
        compiler_params=_params(("arbitrary",)),
        name="merge_out",
    )(xt, mod4, attn, zbig, zbig, zbig, zbig, zbig, zbig, zbig,
      pool_w, pool_scale.reshape(1, POOL_WIDTH), w_o_pool, w_o_mla, w_out, final_g.reshape(1, D))


def kernel(x, c, positions, ada_w, ada_b, norm_g, w_in, q_norm_g, w_uq, kv_norm_g, w_ukv, w_o_mla,
           pool_w, pool_scale, w_o_pool, w_out, final_g):
    B, S, D = x.shape
    depth = ada_w.shape[0]
    inv_freq = 1.0 / (ROPE_THETA ** (jnp.arange(0, QK_ROPE, 2, dtype=F32) / QK_ROPE))
    inv_signed = jnp.concatenate([-inv_freq, inv_freq])
    posr = positions.reshape(B, 1, S)
    q_scale = QK_HEAD ** -0.5 * math.log2(math.e)

    xt = x.reshape(B * S, D)
    for l in range(depth):
        mod4 = _adaln(c, ada_w[l], ada_b[l]).reshape(B, 3, 1, D)
        w_in_t = w_in[l].T
        h, zs = _norm_proj(xt, mod4, norm_g[l], w_in_t, S)
        zbig = _gate_proj(h, w_in_t)
        qt, k, vt = _mla_prep(zs, posr, inv_signed, q_norm_g[l], kv_norm_g[l],
                              w_uq[l], w_ukv[l], B, S, q_scale)
        attn, (pw, wop, wom, wout) = _attention(
            qt, k, vt, B, S,
            (pool_w[l].reshape(POOL_WIDTH, POOL_GROUP_DIM), w_o_pool[l], w_o_mla[l], w_out[l]))
        xt = _merge_out(xt, mod4, attn.reshape(B * S, MLA_WIDTH), zbig,
                        pw.reshape(POOL_GROUPS, POOL_GROUP_DIM, POOL_GROUP_DIM), pool_scale[l],
                        wop, wom, wout, final_g, S, final_norm=(l == depth - 1))
    return xt.reshape(B, S, D)
```

```python
import functools
import math

import jax
import jax.numpy as jnp
from jax import lax
from jax.experimental import pallas as pl
from jax.experimental.pallas import tpu as pltpu

EPS = 1e-6
N_HEADS = 16
QK_NOPE = 128
QK_ROPE = 64
QK_HEAD = QK_NOPE + QK_ROPE
V_HEAD = 128
Q_LORA = 512
KV_LORA = 512
MLA_WIDTH = N_HEADS * V_HEAD
ROPE_THETA = 10000.0
POOL_WINDOWS = (2, 4, 8, 16)
POOL_GROUPS = len(POOL_WINDOWS)
POOL_GROUP_DIM = 256
POOL_WIDTH = POOL_GROUPS * POOL_GROUP_DIM
POOL_HALO = 16
SMALL_WIDTH = Q_LORA + KV_LORA + 2 * QK_ROPE

V7X_VMEM_LIMIT = 56 * 1024 * 1024
BF16_SUBLANES = 16
F32_SUBLANES = 8
MIN_SOFTMAX_MASS = 2.0 ** -60

F32 = jnp.float32
BF16 = jnp.bfloat16
NT_DIMS = (((1,), (1,)), ((), ()))


def _sigmoid(v):
    return 0.5 * jnp.tanh(0.5 * v) + 0.5


def _params(semantics, vmem=V7X_VMEM_LIMIT, flags=None):
    return pltpu.CompilerParams(dimension_semantics=semantics, vmem_limit_bytes=vmem, flags=flags)


def _adaln_kernel(ct_ref, w_ref, b_ref, o_ref):
    w = w_ref[...]
    for b in range(ct_ref.shape[1]):
        cb = ct_ref[:, b:b + 1]
        act = cb * _sigmoid(cb)
        o_ref[b:b + 1, :] = jnp.sum(w * act, axis=0, keepdims=True) + b_ref[...]


def _adaln(c, w, bias):
    B, D = c.shape
    n = w.shape[1]
    tn = 1024
    return pl.pallas_call(
        _adaln_kernel,
        grid=(n // tn,),
        in_specs=[pl.BlockSpec((D, B), lambda j: (0, 0)),
                  pl.BlockSpec((D, tn), lambda j: (0, j)),
                  pl.BlockSpec((1, tn), lambda j: (0, j))],
        out_specs=pl.BlockSpec((B, tn), lambda j: (0, j)),
        out_shape=jax.ShapeDtypeStruct((B, n), F32),
        compiler_params=_params(("arbitrary",)),
        name="adaln",
    )(c.T, w, bias.reshape(1, n))


NORM_ROW_CHUNKS = 4


def _norm_proj_kernel(x_ref, shift_ref, scale_ref, g_ref, ws_ref, h_ref, zs_ref, ws_bf16):
    @pl.when(pl.program_id(0) == 0)
    def _():
        ws_bf16[...] = ws_ref[...].T.astype(BF16)

    rows = x_ref.shape[0] // NORM_ROW_CHUNKS
    for c in range(NORM_ROW_CHUNKS):
        sl = slice(c * rows, (c + 1) * rows)
        x = x_ref[sl, :]
        y = x * lax.rsqrt(jnp.mean(x * x, axis=-1, keepdims=True) + EPS) * g_ref[...]
        h = (y * (1.0 + scale_ref[...]) + shift_ref[...]).astype(BF16)
        h_ref[sl, :] = h
        zs_ref[sl, :] = jnp.dot(h, ws_bf16[...], preferred_element_type=F32)


def _norm_proj(xt, mod4, norm_g, w_in_t, seq):
    T, D = xt.shape
    tm = 512
    per_b = seq // tm
    return pl.pallas_call(
        _norm_proj_kernel,
        grid=(T // tm,),
        in_specs=[pl.BlockSpec((tm, D), lambda i: (i, 0)),
                  pl.BlockSpec((None, None, 1, D), lambda i: (i // per_b, 0, 0, 0)),
                  pl.BlockSpec((None, None, 1, D), lambda i: (i // per_b, 1, 0, 0)),
                  pl.BlockSpec((1, D), lambda i: (0, 0)),
                  pl.BlockSpec((SMALL_WIDTH, D), lambda i: (0, 0), pipeline_mode=pl.Buffered(1))],
        out_specs=[pl.BlockSpec((tm, D), lambda i: (i, 0)),
                   pl.BlockSpec((tm, SMALL_WIDTH), lambda i: (i, 0))],
        out_shape=[jax.ShapeDtypeStruct((T, D), BF16),
                   jax.ShapeDtypeStruct((T, SMALL_WIDTH), F32)],
        scratch_shapes=[pltpu.VMEM((D, SMALL_WIDTH), BF16)],
        compiler_params=_params(("arbitrary",)),
        name="norm_proj",
    )(xt, mod4, mod4, norm_g.reshape(1, D), w_in_t)


_ACTIVATIONS = {
    "silu": lambda a: a * _sigmoid(a),
    "linear": lambda a: a,
    "sigmoid": _sigmoid,
}


def _gate_proj_kernel(h_ref, w_hbm, o_ref, w_f32, w_bf16, sem, *, tile_kinds, first_row):
    j = pl.program_id(0)
    tn = w_bf16.shape[0]

    def window_copy(tile):
        rows = pl.ds(pl.multiple_of(first_row + tile * tn, F32_SUBLANES), tn)
        return pltpu.make_async_copy(w_hbm.at[rows, :], w_f32, sem)

    @pl.when(pl.program_id(1) == 0)
    def _():
        @pl.when(j == 0)
        def _():
            window_copy(0).start()

        window_copy(j).wait()
        w_bf16[...] = w_f32[...].astype(BF16)

        @pl.when(j + 1 < pl.num_programs(0))
        def _():
            window_copy(j + 1).start()

    for kind, act in _ACTIVATIONS.items():
        tiles = [t for t, k in enumerate(tile_kinds) if k == kind]
        cond = functools.reduce(jnp.logical_or, [j == t for t in tiles])

        @pl.when(cond)
        def _(act=act):
            acc = lax.dot_general(h_ref[...], w_bf16[...], NT_DIMS, preferred_element_type=F32)
            o_ref[...] = act(acc).astype(o_ref.dtype)


def _gate_proj(h, w_in_t):
    T, D = h.shape
    tm, tn = 1024, 1024
    first_row = Q_LORA + KV_LORA + QK_ROPE
    n = w_in_t.shape[0] - first_row
    assert n % tn == 0 and first_row % F32_SUBLANES == 0 and tn % F32_SUBLANES == 0
    tile_kinds = (("silu",) * (MLA_WIDTH // tn) + ("linear",) * (POOL_WIDTH // tn)
                  + ("silu",) * (POOL_WIDTH // tn) + ("sigmoid",) * (2 * D // tn))
    kern = functools.partial(_gate_proj_kernel, tile_kinds=tile_kinds, first_row=first_row)
    return pl.pallas_call(
        kern,
        grid=(n // tn, T // tm),
        in_specs=[pl.BlockSpec((tm, D), lambda j, i: (i, 0)),
                  pl.BlockSpec(memory_space=pl.ANY)],
        out_specs=pl.BlockSpec((tm, tn), lambda j, i: (i, j)),
        out_shape=jax.ShapeDtypeStruct((T, n), BF16),
        scratch_shapes=[pltpu.VMEM((tn, D), F32), pltpu.VMEM((tn, D), BF16),
                        pltpu.SemaphoreType.DMA(())],
        compiler_params=_params(("arbitrary", "arbitrary")),
        name="gate_proj",
    )(h, w_in_t)


def _mla_prep_kernel(zs_ref, posr_ref, invc_ref, qg_ref, kvg_ref, wuq_ref, wukv_ref,
                     qt_ref, k_ref, vt_ref, wq_ref, wk_ref, wv_ref, *, q_scale):
    @pl.when(pl.program_id(0) == 0)
    def _():
        wq_ref[...] = wuq_ref[...].T.astype(BF16)
        kv_head = QK_NOPE + V_HEAD
        for h in range(N_HEADS):
            wk_ref[:, h * QK_NOPE:(h + 1) * QK_NOPE] = (
                wukv_ref[:, h * kv_head:h * kv_head + QK_NOPE].astype(BF16))
            wv_ref[h * V_HEAD:(h + 1) * V_HEAD, :] = (
                wukv_ref[:, h * kv_head + QK_NOPE:(h + 1) * kv_head].T.astype(BF16))

    def rms(v, g):
        return (v * lax.rsqrt(jnp.mean(v * v, axis=-1, keepdims=True) + EPS) * g).astype(BF16)

    cqn = rms(zs_ref[:, 0:Q_LORA], qg_ref[...])
    ckvn = rms(zs_ref[:, Q_LORA:Q_LORA + KV_LORA], kvg_ref[...])
    kr = zs_ref[:, Q_LORA + KV_LORA:Q_LORA + KV_LORA + QK_ROPE]
    kr_sw = jnp.concatenate([kr[:, QK_ROPE // 2:], kr[:, :QK_ROPE // 2]], axis=1)

    ang_t = invc_ref[...] * posr_ref[...].astype(F32)
    cos_t, sin_t = jnp.cos(ang_t), jnp.sin(ang_t)
    cos, sin = cos_t.T, sin_t.T

    qf = lax.dot_general(wq_ref[...], cqn, NT_DIMS, preferred_element_type=F32)
    half = QK_ROPE // 2
    for h in range(N_HEADS):
        r0 = h * QK_HEAD + QK_NOPE
        qt_ref[h * QK_HEAD:r0, :] = (qf[h * QK_HEAD:r0] * q_scale).astype(BF16)
        rope = qf[r0:r0 + QK_ROPE]
        rope_sw = jnp.concatenate([rope[half:], rope[:half]], axis=0)
        qt_ref[r0:r0 + QK_ROPE, :] = ((rope * cos_t + rope_sw * sin_t) * q_scale).astype(BF16)

    kn = jnp.dot(ckvn, wk_ref[...], preferred_element_type=F32)
    k_rot = (kr * cos + kr_sw * sin).astype(BF16)
    for h in range(N_HEADS):
        k_ref[h, :, 0:QK_NOPE] = kn[:, h * QK_NOPE:(h + 1) * QK_NOPE].astype(BF16)
        k_ref[h, :, QK_NOPE:QK_HEAD] = k_rot

    vt_ref[...] = lax.dot_general(wv_ref[...], ckvn, NT_DIMS,
                                  preferred_element_type=F32).astype(BF16)


def _mla_prep(zs, posr, inv_signed, q_norm_g, kv_norm_g, w_uq, w_ukv, batch, seq, q_scale):
    tm = 512
    per_b = seq // tm
    const = lambda i: (0, 0)
    resident = functools.partial(pl.BlockSpec, pipeline_mode=pl.Buffered(1))
    kern = functools.partial(_mla_prep_kernel, q_scale=q_scale)
    return pl.pallas_call(
        kern,
        grid=(batch * per_b,),
        in_specs=[pl.BlockSpec((tm, SMALL_WIDTH), lambda i: (i, 0)),
                  pl.BlockSpec((None, 1, tm), lambda i: (i // per_b, 0, i % per_b)),
                  pl.BlockSpec((QK_ROPE, 1), const),
                  pl.BlockSpec((1, Q_LORA), const),
                  pl.BlockSpec((1, KV_LORA), const),
                  resident(w_uq.shape, const),
                  resident(w_ukv.shape, const)],
        out_specs=[pl.BlockSpec((None, N_HEADS * QK_HEAD, tm), lambda i: (i // per_b, 0, i % per_b)),
                   pl.BlockSpec((None, N_HEADS, tm, QK_HEAD), lambda i: (i // per_b, 0, i % per_b, 0)),
                   pl.BlockSpec((None, MLA_WIDTH, tm), lambda i: (i // per_b, 0, i % per_b))],
        out_shape=[jax.ShapeDtypeStruct((batch, N_HEADS * QK_HEAD, seq), BF16),
                   jax.ShapeDtypeStruct((batch, N_HEADS, seq, QK_HEAD), BF16),
                   jax.ShapeDtypeStruct((batch, MLA_WIDTH, seq), BF16)],
        scratch_shapes=[pltpu.VMEM((N_HEADS * QK_HEAD, Q_LORA), BF16),
                        pltpu.VMEM((KV_LORA, N_HEADS * QK_NOPE), BF16),
                        pltpu.VMEM((MLA_WIDTH, KV_LORA), BF16)],
        compiler_params=_params(("arbitrary",)),
        name="mla_prep",
    )(zs, posr, inv_signed.reshape(QK_ROPE, 1),
      q_norm_g.reshape(1, Q_LORA), kv_norm_g.reshape(1, KV_LORA), w_uq, w_ukv)


def _attention_kernel(qt_ref, k_ref, knext_ref, vt_ref, *refs, n_cast):
    cast_in, (o_ref, *cast_out) = refs[:n_cast], refs[n_cast:2 * n_cast + 1]
    s_a, m_a, s_b, m_b, lmin_ref, ksq_ref = refs[2 * n_cast + 1:]
    for w_in, w_out in zip(cast_in, cast_out):
        w_out[...] = w_in[...].astype(w_out.dtype)

    def max_sq_norm(rows):
        rf = rows.astype(F32)
        return jnp.max(jnp.sum(rf * rf, axis=1, keepdims=True), axis=0, keepdims=True)

    tq = lmin_ref.shape[1]
    nq = qt_ref.shape[1] // tq

    @pl.when(jnp.logical_and(pl.program_id(0) == 0, pl.program_id(1) == 0))
    def _():
        ksq_ref[...] = max_sq_norm(k_ref[...])

    k_norm = jnp.sqrt(ksq_ref[...])
    lmin_ref[...] = jnp.full(lmin_ref.shape, jnp.inf, F32)

    def fast_tile(i, next_ksq):
        off = pl.multiple_of(i * tq, tq)
        next_ksq = jnp.maximum(next_ksq, max_sq_norm(knext_ref[pl.ds(off, tq), :]))
        qt = qt_ref[:, pl.ds(off, tq)]
        qf = qt.astype(F32)
        shift = jnp.sqrt(jnp.sum(qf * qf, axis=0, keepdims=True)) * k_norm
        s = jnp.dot(k_ref[...], qt, preferred_element_type=F32)
        p = jnp.exp2(s - shift)
        l = jnp.sum(p, axis=0, keepdims=True)
        ot = jnp.dot(vt_ref[...], p.astype(BF16), preferred_element_type=F32)
        o_ref[pl.ds(off, tq), :] = (ot / l).T.astype(o_ref.dtype)
        lmin_ref[...] = jnp.minimum(lmin_ref[...], l)
        return next_ksq

    ksq_ref[...] = lax.fori_loop(0, nq, fast_tile, jnp.zeros((1, 1), F32), unroll=4)
    trusted = jnp.min(lmin_ref[...]) >= MIN_SOFTMAX_MASS

    @pl.when(jnp.logical_not(trusted))
    def _():
        _attention_exact(qt_ref, k_ref, vt_ref, o_ref, s_a, m_a, s_b, m_b)


def _attention_exact(qt_ref, k_ref, vt_ref, o_ref, s_a, m_a, s_b, m_b):
    tq = s_a.shape[1]
    nq = qt_ref.shape[1] // tq

    def scores(i, s_ref, m_ref):
        off = pl.multiple_of(i * tq, tq)
        s = jnp.dot(k_ref[...], qt_ref[:, pl.ds(off, tq)], preferred_element_type=F32)
        s_ref[...] = s
        m_ref[...] = jnp.max(s, axis=0, keepdims=True)

    def finish(i, s_ref, m_ref):
        off = pl.multiple_of(i * tq, tq)
        p = jnp.exp2(s_ref[...] - m_ref[...])
        l = jnp.sum(p, axis=0, keepdims=True)
        ot = jnp.dot(vt_ref[...], p.astype(BF16), preferred_element_type=F32)
        o_ref[pl.ds(off, tq), :] = (ot / l).T.astype(o_ref.dtype)

    scores(0, s_a, m_a)

    def step(i, carry):
        @pl.when(i % 2 == 1)
        def _():
            scores(i, s_b, m_b)
            finish(i - 1, s_a, m_a)

        @pl.when(i % 2 == 0)
        def _():
            scores(i, s_a, m_a)
            finish(i - 1, s_b, m_b)

        return carry

    lax.fori_loop(1, nq, step, 0)
    finish(nq - 1, s_b, m_b)


def _attention(qt, k, vt, batch, seq, later_weights):
    tq = 256
    tq_fast = 512
    assert seq % (2 * tq) == 0 and seq % tq_fast == 0
    steps = batch * N_HEADS
    slabs = [w.shape[0] // steps for w in later_weights]
    assert all(s % BF16_SUBLANES == 0 and s * steps == w.shape[0]
               for s, w in zip(slabs, later_weights))
    step = lambda b, h: (b * N_HEADS + h, 0)
    cast_specs = [pl.BlockSpec((s, w.shape[1]), step) for s, w in zip(slabs, later_weights)]

    def next_head(b, h):
        t = jnp.minimum(b * N_HEADS + h + 1, steps - 1)
        return t // N_HEADS, t % N_HEADS, 0, 0

    outs = pl.pallas_call(
        functools.partial(_attention_kernel, n_cast=len(later_weights)),
        grid=(batch, N_HEADS),
        in_specs=[pl.BlockSpec((None, QK_HEAD, seq), lambda b, h: (b, h, 0)),
                  pl.BlockSpec((None, None, seq, QK_HEAD), lambda b, h: (b, h, 0, 0)),
                  pl.BlockSpec((None, None, seq, QK_HEAD), next_head),
                  pl.BlockSpec((None, V_HEAD, seq), lambda b, h: (b, h, 0))] + cast_specs,
        out_specs=[pl.BlockSpec((None, seq, V_HEAD), lambda b, h: (b, 0, h))] + cast_specs,
        out_shape=[jax.ShapeDtypeStruct((batch, seq, MLA_WIDTH), BF16)]
                  + [jax.ShapeDtypeStruct(w.shape, BF16) for w in later_weights],
        scratch_shapes=[pltpu.VMEM((seq, tq), F32), pltpu.VMEM((1, tq), F32),
                        pltpu.VMEM((seq, tq), F32), pltpu.VMEM((1, tq), F32),
                        pltpu.VMEM((1, tq_fast), F32), pltpu.VMEM((1, 1), F32)],
        compiler_params=_params(("arbitrary", "arbitrary")),
        name="attention",
    )(qt, k, k, vt, *later_weights)
    return outs[0], outs[1:]


def _merge_out_kernel(x_ref, gate_ref, attn_ref, gm_ref, gp_ref, vp_ref, vprev_ref, vnext_ref,
                      mm_ref, mp_ref, pw_ref, ps_ref, wop_ref, wom_ref, wout_ref, fg_ref, o_ref,
                      *, seq, final_norm):
    tm = x_ref.shape[0]
    t0 = (pl.program_id(0) % (seq // tm)) * tm

    gated = attn_ref[...] * gm_ref[...]

    cur = vp_ref[...].astype(F32)
    prev = jnp.where(t0 > 0, vprev_ref[...].astype(F32), 0.0)
    nxt = jnp.where(t0 + tm < seq, vnext_ref[...].astype(F32), 0.0)
    ext = jnp.concatenate([prev, cur, nxt], axis=0)
    n_ext = tm + 2 * POOL_HALO
    tok = t0 + lax.broadcasted_iota(jnp.int32, (tm, 1), 0)
    mixed = []
    p_mla = []
    mla_cols = wom_ref.shape[1] // POOL_GROUPS
    for g, w in enumerate(POOL_WINDOWS):
        p_mla.append(jnp.dot(gated, wom_ref[:, g * mla_cols:(g + 1) * mla_cols],
                             preferred_element_type=F32))
        acc = ext[:, g * POOL_GROUP_DIM:(g + 1) * POOL_GROUP_DIM]
        acc = acc + pltpu.roll(acc, 1, axis=0)
        half = 1
        while 2 * half < w:
            acc = pltpu.roll(acc, half, axis=0) + pltpu.roll(acc, n_ext - half, axis=0)
            half *= 2
        wsum = acc[POOL_HALO:POOL_HALO + tm]
        count = (jnp.minimum(tok + w // 2, seq) - jnp.maximum(tok - w // 2, 0)).astype(F32)
        pooled = wsum / count - cur[:, g * POOL_GROUP_DIM:(g + 1) * POOL_GROUP_DIM]
        mixed.append(jnp.dot(pooled.astype(BF16), pw_ref[g], preferred_element_type=F32))
    mixed = jnp.concatenate(mixed, axis=1)
    p_mla = jnp.concatenate(p_mla, axis=1)
    u = (mixed * ps_ref[...] * gp_ref[...].astype(F32)).astype(BF16)
    p_pool = jnp.dot(u, wop_ref[...], preferred_element_type=F32)

    y = mm_ref[...].astype(F32) * p_mla + mp_ref[...].astype(F32) * p_pool
    r = jnp.dot(y.astype(BF16), wout_ref[...], preferred_element_type=F32)
    xo = x_ref[...] + gate_ref[...] * r
    if final_norm:
        xo = xo * lax.rsqrt(jnp.mean(xo * xo, axis=-1, keepdims=True) + EPS) * fg_ref[...]
    o_ref[...] = xo


def _merge_out(xt, mod4, attn, zbig, pool_w, pool_scale, w_o_pool, w_o_mla, w_out, final_g,
               seq, final_norm):
    T, D = xt.shape
    tm = 256
    per_b = seq // tm
    halo_per_tile = tm // POOL_HALO
    n_halo = T // POOL_HALO
    vp_blk = MLA_WIDTH // POOL_WIDTH
    gp_blk = vp_blk + 1
    mm_blk = (MLA_WIDTH + 2 * POOL_WIDTH) // D
    resident = functools.partial(pl.BlockSpec, pipeline_mode=pl.Buffered(1))
    kern = functools.partial(_merge_out_kernel, seq=seq, final_norm=final_norm)
    return pl.pallas_call(
        kern,
        grid=(T // tm,),
        in_specs=[pl.BlockSpec((tm, D), lambda i: (i, 0)),
                  pl.BlockSpec((None, None, 1, D), lambda i: (i // per_b, 2, 0, 0)),
                  pl.BlockSpec((tm, MLA_WIDTH), lambda i: (i, 0)),
                  pl.BlockSpec((tm, MLA_WIDTH), lambda i: (i, 0)),
                  pl.BlockSpec((tm, POOL_WIDTH), lambda i: (i, gp_blk)),
                  pl.BlockSpec((tm, POOL_WIDTH), lambda i: (i, vp_blk)),
                  pl.BlockSpec((POOL_HALO, POOL_WIDTH),
                               lambda i: (jnp.maximum(i * halo_per_tile - 1, 0), vp_blk)),
                  pl.BlockSpec((POOL_HALO, POOL_WIDTH),
                               lambda i: (jnp.minimum((i + 1) * halo_per_tile, n_halo - 1), vp_blk)),
                  pl.BlockSpec((tm, D), lambda i: (i, mm_blk)),
                  pl.BlockSpec((tm, D), lambda i: (i, mm_blk + 1)),
                  resident(pool_w.shape, lambda i: (0, 0, 0)),
                  resident((1, POOL_WIDTH), lambda i: (0, 0)),
                  resident(w_o_pool.shape, lambda i: (0, 0)),
                  resident(w_o_mla.shape, lambda i: (0, 0)),
                  resident(w_out.shape, lambda i: (0, 0)),
                  resident((1, D), lambda i: (0, 0))],
        out_specs=pl.BlockSpec((tm, D), lambda i: (i, 0)),
        out_shape=jax.ShapeDtypeStruct((T, D), F32),
        compiler_params=_params(("arbitrary",)),
        name="merge_out",
    )(xt, mod4, attn, zbig, zbig, zbig, zbig, zbig, zbig, zbig,
      pool_w, pool_scale.reshape(1, POOL_WIDTH), w_o_pool, w_o_mla, w_out, final_g.reshape(1, D))


def kernel(x, c, positions, ada_w, ada_b, norm_g, w_in, q_norm_g, w_uq, kv_norm_g, w_ukv, w_o_mla,
           pool_w, pool_scale, w_o_pool, w_out, final_g):
    B, S, D = x.shape
    depth = ada_w.shape[0]
    inv_freq = 1.0 / (ROPE_THETA ** (jnp.arange(0, QK_ROPE, 2, dtype=F32) / QK_ROPE))
    inv_signed = jnp.concatenate([-inv_freq, inv_freq])
    posr = positions.reshape(B, 1, S)
    q_scale = QK_HEAD ** -0.5 * math.log2(math.e)

    xt = x.reshape(B * S, D)
    for l in range(depth):
        mod4 = _adaln(c, ada_w[l], ada_b[l]).reshape(B, 3, 1, D)
        w_in_t = w_in[l].T
        h, zs = _norm_proj(xt, mod4, norm_g[l], w_in_t, S)
        zbig = _gate_proj(h, w_in_t)
        qt, k, vt = _mla_prep(zs, posr, inv_signed, q_norm_g[l], kv_norm_g[l],
                              w_uq[l], w_ukv[l], B, S, q_scale)
        attn, (pw, wop, wom, wout) = _attention(
            qt, k, vt, B, S,
            (pool_w[l].reshape(POOL_WIDTH, POOL_GROUP_DIM), w_o_pool[l], w_o_mla[l], w_out[l]))
        xt = _merge_out(xt, mod4, attn.reshape(B * S, MLA_WIDTH), zbig,
                        pw.reshape(POOL_GROUPS, POOL_GROUP_DIM, POOL_GROUP_DIM), pool_scale[l],
                        wop, wom, wout, final_g, S, final_norm=(l == depth - 1))
    return xt.reshape(B, S, D)
```

```python
import functools
import math

import jax
import jax.numpy as jnp
from jax import lax
from jax.experimental import pallas as pl
from jax.experimental.pallas import tpu as pltpu

EPS = 1e-6
N_HEADS = 16
QK_NOPE = 128
QK_ROPE = 64
QK_HEAD = QK_NOPE + QK_ROPE
V_HEAD = 128
Q_LORA = 512
KV_LORA = 512
MLA_WIDTH = N_HEADS * V_HEAD
ROPE_THETA = 10000.0
POOL_WINDOWS = (2, 4, 8, 16)
POOL_GROUPS = len(POOL_WINDOWS)
POOL_GROUP_DIM = 256
POOL_WIDTH = POOL_GROUPS * POOL_GROUP_DIM
POOL_HALO = 16
SMALL_WIDTH = Q_LORA + KV_LORA + 2 * QK_ROPE

V7X_VMEM_LIMIT = 56 * 1024 * 1024
BF16_SUBLANES = 16
F32_SUBLANES = 8
MIN_SOFTMAX_MASS = 2.0 ** -60

F32 = jnp.float32
BF16 = jnp.bfloat16
NT_DIMS = (((1,), (1,)), ((), ()))


def _sigmoid(v):
    return 0.5 * jnp.tanh(0.5 * v) + 0.5


def _params(semantics, vmem=V7X_VMEM_LIMIT, flags=None):
    return pltpu.CompilerParams(dimension_semantics=semantics, vmem_limit_bytes=vmem, flags=flags)


def _adaln_kernel(ct_ref, w_ref, b_ref, o_ref):
    w = w_ref[...]
    for b in range(ct_ref.shape[1]):
        cb = ct_ref[:, b:b + 1]
        act = cb * _sigmoid(cb)
        o_ref[b:b + 1, :] = jnp.sum(w * act, axis=0, keepdims=True) + b_ref[...]


def _adaln(c, w, bias):
    B, D = c.shape
    n = w.shape[1]
    tn = 1024
    return pl.pallas_call(
        _adaln_kernel,
        grid=(n // tn,),
        in_specs=[pl.BlockSpec((D, B), lambda j: (0, 0)),
                  pl.BlockSpec((D, tn), lambda j: (0, j)),
                  pl.BlockSpec((1, tn), lambda j: (0, j))],
        out_specs=pl.BlockSpec((B, tn), lambda j: (0, j)),
        out_shape=jax.ShapeDtypeStruct((B, n), F32),
        compiler_params=_params(("arbitrary",)),
        name="adaln",
    )(c.T, w, bias.reshape(1, n))


NORM_ROW_CHUNKS = 4


def _norm_proj_kernel(x_ref, shift_ref, scale_ref, g_ref, ws_ref, h_ref, zs_ref, ws_bf16):
    @pl.when(pl.program_id(0) == 0)
    def _():
        ws_bf16[...] = ws_ref[...].T.astype(BF16)

    rows = x_ref.shape[0] // NORM_ROW_CHUNKS
    for c in range(NORM_ROW_CHUNKS):
        sl = slice(c * rows, (c + 1) * rows)
        x = x_ref[sl, :]
        y = x * lax.rsqrt(jnp.mean(x * x, axis=-1, keepdims=True) + EPS) * g_ref[...]
        h = (y * (1.0 + scale_ref[...]) + shift_ref[...]).astype(BF16)
        h_ref[sl, :] = h
        zs_ref[sl, :] = jnp.dot(h, ws_bf16[...], preferred_element_type=F32)


def _norm_proj(xt, mod4, norm_g, w_in_t, seq):
    T, D = xt.shape
    tm = 512
    per_b = seq // tm
    return pl.pallas_call(
        _norm_proj_kernel,
        grid=(T // tm,),
        in_specs=[pl.BlockSpec((tm, D), lambda i: (i, 0)),
                  pl.BlockSpec((None, None, 1, D), lambda i: (i // per_b, 0, 0, 0)),
                  pl.BlockSpec((None, None, 1, D), lambda i: (i // per_b, 1, 0, 0)),
                  pl.BlockSpec((1, D), lambda i: (0, 0)),
                  pl.BlockSpec((SMALL_WIDTH, D), lambda i: (0, 0), pipeline_mode=pl.Buffered(1))],
        out_specs=[pl.BlockSpec((tm, D), lambda i: (i, 0)),
                   pl.BlockSpec((tm, SMALL_WIDTH), lambda i: (i, 0))],
        out_shape=[jax.ShapeDtypeStruct((T, D), BF16),
                   jax.ShapeDtypeStruct((T, SMALL_WIDTH), F32)],
        scratch_shapes=[pltpu.VMEM((D, SMALL_WIDTH), BF16)],
        compiler_params=_params(("arbitrary",)),
        name="norm_proj",
    )(xt, mod4, mod4, norm_g.reshape(1, D), w_in_t)


_ACTIVATIONS = {
    "silu": lambda a: a * _sigmoid(a),
    "linear": lambda a: a,
    "sigmoid": _sigmoid,
}


def _gate_proj_kernel(h_ref, w_hbm, o_ref, w_f32, w_bf16, sem, *, tile_kinds, first_row):
    j = pl.program_id(0)
    tn = w_bf16.shape[0]

    def window_copy(tile):
        rows = pl.ds(pl.multiple_of(first_row + tile * tn, F32_SUBLANES), tn)
        return pltpu.make_async_copy(w_hbm.at[rows, :], w_f32, sem)

    @pl.when(pl.program_id(1) == 0)
    def _():
        @pl.when(j == 0)
        def _():
            window_copy(0).start()

        window_copy(j).wait()
        w_bf16[...] = w_f32[...].astype(BF16)

        @pl.when(j + 1 < pl.num_programs(0))
        def _():
            window_copy(j + 1).start()

    for kinds in sorted(set(tile_kinds)):
        tiles = [t for t, k in enumerate(tile_kinds) if k == kinds]
        cond = functools.reduce(jnp.logical_or, [j == t for t in tiles])

        @pl.when(cond)
        def _(kinds=kinds):
            acc = lax.dot_general(h_ref[...], w_bf16[...], NT_DIMS, preferred_element_type=F32)
            width = tn // len(kinds)
            for s, kind in enumerate(kinds):
                cols = slice(s * width, (s + 1) * width)
                o_ref[:, cols] = _ACTIVATIONS[kind](acc[:, cols]).astype(o_ref.dtype)


def _gate_proj(h, w_in_t):
    T, D = h.shape
    tm, tn = 1024, 2048
    first_row = Q_LORA + KV_LORA + QK_ROPE
    n = w_in_t.shape[0] - first_row
    assert n % tn == 0 and first_row % F32_SUBLANES == 0 and tn % F32_SUBLANES == 0
    group_kinds = (("silu",) * (MLA_WIDTH // POOL_WIDTH) + ("linear", "silu")
                   + ("sigmoid",) * (2 * D // POOL_WIDTH))
    per_tile = tn // POOL_WIDTH
    tile_kinds = tuple(group_kinds[t * per_tile:(t + 1) * per_tile] for t in range(n // tn))
    kern =functools.partial(_gate_proj_kernel, tile_kinds=tile_kinds, first_row=first_row)
    return pl.pallas_call(
        kern,
        grid=(n // tn, T // tm),
        in_specs=[pl.BlockSpec((tm, D), lambda j, i: (i, 0)),
                  pl.BlockSpec(memory_space=pl.ANY)],
        out_specs=pl.BlockSpec((tm, tn), lambda j, i: (i, j)),
        out_shape=jax.ShapeDtypeStruct((T, n), BF16),
        scratch_shapes=[pltpu.VMEM((tn, D), F32), pltpu.VMEM((tn, D), BF16),
                        pltpu.SemaphoreType.DMA(())],
        compiler_params=_params(("arbitrary", "arbitrary")),
        name="gate_proj",
    )(h, w_in_t)


def _mla_prep_kernel(zs_ref, posr_ref, invc_ref, qg_ref, kvg_ref, wuq_ref, wukv_ref,
                     qt_ref, k_ref, vt_ref, wq_ref, wk_ref, wv_ref, *, q_scale):
    @pl.when(pl.program_id(0) == 0)
    def _():
        wq_ref[...] = wuq_ref[...].T.astype(BF16)
        kv_head = QK_NOPE + V_HEAD
        for h in range(N_HEADS):
            wk_ref[:, h * QK_NOPE:(h + 1) * QK_NOPE] = (
                wukv_ref[:, h * kv_head:h * kv_head + QK_NOPE].astype(BF16))
            wv_ref[h * V_HEAD:(h + 1) * V_HEAD, :] = (
                wukv_ref[:, h * kv_head + QK_NOPE:(h + 1) * kv_head].T.astype(BF16))

    def rms(v, g):
        return (v * lax.rsqrt(jnp.mean(v * v, axis=-1, keepdims=True) + EPS) * g).astype(BF16)

    cqn = rms(zs_ref[:, 0:Q_LORA], qg_ref[...])
    ckvn = rms(zs_ref[:, Q_LORA:Q_LORA + KV_LORA], kvg_ref[...])
    kr = zs_ref[:, Q_LORA + KV_LORA:Q_LORA + KV_LORA + QK_ROPE]
    kr_sw = jnp.concatenate([kr[:, QK_ROPE // 2:], kr[:, :QK_ROPE // 2]], axis=1)

    ang_t = invc_ref[...] * posr_ref[...].astype(F32)
    cos_t, sin_t = jnp.cos(ang_t), jnp.sin(ang_t)
    cos, sin = cos_t.T, sin_t.T

    qf = lax.dot_general(wq_ref[...], cqn, NT_DIMS, preferred_element_type=F32)
    half = QK_ROPE // 2
    for h in range(N_HEADS):
        r0 = h * QK_HEAD + QK_NOPE
        qt_ref[h * QK_HEAD:r0, :] = (qf[h * QK_HEAD:r0] * q_scale).astype(BF16)
        rope = qf[r0:r0 + QK_ROPE]
        rope_sw = jnp.concatenate([rope[half:], rope[:half]], axis=0)
        qt_ref[r0:r0 + QK_ROPE, :] = ((rope * cos_t + rope_sw * sin_t) * q_scale).astype(BF16)

    kn = jnp.dot(ckvn, wk_ref[...], preferred_element_type=F32)
    k_rot = (kr * cos + kr_sw * sin).astype(BF16)
    for h in range(N_HEADS):
        k_ref[h, :, 0:QK_NOPE] = kn[:, h * QK_NOPE:(h + 1) * QK_NOPE].astype(BF16)
        k_ref[h, :, QK_NOPE:QK_HEAD] = k_rot

    vt_ref[...] = lax.dot_general(wv_ref[...], ckvn, NT_DIMS,
                                  preferred_element_type=F32).astype(BF16)


def _mla_prep(zs, posr, inv_signed, q_norm_g, kv_norm_g, w_uq, w_ukv, batch, seq, q_scale):
    tm = 512
    per_b = seq // tm
    const = lambda i: (0, 0)
    resident = functools.partial(pl.BlockSpec, pipeline_mode=pl.Buffered(1))
    kern = functools.partial(_mla_prep_kernel, q_scale=q_scale)
    return pl.pallas_call(
        kern,
        grid=(batch * per_b,),
        in_specs=[pl.BlockSpec((tm, SMALL_WIDTH), lambda i: (i, 0)),
                  pl.BlockSpec((None, 1, tm), lambda i: (i // per_b, 0, i % per_b)),
                  pl.BlockSpec((QK_ROPE, 1), const),
                  pl.BlockSpec((1, Q_LORA), const),
                  pl.BlockSpec((1, KV_LORA), const),
                  resident(w_uq.shape, const),
                  resident(w_ukv.shape, const)],
        out_specs=[pl.BlockSpec((None, N_HEADS * QK_HEAD, tm), lambda i: (i // per_b, 0, i % per_b)),
                   pl.BlockSpec((None, N_HEADS, tm, QK_HEAD), lambda i: (i // per_b, 0, i % per_b, 0)),
                   pl.BlockSpec((None, MLA_WIDTH, tm), lambda i: (i // per_b, 0, i % per_b))],
        out_shape=[jax.ShapeDtypeStruct((batch, N_HEADS * QK_HEAD, seq), BF16),
                   jax.ShapeDtypeStruct((batch, N_HEADS, seq, QK_HEAD), BF16),
                   jax.ShapeDtypeStruct((batch, MLA_WIDTH, seq), BF16)],
        scratch_shapes=[pltpu.VMEM((N_HEADS * QK_HEAD, Q_LORA), BF16),
                        pltpu.VMEM((KV_LORA, N_HEADS * QK_NOPE), BF16),
                        pltpu.VMEM((MLA_WIDTH, KV_LORA), BF16)],
        compiler_params=_params(("arbitrary",)),
        name="mla_prep",
    )(zs, posr, inv_signed.reshape(QK_ROPE, 1),
      q_norm_g.reshape(1, Q_LORA), kv_norm_g.reshape(1, KV_LORA), w_uq, w_ukv)


def _attention_kernel(qt_ref, k_ref, knext_ref, vt_ref, *refs, n_cast):
    cast_in, (o_ref, *cast_out) = refs[:n_cast], refs[n_cast:2 * n_cast + 1]
    s_a, m_a, s_b, m_b, lmin_ref, ksq_ref = refs[2 * n_cast + 1:]
    for w_in, w_out in zip(cast_in, cast_out):
        w_out[...] = w_in[...].astype(w_out.dtype)

    def max_sq_norm(rows):
        rf = rows.astype(F32)
        return jnp.max(jnp.sum(rf * rf, axis=1, keepdims=True), axis=0, keepdims=True)

    tq = lmin_ref.shape[1]
    nq = qt_ref.shape[1] // tq

    @pl.when(jnp.logical_and(pl.program_id(0) == 0, pl.program_id(1) == 0))
    def _():
        ksq_ref[...] = max_sq_norm(k_ref[...])

    k_norm = jnp.sqrt(ksq_ref[...])
    lmin_ref[...] = jnp.full(lmin_ref.shape, jnp.inf, F32)

    def fast_tile(i, next_ksq):
        off = pl.multiple_of(i * tq, tq)
        next_ksq = jnp.maximum(next_ksq, max_sq_norm(knext_ref[pl.ds(off, tq), :]))
        qt = qt_ref[:, pl.ds(off, tq)]
        qf = qt.astype(F32)
        shift = jnp.sqrt(jnp.sum(qf * qf, axis=0, keepdims=True)) * k_norm
        s = jnp.dot(k_ref[...], qt, preferred_element_type=F32)
        p = jnp.exp2(s - shift)
        l = jnp.sum(p, axis=0, keepdims=True)
        ot = jnp.dot(vt_ref[...], p.astype(BF16), preferred_element_type=F32)
        o_ref[pl.ds(off, tq), :] = (ot / l).T.astype(o_ref.dtype)
        lmin_ref[...] = jnp.minimum(lmin_ref[...], l)
        return next_ksq

    ksq_ref[...] = lax.fori_loop(0, nq, fast_tile, jnp.zeros((1, 1), F32), unroll=4)
    trusted = jnp.min(lmin_ref[...]) >= MIN_SOFTMAX_MASS

    @pl.when(jnp.logical_not(trusted))
    def _():
        _attention_exact(qt_ref, k_ref, vt_ref, o_ref, s_a, m_a, s_b, m_b)


def _attention_exact(qt_ref, k_ref, vt_ref, o_ref, s_a, m_a, s_b, m_b):
    tq = s_a.shape[1]
    nq = qt_ref.shape[1] // tq

    def scores(i, s_ref, m_ref):
        off = pl.multiple_of(i * tq, tq)
        s = jnp.dot(k_ref[...], qt_ref[:, pl.ds(off, tq)], preferred_element_type=F32)
        s_ref[...] = s
        m_ref[...] = jnp.max(s, axis=0, keepdims=True)

    def finish(i, s_ref, m_ref):
        off = pl.multiple_of(i * tq, tq)
        p = jnp.exp2(s_ref[...] - m_ref[...])
        l = jnp.sum(p, axis=0, keepdims=True)
        ot = jnp.dot(vt_ref[...], p.astype(BF16), preferred_element_type=F32)
        o_ref[pl.ds(off, tq), :] = (ot / l).T.astype(o_ref.dtype)

    scores(0, s_a, m_a)

    def step(i, carry):
        @pl.when(i % 2 == 1)
        def _():
            scores(i, s_b, m_b)
            finish(i - 1, s_a, m_a)

        @pl.when(i % 2 == 0)
        def _():
            scores(i, s_a, m_a)
            finish(i - 1, s_b, m_b)

        return carry

    lax.fori_loop(1, nq, step, 0)
    finish(nq - 1, s_b, m_b)


def _attention(qt, k, vt, batch, seq, later_weights):
    tq = 256
    tq_fast = 512
    assert seq % (2 * tq) == 0 and seq % tq_fast == 0
    steps = batch * N_HEADS
    slabs = [w.shape[0] // steps for w in later_weights]
    assert all(s % BF16_SUBLANES == 0 and s * steps == w.shape[0]
               for s, w in zip(slabs, later_weights))
    step = lambda b, h: (b * N_HEADS + h, 0)
    cast_specs = [pl.BlockSpec((s, w.shape[1]), step) for s, w in zip(slabs, later_weights)]

    def next_head(b, h):
        t = jnp.minimum(b * N_HEADS + h + 1, steps - 1)
        return t // N_HEADS, t % N_HEADS, 0, 0

    outs = pl.pallas_call(
        functools.partial(_attention_kernel, n_cast=len(later_weights)),
        grid=(batch, N_HEADS),
        in_specs=[pl.BlockSpec((None, QK_HEAD, seq), lambda b, h: (b, h, 0)),
                  pl.BlockSpec((None, None, seq, QK_HEAD), lambda b, h: (b, h, 0, 0)),
                  pl.BlockSpec((None, None, seq, QK_HEAD), next_head),
                  pl.BlockSpec((None, V_HEAD, seq), lambda b, h: (b, h, 0))] + cast_specs,
        out_specs=[pl.BlockSpec((None, seq, V_HEAD), lambda b, h: (b, 0, h))] + cast_specs,
        out_shape=[jax.ShapeDtypeStruct((batch, seq, MLA_WIDTH), BF16)]
                  + [jax.ShapeDtypeStruct(w.shape, BF16) for w in later_weights],
        scratch_shapes=[pltpu.VMEM((seq, tq), F32), pltpu.VMEM((1, tq), F32),
                        pltpu.VMEM((seq, tq), F32), pltpu.VMEM((1, tq), F32),
                        pltpu.VMEM((1, tq_fast), F32), pltpu.VMEM((1, 1), F32)],
        compiler_params=_params(("arbitrary", "arbitrary")),
        name="attention",
    )(qt, k, k, vt, *later_weights)
    return outs[0], outs[1:]


def _merge_out_kernel(x_ref, gate_ref, attn_ref, gm_ref, gp_ref, vp_ref, vprev_ref, vnext_ref,
                      mm_ref, mp_ref, pw_ref, ps_ref, wop_ref, wom_ref, wout_ref, fg_ref, o_ref,
                      *, seq, final_norm):
    tm = x_ref.shape[0]
    t0 = (pl.program_id(0) % (seq // tm)) * tm

    gated = attn_ref[...] * gm_ref[...]

    cur = vp_ref[...].astype(F32)
    prev = jnp.where(t0 > 0, vprev_ref[...].astype(F32), 0.0)
    nxt = jnp.where(t0 + tm < seq, vnext_ref[...].astype(F32), 0.0)
    ext = jnp.concatenate([prev, cur, nxt], axis=0)
    n_ext = tm + 2 * POOL_HALO
    tok = t0 + lax.broadcasted_iota(jnp.int32, (tm, 1), 0)
    mixed = []
    p_mla = []
    mla_cols = wom_ref.shape[1] // POOL_GROUPS
    for g, w in enumerate(POOL_WINDOWS):
        p_mla.append(jnp.dot(gated, wom_ref[:, g * mla_cols:(g + 1) * mla_cols],
                             preferred_element_type=F32))
        acc = ext[:, g * POOL_GROUP_DIM:(g + 1) * POOL_GROUP_DIM]
        acc = acc + pltpu.roll(acc, 1, axis=0)
        half = 1
        while 2 * half < w:
            acc = pltpu.roll(acc, half, axis=0) + pltpu.roll(acc, n_ext - half, axis=0)
            half *= 2
        wsum = acc[POOL_HALO:POOL_HALO + tm]
        count = (jnp.minimum(tok + w // 2, seq) - jnp.maximum(tok - w // 2, 0)).astype(F32)
        pooled = wsum / count - cur[:, g * POOL_GROUP_DIM:(g + 1) * POOL_GROUP_DIM]
        mixed.append(jnp.dot(pooled.astype(BF16), pw_ref[g], preferred_element_type=F32))
    mixed = jnp.concatenate(mixed, axis=1)
    p_mla = jnp.concatenate(p_mla, axis=1)
    u = (mixed * ps_ref[...] * gp_ref[...].astype(F32)).astype(BF16)
    p_pool = jnp.dot(u, wop_ref[...], preferred_element_type=F32)

    y = mm_ref[...].astype(F32) * p_mla + mp_ref[...].astype(F32) * p_pool
    r = jnp.dot(y.astype(BF16), wout_ref[...], preferred_element_type=F32)
    xo = x_ref[...] + gate_ref[...] * r
    if final_norm:
        xo = xo * lax.rsqrt(jnp.mean(xo * xo, axis=-1, keepdims=True) + EPS) * fg_ref[...]
    o_ref[...] = xo


def _merge_out(xt, mod4, attn, zbig, pool_w, pool_scale, w_o_pool, w_o_mla, w_out, final_g,
               seq, final_norm):
    T, D = xt.shape
    tm = 256
    per_b = seq // tm
    halo_per_tile = tm // POOL_HALO
    n_halo = T // POOL_HALO
    vp_blk = MLA_WIDTH // POOL_WIDTH
    gp_blk = vp_blk + 1
    mm_blk = (MLA_WIDTH + 2 * POOL_WIDTH) // D
    resident = functools.partial(pl.BlockSpec, pipeline_mode=pl.Buffered(1))
    kern = functools.partial(_merge_out_kernel, seq=seq, final_norm=final_norm)
    return pl.pallas_call(
        kern,
        grid=(T // tm,),
        in_specs=[pl.BlockSpec((tm, D), lambda i: (i, 0)),
                  pl.BlockSpec((None, None, 1, D), lambda i: (i // per_b, 2, 0, 0)),
                  pl.BlockSpec((tm, MLA_WIDTH), lambda i: (i, 0)),
                  pl.BlockSpec((tm, MLA_WIDTH), lambda i: (i, 0)),
                  pl.BlockSpec((tm, POOL_WIDTH), lambda i: (i, gp_blk)),
                  pl.BlockSpec((tm, POOL_WIDTH), lambda i: (i, vp_blk)),
                  pl.BlockSpec((POOL_HALO, POOL_WIDTH),
                               lambda i: (jnp.maximum(i * halo_per_tile - 1, 0), vp_blk)),
                  pl.BlockSpec((POOL_HALO, POOL_WIDTH),
                               lambda i: (jnp.minimum((i + 1) * halo_per_tile, n_halo - 1), vp_blk)),
                  pl.BlockSpec((tm, D), lambda i: (i, mm_blk)),
                  pl.BlockSpec((tm, D), lambda i: (i, mm_blk + 1)),
                  resident(pool_w.shape, lambda i: (0, 0, 0)),
                  resident((1, POOL_WIDTH), lambda i: (0, 0)),
                  resident(w_o_pool.shape, lambda i: (0, 0)),
                  resident(w_o_mla.shape, lambda i: (0, 0)),
                  resident(w_out.shape, lambda i: (0, 0)),
                  resident((1, D), lambda i: (0, 0))],
        out_specs=pl.BlockSpec((tm, D), lambda i: (i, 0)),
        out_shape=jax.ShapeDtypeStruct((T, D), F32),
        compiler_params=_params(("arbitrary",)),
        name="merge_out",
    )(xt, mod4, attn, zbig, zbig, zbig, zbig, zbig, zbig, zbig,
      pool_w, pool_scale.reshape(1, POOL_WIDTH), w_o_pool, w_o_mla, w_out, final_g.reshape(1, D))


def kernel(x, c, positions, ada_w, ada_b, norm_g, w_in, q_norm_g, w_uq, kv_norm_g, w_ukv, w_o_mla,
           pool_w, pool_scale, w_o_pool, w_out, final_g):
    B, S, D = x.shape
    depth = ada_w.shape[0]
    inv_freq = 1.0 / (ROPE_THETA ** (jnp.arange(0, QK_ROPE, 2, dtype=F32) / QK_ROPE))
    inv_signed = jnp.concatenate([-inv_freq, inv_freq])
    posr = positions.reshape(B, 1, S)
    q_scale = QK_HEAD ** -0.5 * math.log2(math.e)

    xt = x.reshape(B * S, D)
    for l in range(depth):
        mod4 = _adaln(c, ada_w[l], ada_b[l]).reshape(B, 3, 1, D)
        w_in_t = w_in[l].T
        h, zs = _norm_proj(xt, mod4, norm_g[l], w_in_t, S)
        zbig = _gate_proj(h, w_in_t)
        qt, k, vt = _mla_prep(zs, posr, inv_signed, q_norm_g[l], kv_norm_g[l],
                              w_uq[l], w_ukv[l], B, S, q_scale)
        attn, (pw, wop, wom, wout) = _attention(
            qt, k, vt, B, S,
            (pool_w[l].reshape(POOL_WIDTH, POOL_GROUP_DIM), w_o_pool[l], w_o_mla[l], w_out[l]))
        xt = _merge_out(xt, mod4, attn.reshape(B * S, MLA_WIDTH), zbig,
                        pw.reshape(POOL_GROUPS, POOL_GROUP_DIM, POOL_GROUP_DIM), pool_scale[l],
                        wop, wom, wout, final_g, S, final_norm=(l == depth - 1))
    return xt.reshape(B, S, D)
```

```python
import functools
import math

import jax
import jax.numpy as jnp
from jax import lax
from jax.experimental import pallas as pl
from jax.experimental.pallas import tpu as pltpu

EPS = 1e-6
N_HEADS = 16
QK_NOPE = 128
QK_ROPE = 64
QK_HEAD = QK_NOPE + QK_ROPE
V_HEAD = 128
Q_LORA = 512
KV_LORA = 512
MLA_WIDTH = N_HEADS * V_HEAD
ROPE_THETA = 10000.0
POOL_WINDOWS = (2, 4, 8, 16)
POOL_GROUPS = len(POOL_WINDOWS)
POOL_GROUP_DIM = 256
POOL_WIDTH = POOL_GROUPS * POOL_GROUP_DIM
POOL_HALO = 16
SMALL_WIDTH = Q_LORA + KV_LORA + 2 * QK_ROPE

V7X_VMEM_LIMIT = 56 * 1024 * 1024
BF16_SUBLANES = 16
F32_SUBLANES = 8
MIN_SOFTMAX_MASS = 2.0 ** -60

F32 = jnp.float32
BF16 = jnp.bfloat16
NT_DIMS = (((1,), (1,)), ((), ()))


def _sigmoid(v):
    return 0.5 * jnp.tanh(0.5 * v) + 0.5


def _params(semantics, vmem=V7X_VMEM_LIMIT, flags=None):
    return pltpu.CompilerParams(dimension_semantics=semantics, vmem_limit_bytes=vmem, flags=flags)


def _adaln_kernel(ct_ref, w_ref, b_ref, o_ref):
    w = w_ref[...]
    for b in range(ct_ref.shape[1]):
        cb = ct_ref[:, b:b + 1]
        act = cb * _sigmoid(cb)
        o_ref[b:b + 1, :] = jnp.sum(w * act, axis=0, keepdims=True) + b_ref[...]


def _adaln(c, w, bias):
    B, D = c.shape
    n = w.shape[1]
    tn = 1024
    return pl.pallas_call(
        _adaln_kernel,
        grid=(n // tn,),
        in_specs=[pl.BlockSpec((D, B), lambda j: (0, 0)),
                  pl.BlockSpec((D, tn), lambda j: (0, j)),
                  pl.BlockSpec((1, tn), lambda j: (0, j))],
        out_specs=pl.BlockSpec((B, tn), lambda j: (0, j)),
        out_shape=jax.ShapeDtypeStruct((B, n), F32),
        compiler_params=_params(("arbitrary",)),
        name="adaln",
    )(c.T, w, bias.reshape(1, n))


NORM_ROW_CHUNKS = 4


def _norm_proj_kernel(x_ref, shift_ref, scale_ref, g_ref, ws_ref, h_ref, zs_ref, ws_bf16):
    @pl.when(pl.program_id(0) == 0)
    def _():
        ws_bf16[...] = ws_ref[...].T.astype(BF16)

    rows = x_ref.shape[0] // NORM_ROW_CHUNKS
    for c in range(NORM_ROW_CHUNKS):
        sl = slice(c * rows, (c + 1) * rows)
        x = x_ref[sl, :]
        y = x * lax.rsqrt(jnp.mean(x * x, axis=-1, keepdims=True) + EPS) * g_ref[...]
        h = (y * (1.0 + scale_ref[...]) + shift_ref[...]).astype(BF16)
        h_ref[sl, :] = h
        zs_ref[sl, :] = jnp.dot(h, ws_bf16[...], preferred_element_type=F32)


def _norm_proj(xt, mod4, norm_g, w_in_t, seq):
    T, D = xt.shape
    tm = 512
    per_b = seq // tm
    return pl.pallas_call(
        _norm_proj_kernel,
        grid=(T // tm,),
        in_specs=[pl.BlockSpec((tm, D), lambda i: (i, 0)),
                  pl.BlockSpec((None, None, 1, D), lambda i: (i // per_b, 0, 0, 0)),
                  pl.BlockSpec((None, None, 1, D), lambda i: (i // per_b, 1, 0, 0)),
                  pl.BlockSpec((1, D), lambda i: (0, 0)),
                  pl.BlockSpec((SMALL_WIDTH, D), lambda i: (0, 0), pipeline_mode=pl.Buffered(1))],
        out_specs=[pl.BlockSpec((tm, D), lambda i: (i, 0)),
                   pl.BlockSpec((tm, SMALL_WIDTH), lambda i: (i, 0))],
        out_shape=[jax.ShapeDtypeStruct((T, D), BF16),
                   jax.ShapeDtypeStruct((T, SMALL_WIDTH), F32)],
        scratch_shapes=[pltpu.VMEM((D, SMALL_WIDTH), BF16)],
        compiler_params=_params(("arbitrary",)),
        name="norm_proj",
    )(xt, mod4, mod4, norm_g.reshape(1, D), w_in_t)


_ACTIVATIONS = {
    "silu": lambda a: a * _sigmoid(a),
    "linear": lambda a: a,
    "sigmoid": _sigmoid,
}


def _gate_proj_kernel(h_ref, w_hbm, o_ref, w_f32, w_bf16, sem, *, tile_kinds, first_row):
    j = pl.program_id(0)
    tn = w_bf16.shape[0]

    def window_copy(tile):
        rows = pl.ds(pl.multiple_of(first_row + tile * tn, F32_SUBLANES), tn)
        return pltpu.make_async_copy(w_hbm.at[rows, :], w_f32, sem)

    @pl.when(pl.program_id(1) == 0)
    def _():
        @pl.when(j == 0)
        def _():
            window_copy(0).start()

        window_copy(j).wait()
        w_bf16[...] = w_f32[...].astype(BF16)

        @pl.when(j + 1 < pl.num_programs(0))
        def _():
            window_copy(j + 1).start()

    for kinds in sorted(set(tile_kinds)):
        tiles = [t for t, k in enumerate(tile_kinds) if k == kinds]
        cond = functools.reduce(jnp.logical_or, [j == t for t in tiles])

        @pl.when(cond)
        def _(kinds=kinds):
            acc = lax.dot_general(h_ref[...], w_bf16[...], NT_DIMS, preferred_element_type=F32)
            width = tn // len(kinds)
            for s, kind in enumerate(kinds):
                cols = slice(s * width, (s + 1) * width)
                o_ref[:, cols] = _ACTIVATIONS[kind](acc[:, cols]).astype(o_ref.dtype)


def _gate_proj(h, w_in_t):
    T, D = h.shape
    tm, tn = 1024, 2048
    first_row = Q_LORA + KV_LORA + QK_ROPE
    n = w_in_t.shape[0] - first_row
    assert n % tn == 0 and first_row % F32_SUBLANES == 0 and tn % F32_SUBLANES == 0
    group_kinds = (("silu",) * (MLA_WIDTH // POOL_WIDTH) + ("linear", "silu")
                   + ("sigmoid",) * (2 * D // POOL_WIDTH))
    per_tile = tn // POOL_WIDTH
    tile_kinds = tuple(group_kinds[t * per_tile:(t + 1) * per_tile] for t in range(n // tn))
    kern =functools.partial(_gate_proj_kernel, tile_kinds=tile_kinds, first_row=first_row)
    return pl.pallas_call(
        kern,
        grid=(n // tn, T // tm),
        in_specs=[pl.BlockSpec((tm, D), lambda j, i: (i, 0)),
                  pl.BlockSpec(memory_space=pl.ANY)],
        out_specs=pl.BlockSpec((tm, tn), lambda j, i: (i, j)),
        out_shape=jax.ShapeDtypeStruct((T, n), BF16),
        scratch_shapes=[pltpu.VMEM((tn, D), F32), pltpu.VMEM((tn, D), BF16),
                        pltpu.SemaphoreType.DMA(())],
        compiler_params=_params(("arbitrary", "arbitrary")),
        name="gate_proj",
    )(h, w_in_t)


def _mla_prep_kernel(zs_ref, posr_ref, invc_ref, qg_ref, kvg_ref, wuq_ref, wukv_ref,
                     qt_ref, k_ref, vt_ref, wq_ref, wk_ref, wv_ref, *, q_scale):
    @pl.when(pl.program_id(0) == 0)
    def _():
        wq_ref[...] = wuq_ref[...].T.astype(BF16)
        kv_head = QK_NOPE + V_HEAD
        for h in range(N_HEADS):
            wk_ref[:, h * QK_NOPE:(h + 1) * QK_NOPE] = (
                wukv_ref[:, h * kv_head:h * kv_head + QK_NOPE].astype(BF16))
            wv_ref[h * V_HEAD:(h + 1) * V_HEAD, :] = (
                wukv_ref[:, h * kv_head + QK_NOPE:(h + 1) * kv_head].T.astype(BF16))

    def rms(v, g):
        return (v * lax.rsqrt(jnp.mean(v * v, axis=-1, keepdims=True) + EPS) * g).astype(BF16)

    cqn = rms(zs_ref[:, 0:Q_LORA], qg_ref[...])
    ckvn = rms(zs_ref[:, Q_LORA:Q_LORA + KV_LORA], kvg_ref[...])
    kr = zs_ref[:, Q_LORA + KV_LORA:Q_LORA + KV_LORA + QK_ROPE]
    kr_sw = jnp.concatenate([kr[:, QK_ROPE // 2:], kr[:, :QK_ROPE // 2]], axis=1)

    ang_t = invc_ref[...] * posr_ref[...].astype(F32)
    cos_t, sin_t = jnp.cos(ang_t), jnp.sin(ang_t)
    cos, sin = cos_t.T, sin_t.T

    qf = lax.dot_general(wq_ref[...], cqn, NT_DIMS, preferred_element_type=F32)
    half = QK_ROPE // 2
    for h in range(N_HEADS):
        r0 = h * QK_HEAD + QK_NOPE
        qt_ref[h * QK_HEAD:r0, :] = (qf[h * QK_HEAD:r0] * q_scale).astype(BF16)
        rope = qf[r0:r0 + QK_ROPE]
        rope_sw = jnp.concatenate([rope[half:], rope[:half]], axis=0)
        qt_ref[r0:r0 + QK_ROPE, :] = ((rope * cos_t + rope_sw * sin_t) * q_scale).astype(BF16)

    kn = jnp.dot(ckvn, wk_ref[...], preferred_element_type=F32)
    k_rot = (kr * cos + kr_sw * sin).astype(BF16)
    for h in range(N_HEADS):
        k_ref[h, :, 0:QK_NOPE] = kn[:, h * QK_NOPE:(h + 1) * QK_NOPE].astype(BF16)
        k_ref[h, :, QK_NOPE:QK_HEAD] = k_rot

    vt_ref[...] = lax.dot_general(wv_ref[...], ckvn, NT_DIMS,
                                  preferred_element_type=F32).astype(BF16)


def _mla_prep(zs, posr, inv_signed, q_norm_g, kv_norm_g, w_uq, w_ukv, batch, seq, q_scale):
    tm = 512
    per_b = seq // tm
    const = lambda i: (0, 0)
    resident = functools.partial(pl.BlockSpec, pipeline_mode=pl.Buffered(1))
    kern = functools.partial(_mla_prep_kernel, q_scale=q_scale)
    return pl.pallas_call(
        kern,
        grid=(batch * per_b,),
        in_specs=[pl.BlockSpec((tm, SMALL_WIDTH), lambda i: (i, 0)),
                  pl.BlockSpec((None, 1, tm), lambda i: (i // per_b, 0, i % per_b)),
                  pl.BlockSpec((QK_ROPE, 1), const),
                  pl.BlockSpec((1, Q_LORA), const),
                  pl.BlockSpec((1, KV_LORA), const),
                  resident(w_uq.shape, const),
                  resident(w_ukv.shape, const)],
        out_specs=[pl.BlockSpec((None, N_HEADS * QK_HEAD, tm), lambda i: (i // per_b, 0, i % per_b)),
                   pl.BlockSpec((None, N_HEADS, tm, QK_HEAD), lambda i: (i // per_b, 0, i % per_b, 0)),
                   pl.BlockSpec((None, MLA_WIDTH, tm), lambda i: (i // per_b, 0, i % per_b))],
        out_shape=[jax.ShapeDtypeStruct((batch, N_HEADS * QK_HEAD, seq), BF16),
                   jax.ShapeDtypeStruct((batch, N_HEADS, seq, QK_HEAD), BF16),
                   jax.ShapeDtypeStruct((batch, MLA_WIDTH, seq), BF16)],
        scratch_shapes=[pltpu.VMEM((N_HEADS * QK_HEAD, Q_LORA), BF16),
                        pltpu.VMEM((KV_LORA, N_HEADS * QK_NOPE), BF16),
                        pltpu.VMEM((MLA_WIDTH, KV_LORA), BF16)],
        compiler_params=_params(("arbitrary",)),
        name="mla_prep",
    )(zs, posr, inv_signed.reshape(QK_ROPE, 1),
      q_norm_g.reshape(1, Q_LORA), kv_norm_g.reshape(1, KV_LORA), w_uq, w_ukv)


def _attention_kernel(qt_ref, k_ref, knext_ref, vt_ref, *refs, n_cast):
    cast_in, (o_ref, *cast_out) = refs[:n_cast], refs[n_cast:2 * n_cast + 1]
    s_a, m_a, s_b, m_b, lmin_ref, ksq_ref = refs[2 * n_cast + 1:]
    for w_in, w_out in zip(cast_in, cast_out):
        w_out[...] = w_in[...].astype(w_out.dtype)

    def max_sq_norm(rows):
        rf = rows.astype(F32)
        return jnp.max(jnp.sum(rf * rf, axis=1, keepdims=True), axis=0, keepdims=True)

    tq = lmin_ref.shape[1]
    nq = qt_ref.shape[1] // tq

    @pl.when(jnp.logical_and(pl.program_id(0) == 0, pl.program_id(1) == 0))
    def _():
        ksq_ref[...] = max_sq_norm(k_ref[...])

    k_norm = jnp.sqrt(ksq_ref[...])
    lmin_ref[...] = jnp.full(lmin_ref.shape, jnp.inf, F32)

    def fast_tile(i, next_ksq):
        off = pl.multiple_of(i * tq, tq)
        next_ksq = jnp.maximum(next_ksq, max_sq_norm(knext_ref[pl.ds(off, tq), :]))
        qt = qt_ref[:, pl.ds(off, tq)]
        qf = qt.astype(F32)
        shift = jnp.sqrt(jnp.sum(qf * qf, axis=0, keepdims=True)) * k_norm
        s = jnp.dot(k_ref[...], qt, preferred_element_type=F32)
        p = jnp.exp2(s - shift)
        l = jnp.sum(p, axis=0, keepdims=True)
        ot = jnp.dot(vt_ref[...], p.astype(BF16), preferred_element_type=F32)
        o_ref[pl.ds(off, tq), :] = (ot / l).T.astype(o_ref.dtype)
        lmin_ref[...] = jnp.minimum(lmin_ref[...], l)
        return next_ksq

    ksq_ref[...] = lax.fori_loop(0, nq, fast_tile, jnp.zeros((1, 1), F32), unroll=True)
    trusted = jnp.min(lmin_ref[...]) >= MIN_SOFTMAX_MASS

    @pl.when(jnp.logical_not(trusted))
    def _():
        _attention_exact(qt_ref, k_ref, vt_ref, o_ref, s_a, m_a, s_b, m_b)


def _attention_exact(qt_ref, k_ref, vt_ref, o_ref, s_a, m_a, s_b, m_b):
    tq = s_a.shape[1]
    nq = qt_ref.shape[1] // tq

    def scores(i, s_ref, m_ref):
        off = pl.multiple_of(i * tq, tq)
        s = jnp.dot(k_ref[...], qt_ref[:, pl.ds(off, tq)], preferred_element_type=F32)
        s_ref[...] = s
        m_ref[...] = jnp.max(s, axis=0, keepdims=True)

    def finish(i, s_ref, m_ref):
        off = pl.multiple_of(i * tq, tq)
        p = jnp.exp2(s_ref[...] - m_ref[...])
        l = jnp.sum(p, axis=0, keepdims=True)
        ot = jnp.dot(vt_ref[...], p.astype(BF16), preferred_element_type=F32)
        o_ref[pl.ds(off, tq), :] = (ot / l).T.astype(o_ref.dtype)

    scores(0, s_a, m_a)

    def step(i, carry):
        @pl.when(i % 2 == 1)
        def _():
            scores(i, s_b, m_b)
            finish(i - 1, s_a, m_a)

        @pl.when(i % 2 == 0)
        def _():
            scores(i, s_a, m_a)
            finish(i - 1, s_b, m_b)

        return carry

    lax.fori_loop(1, nq, step, 0)
    finish(nq - 1, s_b, m_b)


def _attention(qt, k, vt, batch, seq, later_weights):
    tq = 256
    tq_fast = 512
    assert seq % (2 * tq) == 0 and seq % tq_fast == 0
    steps = batch * N_HEADS
    slabs = [w.shape[0] // steps for w in later_weights]
    assert all(s % BF16_SUBLANES == 0 and s * steps == w.shape[0]
               for s, w in zip(slabs, later_weights))
    step = lambda b, h: (b * N_HEADS + h, 0)
    cast_specs = [pl.BlockSpec((s, w.shape[1]), step) for s, w in zip(slabs, later_weights)]

    def next_head(b, h):
        t = jnp.minimum(b * N_HEADS + h + 1, steps - 1)
        return t // N_HEADS, t % N_HEADS, 0, 0

    outs = pl.pallas_call(
        functools.partial(_attention_kernel, n_cast=len(later_weights)),
        grid=(batch, N_HEADS),
        in_specs=[pl.BlockSpec((None, QK_HEAD, seq), lambda b, h: (b, h, 0)),
                  pl.BlockSpec((None, None, seq, QK_HEAD), lambda b, h: (b, h, 0, 0)),
                  pl.BlockSpec((None, None, seq, QK_HEAD), next_head),
                  pl.BlockSpec((None, V_HEAD, seq), lambda b, h: (b, h, 0))] + cast_specs,
        out_specs=[pl.BlockSpec((None, seq, V_HEAD), lambda b, h: (b, 0, h))] + cast_specs,
        out_shape=[jax.ShapeDtypeStruct((batch, seq, MLA_WIDTH), BF16)]
                  + [jax.ShapeDtypeStruct(w.shape, BF16) for w in later_weights],
        scratch_shapes=[pltpu.VMEM((seq, tq), F32), pltpu.VMEM((1, tq), F32),
                        pltpu.VMEM((seq, tq), F32), pltpu.VMEM((1, tq), F32),
                        pltpu.VMEM((1, tq_fast), F32), pltpu.VMEM((1, 1), F32)],
        compiler_params=_params(("arbitrary", "arbitrary")),
        name="attention",
    )(qt, k, k, vt, *later_weights)
    return outs[0], outs[1:]


def _merge_out_kernel(x_ref, gate_ref, attn_ref, gm_ref, gp_ref, vp_ref, vprev_ref, vnext_ref,
                      mm_ref, mp_ref, pw_ref, ps_ref, wop_ref, wom_ref, wout_ref, fg_ref, o_ref,
                      *, seq, final_norm):
    tm = x_ref.shape[0]
    t0 = (pl.program_id(0) % (seq // tm)) * tm

    gated = attn_ref[...] * gm_ref[...]

    cur = vp_ref[...].astype(F32)
    prev = jnp.where(t0 > 0, vprev_ref[...].astype(F32), 0.0)
    nxt = jnp.where(t0 + tm < seq, vnext_ref[...].astype(F32), 0.0)
    ext = jnp.concatenate([prev, cur, nxt], axis=0)
    n_ext = tm + 2 * POOL_HALO
    tok = t0 + lax.broadcasted_iota(jnp.int32, (tm, 1), 0)
    mixed = []
    p_mla = []
    mla_cols = wom_ref.shape[1] // POOL_GROUPS
    for g, w in enumerate(POOL_WINDOWS):
        p_mla.append(jnp.dot(gated, wom_ref[:, g * mla_cols:(g + 1) * mla_cols],
                             preferred_element_type=F32))
        acc = ext[:, g * POOL_GROUP_DIM:(g + 1) * POOL_GROUP_DIM]
        acc = acc + pltpu.roll(acc, 1, axis=0)
        half = 1
        while 2 * half < w:
            acc = pltpu.roll(acc, half, axis=0) + pltpu.roll(acc, n_ext - half, axis=0)
            half *= 2
        wsum = acc[POOL_HALO:POOL_HALO + tm]
        count = (jnp.minimum(tok + w // 2, seq) - jnp.maximum(tok - w // 2, 0)).astype(F32)
        pooled = wsum / count - cur[:, g * POOL_GROUP_DIM:(g + 1) * POOL_GROUP_DIM]
        mixed.append(jnp.dot(pooled.astype(BF16), pw_ref[g], preferred_element_type=F32))
    mixed = jnp.concatenate(mixed, axis=1)
    p_mla = jnp.concatenate(p_mla, axis=1)
    u = (mixed * ps_ref[...] * gp_ref[...].astype(F32)).astype(BF16)
    p_pool = jnp.dot(u, wop_ref[...], preferred_element_type=F32)

    y = mm_ref[...].astype(F32) * p_mla + mp_ref[...].astype(F32) * p_pool
    r = jnp.dot(y.astype(BF16), wout_ref[...], preferred_element_type=F32)
    xo = x_ref[...] + gate_ref[...] * r
    if final_norm:
        xo = xo * lax.rsqrt(jnp.mean(xo * xo, axis=-1, keepdims=True) + EPS) * fg_ref[...]
    o_ref[...] = xo


def _merge_out(xt, mod4, attn, zbig, pool_w, pool_scale, w_o_pool, w_o_mla, w_out, final_g,
               seq, final_norm):
    T, D = xt.shape
    tm = 256
    per_b = seq // tm
    halo_per_tile = tm // POOL_HALO
    n_halo = T // POOL_HALO
    vp_blk = MLA_WIDTH // POOL_WIDTH
    gp_blk = vp_blk + 1
    mm_blk = (MLA_WIDTH + 2 * POOL_WIDTH) // D
    resident = functools.partial(pl.BlockSpec, pipeline_mode=pl.Buffered(1))
    kern = functools.partial(_merge_out_kernel, seq=seq, final_norm=final_norm)
    return pl.pallas_call(
        kern,
        grid=(T // tm,),
        in_specs=[pl.BlockSpec((tm, D), lambda i: (i, 0)),
                  pl.BlockSpec((None, None, 1, D), lambda i: (i // per_b, 2, 0, 0)),
                  pl.BlockSpec((tm, MLA_WIDTH), lambda i: (i, 0)),
                  pl.BlockSpec((tm, MLA_WIDTH), lambda i: (i, 0)),
                  pl.BlockSpec((tm, POOL_WIDTH), lambda i: (i, gp_blk)),
                  pl.BlockSpec((tm, POOL_WIDTH), lambda i: (i, vp_blk)),
                  pl.BlockSpec((POOL_HALO, POOL_WIDTH),
                               lambda i: (jnp.maximum(i * halo_per_tile - 1, 0), vp_blk)),
                  pl.BlockSpec((POOL_HALO, POOL_WIDTH),
                               lambda i: (jnp.minimum((i + 1) * halo_per_tile, n_halo - 1), vp_blk)),
                  pl.BlockSpec((tm, D), lambda i: (i, mm_blk)),
                  pl.BlockSpec((tm, D), lambda i: (i, mm_blk + 1)),
                  resident(pool_w.shape, lambda i: (0, 0, 0)),
                  resident((1, POOL_WIDTH), lambda i: (0, 0)),
                  resident(w_o_pool.shape, lambda i: (0, 0)),
                  resident(w_o_mla.shape, lambda i: (0, 0)),
                  resident(w_out.shape, lambda i: (0, 0)),
                  resident((1, D), lambda i: (0, 0))],
        out_specs=pl.BlockSpec((tm, D), lambda i: (i, 0)),
        out_shape=jax.ShapeDtypeStruct((T, D), F32),
        compiler_params=_params(("arbitrary",)),
        name="merge_out",
    )(xt, mod4, attn, zbig, zbig, zbig, zbig, zbig, zbig, zbig,
      pool_w, pool_scale.reshape(1, POOL_WIDTH), w_o_pool, w_o_mla, w_out, final_g.reshape(1, D))


def kernel(x, c, positions, ada_w, ada_b, norm_g, w_in, q_norm_g, w_uq, kv_norm_g, w_ukv, w_o_mla,
           pool_w, pool_scale, w_o_pool, w_out, final_g):
    B, S, D = x.shape
    depth = ada_w.shape[0]
    inv_freq = 1.0 / (ROPE_THETA ** (jnp.arange(0, QK_ROPE, 2, dtype=F32) / QK_ROPE))
    inv_signed = jnp.concatenate([-inv_freq, inv_freq])
    posr = positions.reshape(B, 1, S)
    q_scale = QK_HEAD ** -0.5 * math.log2(math.e)

    xt = x.reshape(B * S, D)
    for l in range(depth):
        mod4 = _adaln(c, ada_w[l], ada_b[l]).reshape(B, 3, 1, D)
        w_in_t = w_in[l].T
        h, zs = _norm_proj(xt, mod4, norm_g[l], w_in_t, S)
        zbig = _gate_proj(h, w_in_t)
        qt, k, vt = _mla_prep(zs, posr, inv_signed, q_norm_g[l], kv_norm_g[l],
                              w_uq[l], w_ukv[l], B, S, q_scale)
        attn, (pw, wop, wom, wout) = _attention(
            qt, k, vt, B, S,
            (pool_w[l].reshape(POOL_WIDTH, POOL_GROUP_DIM), w_o_pool[l], w_o_mla[l], w_out[l]))
        xt = _merge_out(xt, mod4, attn.reshape(B * S, MLA_WIDTH), zbig,
                        pw.reshape(POOL_GROUPS, POOL_GROUP_DIM, POOL_GROUP_DIM), pool_scale[l],
                        wop, wom, wout, final_g, S, final_norm=(l == depth - 1))
    return xt.reshape(B, S, D)
```

```python
import functools
import math

import jax
import jax.numpy as jnp
from jax import lax
from jax.experimental import pallas as pl
from jax.experimental.pallas import tpu as pltpu

EPS = 1e-6
N_HEADS = 16
QK_NOPE = 128
QK_ROPE = 64
QK_HEAD = QK_NOPE + QK_ROPE
V_HEAD = 128
Q_LORA = 512
KV_LORA = 512
MLA_WIDTH = N_HEADS * V_HEAD
ROPE_THETA = 10000.0
POOL_WINDOWS = (2, 4, 8, 16)
POOL_GROUPS = len(POOL_WINDOWS)
POOL_GROUP_DIM = 256
POOL_WIDTH = POOL_GROUPS * POOL_GROUP_DIM
POOL_HALO = 16
SMALL_WIDTH = Q_LORA + KV_LORA + 2 * QK_ROPE

V7X_VMEM_LIMIT = 56 * 1024 * 1024
ADALN_TN = 1024
NORM_TM = 512
NORM_ROW_CHUNKS = 4
GATE_TM, GATE_TN = 1024, 2048
PREP_TM = 512
ATTN_TQ_EXACT = 256
ATTN_TQ_FAST = 512
MERGE_TM = 256
BF16_SUBLANES = 16
F32_SUBLANES = 8
MIN_SOFTMAX_MASS = 2.0 ** -60

F32 = jnp.float32
BF16 = jnp.bfloat16
NT_DIMS = (((1,), (1,)), ((), ()))


def _sigmoid(v):
    return 0.5 * jnp.tanh(0.5 * v) + 0.5


def _params(semantics):
    return pltpu.CompilerParams(dimension_semantics=semantics, vmem_limit_bytes=V7X_VMEM_LIMIT)


def _adaln_kernel(ct_ref, w_ref, b_ref, o_ref):
    w = w_ref[...]
    for b in range(ct_ref.shape[1]):
        cb = ct_ref[:, b:b + 1]
        act = cb * _sigmoid(cb)
        o_ref[b:b + 1, :] = jnp.sum(w * act, axis=0, keepdims=True) + b_ref[...]


def _adaln(c, w, bias):
    B, D = c.shape
    n = w.shape[1]
    tn = ADALN_TN
    return pl.pallas_call(
        _adaln_kernel,
        grid=(n // tn,),
        in_specs=[pl.BlockSpec((D, B), lambda j: (0, 0)),
                  pl.BlockSpec((D, tn), lambda j: (0, j)),
                  pl.BlockSpec((1, tn), lambda j: (0, j))],
        out_specs=pl.BlockSpec((B, tn), lambda j: (0, j)),
        out_shape=jax.ShapeDtypeStruct((B, n), F32),
        compiler_params=_params(("arbitrary",)),
        name="adaln",
    )(c.T, w, bias.reshape(1, n))


def _norm_proj_kernel(x_ref, shift_ref, scale_ref, g_ref, ws_ref, h_ref, zs_ref, ws_bf16):
    @pl.when(pl.program_id(0) == 0)
    def _():
        ws_bf16[...] = ws_ref[...].T.astype(BF16)

    rows = x_ref.shape[0] // NORM_ROW_CHUNKS
    for c in range(NORM_ROW_CHUNKS):
        sl = slice(c * rows, (c + 1) * rows)
        x = x_ref[sl, :]
        y = x * lax.rsqrt(jnp.mean(x * x, axis=-1, keepdims=True) + EPS) * g_ref[...]
        h = (y * (1.0 + scale_ref[...]) + shift_ref[...]).astype(BF16)
        h_ref[sl, :] = h
        zs_ref[sl, :] = jnp.dot(h, ws_bf16[...], preferred_element_type=F32)


def _norm_proj(xt, mod4, norm_g, w_in_t, seq):
    T, D = xt.shape
    tm = NORM_TM
    per_b = seq // tm
    return pl.pallas_call(
        _norm_proj_kernel,
        grid=(T // tm,),
        in_specs=[pl.BlockSpec((tm, D), lambda i: (i, 0)),
                  pl.BlockSpec((None, None, 1, D), lambda i: (i // per_b, 0, 0, 0)),
                  pl.BlockSpec((None, None, 1, D), lambda i: (i // per_b, 1, 0, 0)),
                  pl.BlockSpec((1, D), lambda i: (0, 0)),
                  pl.BlockSpec((SMALL_WIDTH, D), lambda i: (0, 0), pipeline_mode=pl.Buffered(1))],
        out_specs=[pl.BlockSpec((tm, D), lambda i: (i, 0)),
                   pl.BlockSpec((tm, SMALL_WIDTH), lambda i: (i, 0))],
        out_shape=[jax.ShapeDtypeStruct((T, D), BF16),
                   jax.ShapeDtypeStruct((T, SMALL_WIDTH), F32)],
        scratch_shapes=[pltpu.VMEM((D, SMALL_WIDTH), BF16)],
        compiler_params=_params(("arbitrary",)),
        name="norm_proj",
    )(xt, mod4, mod4, norm_g.reshape(1, D), w_in_t)


_ACTIVATIONS = {
    "silu": lambda a: a * _sigmoid(a),
    "linear": lambda a: a,
    "sigmoid": _sigmoid,
}


def _gate_proj_kernel(h_ref, w_hbm, o_ref, w_f32, w_bf16, sem, *, tile_kinds, first_row):
    j = pl.program_id(0)
    tn = w_bf16.shape[0]

    def window_copy(tile):
        rows = pl.ds(pl.multiple_of(first_row + tile * tn, F32_SUBLANES), tn)
        return pltpu.make_async_copy(w_hbm.at[rows, :], w_f32, sem)

    @pl.when(pl.program_id(1) == 0)
    def _():
        @pl.when(j == 0)
        def _():
            window_copy(0).start()

        window_copy(j).wait()
        w_bf16[...] = w_f32[...].astype(BF16)

        @pl.when(j + 1 < pl.num_programs(0))
        def _():
            window_copy(j + 1).start()

    for kinds in sorted(set(tile_kinds)):
        tiles = [t for t, k in enumerate(tile_kinds) if k == kinds]
        cond = functools.reduce(jnp.logical_or, [j == t for t in tiles])

        @pl.when(cond)
        def _(kinds=kinds):
            acc = lax.dot_general(h_ref[...], w_bf16[...], NT_DIMS, preferred_element_type=F32)
            width = tn // len(kinds)
            for s, kind in enumerate(kinds):
                cols = slice(s * width, (s + 1) * width)
                o_ref[:, cols] = _ACTIVATIONS[kind](acc[:, cols]).astype(o_ref.dtype)


def _gate_proj(h, w_in_t):
    T, D = h.shape
    tm, tn = GATE_TM, GATE_TN
    first_row = Q_LORA + KV_LORA + QK_ROPE
    n = w_in_t.shape[0] - first_row
    assert n % tn == 0 and first_row % F32_SUBLANES == 0 and tn % F32_SUBLANES == 0
    group_kinds = (("silu",) * (MLA_WIDTH // POOL_WIDTH) + ("linear", "silu")
                   + ("sigmoid",) * (2 * D // POOL_WIDTH))
    per_tile = tn // POOL_WIDTH
    tile_kinds = tuple(group_kinds[t * per_tile:(t + 1) * per_tile] for t in range(n // tn))
    kern = functools.partial(_gate_proj_kernel, tile_kinds=tile_kinds, first_row=first_row)
    return pl.pallas_call(
        kern,
        grid=(n // tn, T // tm),
        in_specs=[pl.BlockSpec((tm, D), lambda j, i: (i, 0)),
                  pl.BlockSpec(memory_space=pl.ANY)],
        out_specs=pl.BlockSpec((tm, tn), lambda j, i: (i, j)),
        out_shape=jax.ShapeDtypeStruct((T, n), BF16),
        scratch_shapes=[pltpu.VMEM((tn, D), F32), pltpu.VMEM((tn, D), BF16),
                        pltpu.SemaphoreType.DMA(())],
        compiler_params=_params(("arbitrary", "arbitrary")),
        name="gate_proj",
    )(h, w_in_t)


def _mla_prep_kernel(zs_ref, posr_ref, invc_ref, qg_ref, kvg_ref, wuq_ref, wukv_ref,
                     qt_ref, k_ref, vt_ref, wq_ref, wk_ref, wv_ref, *, q_scale):
    @pl.when(pl.program_id(0) == 0)
    def _():
        wq_ref[...] = wuq_ref[...].T.astype(BF16)
        kv_head = QK_NOPE + V_HEAD
        for h in range(N_HEADS):
            wk_ref[:, h * QK_NOPE:(h + 1) * QK_NOPE] = (
                wukv_ref[:, h * kv_head:h * kv_head + QK_NOPE].astype(BF16))
            wv_ref[h * V_HEAD:(h + 1) * V_HEAD, :] = (
                wukv_ref[:, h * kv_head + QK_NOPE:(h + 1) * kv_head].T.astype(BF16))

    def rms(v, g):
        return (v * lax.rsqrt(jnp.mean(v * v, axis=-1, keepdims=True) + EPS) * g).astype(BF16)

    cqn = rms(zs_ref[:, 0:Q_LORA], qg_ref[...])
    ckvn = rms(zs_ref[:, Q_LORA:Q_LORA + KV_LORA], kvg_ref[...])
    kr = zs_ref[:, Q_LORA + KV_LORA:Q_LORA + KV_LORA + QK_ROPE]
    kr_sw = jnp.concatenate([kr[:, QK_ROPE // 2:], kr[:, :QK_ROPE // 2]], axis=1)

    ang_t = invc_ref[...] * posr_ref[...].astype(F32)
    cos_t, sin_t = jnp.cos(ang_t), jnp.sin(ang_t)
    cos, sin = cos_t.T, sin_t.T

    qf = lax.dot_general(wq_ref[...], cqn, NT_DIMS, preferred_element_type=F32)
    half = QK_ROPE // 2
    for h in range(N_HEADS):
        r0 = h * QK_HEAD + QK_NOPE
        qt_ref[h * QK_HEAD:r0, :] = (qf[h * QK_HEAD:r0] * q_scale).astype(BF16)
        rope = qf[r0:r0 + QK_ROPE]
        rope_sw = jnp.concatenate([rope[half:], rope[:half]], axis=0)
        qt_ref[r0:r0 + QK_ROPE, :] = ((rope * cos_t + rope_sw * sin_t) * q_scale).astype(BF16)

    kn = jnp.dot(ckvn, wk_ref[...], preferred_element_type=F32)
    k_rot = (kr * cos + kr_sw * sin).astype(BF16)
    for h in range(N_HEADS):
        k_ref[h, :, 0:QK_NOPE] = kn[:, h * QK_NOPE:(h + 1) * QK_NOPE].astype(BF16)
        k_ref[h, :, QK_NOPE:QK_HEAD] = k_rot

    vt_ref[...] = lax.dot_general(wv_ref[...], ckvn, NT_DIMS,
                                  preferred_element_type=F32).astype(BF16)


def _mla_prep(zs, posr, inv_signed, q_norm_g, kv_norm_g, w_uq, w_ukv, batch, seq, q_scale):
    tm = PREP_TM
    per_b = seq // tm
    const = lambda i: (0, 0)
    resident = functools.partial(pl.BlockSpec, pipeline_mode=pl.Buffered(1))
    kern = functools.partial(_mla_prep_kernel, q_scale=q_scale)
    return pl.pallas_call(
        kern,
        grid=(batch * per_b,),
        in_specs=[pl.BlockSpec((tm, SMALL_WIDTH), lambda i: (i, 0)),
                  pl.BlockSpec((None, 1, tm), lambda i: (i // per_b, 0, i % per_b)),
                  pl.BlockSpec((QK_ROPE, 1), const),
                  pl.BlockSpec((1, Q_LORA), const),
                  pl.BlockSpec((1, KV_LORA), const),
                  resident(w_uq.shape, const),
                  resident(w_ukv.shape, const)],
        out_specs=[pl.BlockSpec((None, N_HEADS * QK_HEAD, tm), lambda i: (i // per_b, 0, i % per_b)),
                   pl.BlockSpec((None, N_HEADS, tm, QK_HEAD), lambda i: (i // per_b, 0, i % per_b, 0)),
                   pl.BlockSpec((None, MLA_WIDTH, tm), lambda i: (i // per_b, 0, i % per_b))],
        out_shape=[jax.ShapeDtypeStruct((batch, N_HEADS * QK_HEAD, seq), BF16),
                   jax.ShapeDtypeStruct((batch, N_HEADS, seq, QK_HEAD), BF16),
                   jax.ShapeDtypeStruct((batch, MLA_WIDTH, seq), BF16)],
        scratch_shapes=[pltpu.VMEM((N_HEADS * QK_HEAD, Q_LORA), BF16),
                        pltpu.VMEM((KV_LORA, N_HEADS * QK_NOPE), BF16),
                        pltpu.VMEM((MLA_WIDTH, KV_LORA), BF16)],
        compiler_params=_params(("arbitrary",)),
        name="mla_prep",
    )(zs, posr, inv_signed.reshape(QK_ROPE, 1),
      q_norm_g.reshape(1, Q_LORA), kv_norm_g.reshape(1, KV_LORA), w_uq, w_ukv)


def _attention_kernel(qt_ref, k_ref, knext_ref, vt_ref, *refs, n_cast):
    cast_in, (o_ref, *cast_out) = refs[:n_cast], refs[n_cast:2 * n_cast + 1]
    s_a, m_a, s_b, m_b, lmin_ref, ksq_ref = refs[2 * n_cast + 1:]
    for w_in, w_out in zip(cast_in, cast_out):
        w_out[...] = w_in[...].astype(w_out.dtype)

    def max_sq_norm(rows):
        rf = rows.astype(F32)
        return jnp.max(jnp.sum(rf * rf, axis=1, keepdims=True), axis=0, keepdims=True)

    tq = lmin_ref.shape[1]
    nq = qt_ref.shape[1] // tq

    @pl.when(jnp.logical_and(pl.program_id(0) == 0, pl.program_id(1) == 0))
    def _():
        ksq_ref[...] = max_sq_norm(k_ref[...])

    k_norm = jnp.sqrt(ksq_ref[...])
    lmin_ref[...] = jnp.full(lmin_ref.shape, jnp.inf, F32)

    def fast_tile(i, next_ksq):
        off = pl.multiple_of(i * tq, tq)
        next_ksq = jnp.maximum(next_ksq, max_sq_norm(knext_ref[pl.ds(off, tq), :]))
        qt = qt_ref[:, pl.ds(off, tq)]
        qf = qt.astype(F32)
        shift = jnp.sqrt(jnp.sum(qf * qf, axis=0, keepdims=True)) * k_norm
        s = jnp.dot(k_ref[...], qt, preferred_element_type=F32)
        p = jnp.exp2(s - shift)
        l = jnp.sum(p, axis=0, keepdims=True)
        ot = jnp.dot(vt_ref[...], p.astype(BF16), preferred_element_type=F32)
        o_ref[pl.ds(off, tq), :] = (ot / l).T.astype(o_ref.dtype)
        lmin_ref[...] = jnp.minimum(lmin_ref[...], l)
        return next_ksq

    ksq_ref[...] = lax.fori_loop(0, nq, fast_tile, jnp.zeros((1, 1), F32), unroll=True)
    trusted = jnp.min(lmin_ref[...]) >= MIN_SOFTMAX_MASS

    @pl.when(jnp.logical_not(trusted))
    def _():
        _attention_exact(qt_ref, k_ref, vt_ref, o_ref, s_a, m_a, s_b, m_b)


def _attention_exact(qt_ref, k_ref, vt_ref, o_ref, s_a, m_a, s_b, m_b):
    tq = s_a.shape[1]
    nq = qt_ref.shape[1] // tq

    def scores(i, s_ref, m_ref):
        off = pl.multiple_of(i * tq, tq)
        s = jnp.dot(k_ref[...], qt_ref[:, pl.ds(off, tq)], preferred_element_type=F32)
        s_ref[...] = s
        m_ref[...] = jnp.max(s, axis=0, keepdims=True)

    def finish(i, s_ref, m_ref):
        off = pl.multiple_of(i * tq, tq)
        p = jnp.exp2(s_ref[...] - m_ref[...])
        l = jnp.sum(p, axis=0, keepdims=True)
        ot = jnp.dot(vt_ref[...], p.astype(BF16), preferred_element_type=F32)
        o_ref[pl.ds(off, tq), :] = (ot / l).T.astype(o_ref.dtype)

    scores(0, s_a, m_a)

    def step(i, carry):
        @pl.when(i % 2 == 1)
        def _():
            scores(i, s_b, m_b)
            finish(i - 1, s_a, m_a)

        @pl.when(i % 2 == 0)
        def _():
            scores(i, s_a, m_a)
            finish(i - 1, s_b, m_b)

        return carry

    lax.fori_loop(1, nq, step, 0)
    finish(nq - 1, s_b, m_b)


def _attention(qt, k, vt, batch, seq, later_weights):
    tq, tq_fast = ATTN_TQ_EXACT, ATTN_TQ_FAST
    assert seq % (2 * tq) == 0 and seq % tq_fast == 0
    steps = batch * N_HEADS
    slabs = [w.shape[0] // steps for w in later_weights]
    assert all(s % BF16_SUBLANES == 0 and s * steps == w.shape[0]
               for s, w in zip(slabs, later_weights))
    step = lambda b, h: (b * N_HEADS + h, 0)
    cast_specs = [pl.BlockSpec((s, w.shape[1]), step) for s, w in zip(slabs, later_weights)]

    def next_head(b, h):
        t = jnp.minimum(b * N_HEADS + h + 1, steps - 1)
        return t // N_HEADS, t % N_HEADS, 0, 0

    outs = pl.pallas_call(
        functools.partial(_attention_kernel, n_cast=len(later_weights)),
        grid=(batch, N_HEADS),
        in_specs=[pl.BlockSpec((None, QK_HEAD, seq), lambda b, h: (b, h, 0)),
                  pl.BlockSpec((None, None, seq, QK_HEAD), lambda b, h: (b, h, 0, 0)),
                  pl.BlockSpec((None, None, seq, QK_HEAD), next_head),
                  pl.BlockSpec((None, V_HEAD, seq), lambda b, h: (b, h, 0))] + cast_specs,
        out_specs=[pl.BlockSpec((None, seq, V_HEAD), lambda b, h: (b, 0, h))] + cast_specs,
        out_shape=[jax.ShapeDtypeStruct((batch, seq, MLA_WIDTH), BF16)]
                  + [jax.ShapeDtypeStruct(w.shape, BF16) for w in later_weights],
        scratch_shapes=[pltpu.VMEM((seq, tq), F32), pltpu.VMEM((1, tq), F32),
                        pltpu.VMEM((seq, tq), F32), pltpu.VMEM((1, tq), F32),
                        pltpu.VMEM((1, tq_fast), F32), pltpu.VMEM((1, 1), F32)],
        compiler_params=_params(("arbitrary", "arbitrary")),
        name="attention",
    )(qt, k, k, vt, *later_weights)
    return outs[0], outs[1:]


def _merge_out_kernel(x_ref, gate_ref, attn_ref, gm_ref, gp_ref, vp_ref, vprev_ref, vnext_ref,
                      mm_ref, mp_ref, pw_ref, ps_ref, wop_ref, wom_ref, wout_ref, fg_ref, o_ref,
                      *, seq, final_norm):
    tm = x_ref.shape[0]
    t0 = (pl.program_id(0) % (seq // tm)) * tm

    gated = attn_ref[...] * gm_ref[...]

    cur = vp_ref[...].astype(F32)
    prev = jnp.where(t0 > 0, vprev_ref[...].astype(F32), 0.0)
    nxt = jnp.where(t0 + tm < seq, vnext_ref[...].astype(F32), 0.0)
    ext = jnp.concatenate([prev, cur, nxt], axis=0)
    n_ext = tm + 2 * POOL_HALO
    tok = t0 + lax.broadcasted_iota(jnp.int32, (tm, 1), 0)
    mixed = []
    p_mla = []
    mla_cols = wom_ref.shape[1] // POOL_GROUPS
    for g, w in enumerate(POOL_WINDOWS):
        p_mla.append(jnp.dot(gated, wom_ref[:, g * mla_cols:(g + 1) * mla_cols],
                             preferred_element_type=F32))
        acc = ext[:, g * POOL_GROUP_DIM:(g + 1) * POOL_GROUP_DIM]
        acc = acc + pltpu.roll(acc, 1, axis=0)
        half = 1
        while 2 * half < w:
            acc = pltpu.roll(acc, half, axis=0) + pltpu.roll(acc, n_ext - half, axis=0)
            half *= 2
        wsum = acc[POOL_HALO:POOL_HALO + tm]
        count = (jnp.minimum(tok + w // 2, seq) - jnp.maximum(tok - w // 2, 0)).astype(F32)
        pooled = wsum / count - cur[:, g * POOL_GROUP_DIM:(g + 1) * POOL_GROUP_DIM]
        mixed.append(jnp.dot(pooled.astype(BF16), pw_ref[g], preferred_element_type=F32))
    mixed = jnp.concatenate(mixed, axis=1)
    p_mla = jnp.concatenate(p_mla, axis=1)
    u = (mixed * ps_ref[...] * gp_ref[...].astype(F32)).astype(BF16)
    p_pool = jnp.dot(u, wop_ref[...], preferred_element_type=F32)

    y = mm_ref[...].astype(F32) * p_mla + mp_ref[...].astype(F32) * p_pool
    r = jnp.dot(y.astype(BF16), wout_ref[...], preferred_element_type=F32)
    xo = x_ref[...] + gate_ref[...] * r
    if final_norm:
        xo = xo * lax.rsqrt(jnp.mean(xo * xo, axis=-1, keepdims=True) + EPS) * fg_ref[...]
    o_ref[...] = xo


def _merge_out(xt, mod4, attn, zbig, pool_w, pool_scale, w_o_pool, w_o_mla, w_out, final_g,
               seq, final_norm):
    T, D = xt.shape
    tm = MERGE_TM
    per_b = seq // tm
    halo_per_tile = tm // POOL_HALO
    n_halo = T // POOL_HALO
    vp_blk = MLA_WIDTH // POOL_WIDTH
    gp_blk = vp_blk + 1
    mm_blk = (MLA_WIDTH + 2 * POOL_WIDTH) // D
    resident = functools.partial(pl.BlockSpec, pipeline_mode=pl.Buffered(1))
    kern = functools.partial(_merge_out_kernel, seq=seq, final_norm=final_norm)
    return pl.pallas_call(
        kern,
        grid=(T // tm,),
        in_specs=[pl.BlockSpec((tm, D), lambda i: (i, 0)),
                  pl.BlockSpec((None, None, 1, D), lambda i: (i // per_b, 2, 0, 0)),
                  pl.BlockSpec((tm, MLA_WIDTH), lambda i: (i, 0)),
                  pl.BlockSpec((tm, MLA_WIDTH), lambda i: (i, 0)),
                  pl.BlockSpec((tm, POOL_WIDTH), lambda i: (i, gp_blk)),
                  pl.BlockSpec((tm, POOL_WIDTH), lambda i: (i, vp_blk)),
                  pl.BlockSpec((POOL_HALO, POOL_WIDTH),
                               lambda i: (jnp.maximum(i * halo_per_tile - 1, 0), vp_blk)),
                  pl.BlockSpec((POOL_HALO, POOL_WIDTH),
                               lambda i: (jnp.minimum((i + 1) * halo_per_tile, n_halo - 1), vp_blk)),
                  pl.BlockSpec((tm, D), lambda i: (i, mm_blk)),
                  pl.BlockSpec((tm, D), lambda i: (i, mm_blk + 1)),
                  resident(pool_w.shape, lambda i: (0, 0, 0)),
                  resident((1, POOL_WIDTH), lambda i: (0, 0)),
                  resident(w_o_pool.shape, lambda i: (0, 0)),
                  resident(w_o_mla.shape, lambda i: (0, 0)),
                  resident(w_out.shape, lambda i: (0, 0)),
                  resident((1, D), lambda i: (0, 0))],
        out_specs=pl.BlockSpec((tm, D), lambda i: (i, 0)),
        out_shape=jax.ShapeDtypeStruct((T, D), F32),
        compiler_params=_params(("arbitrary",)),
        name="merge_out",
    )(xt, mod4, attn, zbig, zbig, zbig, zbig, zbig, zbig, zbig,
      pool_w, pool_scale.reshape(1, POOL_WIDTH), w_o_pool, w_o_mla, w_out, final_g.reshape(1, D))


def kernel(x, c, positions, ada_w, ada_b, norm_g, w_in, q_norm_g, w_uq, kv_norm_g, w_ukv, w_o_mla,
           pool_w, pool_scale, w_o_pool, w_out, final_g):
    B, S, D = x.shape
    depth = ada_w.shape[0]
    inv_freq = 1.0 / (ROPE_THETA ** (jnp.arange(0, QK_ROPE, 2, dtype=F32) / QK_ROPE))
    inv_signed = jnp.concatenate([-inv_freq, inv_freq])
    posr = positions.reshape(B, 1, S)
    q_scale = QK_HEAD ** -0.5 * math.log2(math.e)

    xt = x.reshape(B * S, D)
    for l in range(depth):
        mod4 = _adaln(c, ada_w[l], ada_b[l]).reshape(B, 3, 1, D)
        w_in_t = w_in[l].T
        h, zs = _norm_proj(xt, mod4, norm_g[l], w_in_t, S)
        zbig = _gate_proj(h, w_in_t)
        qt, k, vt = _mla_prep(zs, posr, inv_signed, q_norm_g[l], kv_norm_g[l],
                              w_uq[l], w_ukv[l], B, S, q_scale)
        attn, (pw, wop, wom, wout) = _attention(
            qt, k, vt, B, S,
            (pool_w[l].reshape(POOL_WIDTH, POOL_GROUP_DIM), w_o_pool[l], w_o_mla[l], w_out[l]))
        xt = _merge_out(xt, mod4, attn.reshape(B * S, MLA_WIDTH), zbig,
                        pw.reshape(POOL_GROUPS, POOL_GROUP_DIM, POOL_GROUP_DIM), pool_scale[l],
                        wop, wom, wout, final_g, S, final_norm=(l == depth - 1))
    return xt.reshape(B, S, D)
```

```python
import functools
import math

import jax
import jax.numpy as jnp
from jax import lax
from jax.experimental import pallas as pl
from jax.experimental.pallas import tpu as pltpu

EPS = 1e-6
N_HEADS = 16
QK_NOPE = 128
QK_ROPE = 64
QK_HEAD = QK_NOPE + QK_ROPE
V_HEAD = 128
Q_LORA = 512
KV_LORA = 512
MLA_WIDTH = N_HEADS * V_HEAD
ROPE_THETA = 10000.0
POOL_WINDOWS = (2, 4, 8, 16)
POOL_GROUPS = len(POOL_WINDOWS)
POOL_GROUP_DIM = 256
POOL_WIDTH = POOL_GROUPS * POOL_GROUP_DIM
POOL_HALO = 16
SMALL_WIDTH = Q_LORA + KV_LORA + 2 * QK_ROPE

V7X_VMEM_LIMIT = 56 * 1024 * 1024
ADALN_TN = 1024
NORM_TM = 512
NORM_ROW_CHUNKS = 4
GATE_TM, GATE_TN = 1024, 2048
PREP_TM = 512
ATTN_TQ_EXACT = 256
ATTN_TQ_FAST = 512
MERGE_TM = 256
BF16_SUBLANES = 16
F32_SUBLANES = 8
MIN_SOFTMAX_MASS = 2.0 ** -60

F32 = jnp.float32
BF16 = jnp.bfloat16
NT_DIMS = (((1,), (1,)), ((), ()))


def _sigmoid(v):
    return 0.5 * jnp.tanh(0.5 * v) + 0.5


def _params(semantics):
    return pltpu.CompilerParams(dimension_semantics=semantics, vmem_limit_bytes=V7X_VMEM_LIMIT)


def _adaln_kernel(ct_ref, w_ref, b_ref, o_ref):
    w = w_ref[...]
    for b in range(ct_ref.shape[1]):
        cb = ct_ref[:, b:b + 1]
        act = cb * _sigmoid(cb)
        o_ref[b:b + 1, :] = jnp.sum(w * act, axis=0, keepdims=True) + b_ref[...]


def _adaln(c, w, bias):
    B, D = c.shape
    n = w.shape[1]
    tn = ADALN_TN
    return pl.pallas_call(
        _adaln_kernel,
        grid=(n // tn,),
        in_specs=[pl.BlockSpec((D, B), lambda j: (0, 0)),
                  pl.BlockSpec((D, tn), lambda j: (0, j)),
                  pl.BlockSpec((1, tn), lambda j: (0, j))],
        out_specs=pl.BlockSpec((B, tn), lambda j: (0, j)),
        out_shape=jax.ShapeDtypeStruct((B, n), F32),
        compiler_params=_params(("arbitrary",)),
        name="adaln",
    )(c.T, w, bias.reshape(1, n))


def _norm_proj_kernel(x_ref, shift_ref, scale_ref, g_ref, ws_ref, h_ref, zs_ref, ws_bf16):
    @pl.when(pl.program_id(0) == 0)
    def _():
        ws_bf16[...] = ws_ref[...].T.astype(BF16)

    rows = x_ref.shape[0] // NORM_ROW_CHUNKS
    for c in range(NORM_ROW_CHUNKS):
        sl = slice(c * rows, (c + 1) * rows)
        x = x_ref[sl, :]
        y = x * lax.rsqrt(jnp.mean(x * x, axis=-1, keepdims=True) + EPS) * g_ref[...]
        h = (y * (1.0 + scale_ref[...]) + shift_ref[...]).astype(BF16)
        h_ref[sl, :] = h
        zs_ref[sl, :] = jnp.dot(h, ws_bf16[...], preferred_element_type=F32)


def _norm_proj(xt, mod4, norm_g, w_in_t, seq):
    T, D = xt.shape
    tm = NORM_TM
    per_b = seq // tm
    return pl.pallas_call(
        _norm_proj_kernel,
        grid=(T // tm,),
        in_specs=[pl.BlockSpec((tm, D), lambda i: (i, 0)),
                  pl.BlockSpec((None, None, 1, D), lambda i: (i // per_b, 0, 0, 0)),
                  pl.BlockSpec((None, None, 1, D), lambda i: (i // per_b, 1, 0, 0)),
                  pl.BlockSpec((1, D), lambda i: (0, 0)),
                  pl.BlockSpec((SMALL_WIDTH, D), lambda i: (0, 0), pipeline_mode=pl.Buffered(1))],
        out_specs=[pl.BlockSpec((tm, D), lambda i: (i, 0)),
                   pl.BlockSpec((tm, SMALL_WIDTH), lambda i: (i, 0))],
        out_shape=[jax.ShapeDtypeStruct((T, D), BF16),
                   jax.ShapeDtypeStruct((T, SMALL_WIDTH), F32)],
        scratch_shapes=[pltpu.VMEM((D, SMALL_WIDTH), BF16)],
        compiler_params=_params(("arbitrary",)),
        name="norm_proj",
    )(xt, mod4, mod4, norm_g.reshape(1, D), w_in_t)


_ACTIVATIONS = {
    "silu": lambda a: a * _sigmoid(a),
    "linear": lambda a: a,
    "sigmoid": _sigmoid,
}


def _gate_proj_kernel(h_ref, w_hbm, o_ref, w_f32, w_bf16, sem, *, tile_kinds, first_row):
    j = pl.program_id(0)
    tn = w_bf16.shape[0]

    def window_copy(tile):
        rows = pl.ds(pl.multiple_of(first_row + tile * tn, F32_SUBLANES), tn)
        return pltpu.make_async_copy(w_hbm.at[rows, :], w_f32, sem)

    @pl.when(pl.program_id(1) == 0)
    def _():
        @pl.when(j == 0)
        def _():
            window_copy(0).start()

        window_copy(j).wait()
        w_bf16[...] = w_f32[...].astype(BF16)

        @pl.when(j + 1 < pl.num_programs(0))
        def _():
            window_copy(j + 1).start()

    for kinds in sorted(set(tile_kinds)):
        tiles = [t for t, k in enumerate(tile_kinds) if k == kinds]
        cond = functools.reduce(jnp.logical_or, [j == t for t in tiles])

        @pl.when(cond)
        def _(kinds=kinds):
            acc = lax.dot_general(h_ref[...], w_bf16[...], NT_DIMS, preferred_element_type=F32)
            width = tn // len(kinds)
            for s, kind in enumerate(kinds):
                cols = slice(s * width, (s + 1) * width)
                o_ref[:, cols] = _ACTIVATIONS[kind](acc[:, cols]).astype(o_ref.dtype)


def _gate_proj(h, w_in_t):
    T, D = h.shape
    tm, tn = GATE_TM, GATE_TN
    first_row = Q_LORA + KV_LORA + QK_ROPE
    n = w_in_t.shape[0] - first_row
    assert n % tn == 0 and first_row % F32_SUBLANES == 0 and tn % F32_SUBLANES == 0
    group_kinds = (("silu",) * (MLA_WIDTH // POOL_WIDTH) + ("linear", "silu")
                   + ("sigmoid",) * (2 * D // POOL_WIDTH))
    per_tile = tn // POOL_WIDTH
    tile_kinds = tuple(group_kinds[t * per_tile:(t + 1) * per_tile] for t in range(n // tn))
    kern = functools.partial(_gate_proj_kernel, tile_kinds=tile_kinds, first_row=first_row)
    return pl.pallas_call(
        kern,
        grid=(n // tn, T // tm),
        in_specs=[pl.BlockSpec((tm, D), lambda j, i: (i, 0)),
                  pl.BlockSpec(memory_space=pl.ANY)],
        out_specs=pl.BlockSpec((tm, tn), lambda j, i: (i, j)),
        out_shape=jax.ShapeDtypeStruct((T, n), BF16),
        scratch_shapes=[pltpu.VMEM((tn, D), F32), pltpu.VMEM((tn, D), BF16),
                        pltpu.SemaphoreType.DMA(())],
        compiler_params=_params(("arbitrary", "arbitrary")),
        name="gate_proj",
    )(h, w_in_t)


def _mla_prep_kernel(zs_ref, posr_ref, invc_ref, qg_ref, kvg_ref, wuq_ref, wukv_ref,
                     qt_ref, k_ref, vt_ref, wq_ref, wk_ref, wv_ref, *, q_scale):
    @pl.when(pl.program_id(0) == 0)
    def _():
        wq_ref[...] = wuq_ref[...].T.astype(BF16)
        kv_head = QK_NOPE + V_HEAD
        for h in range(N_HEADS):
            wk_ref[:, h * QK_NOPE:(h + 1) * QK_NOPE] = (
                wukv_ref[:, h * kv_head:h * kv_head + QK_NOPE].astype(BF16))
            wv_ref[h * V_HEAD:(h + 1) * V_HEAD, :] = (
                wukv_ref[:, h * kv_head + QK_NOPE:(h + 1) * kv_head].T.astype(BF16))

    def rms(v, g):
        return (v * lax.rsqrt(jnp.mean(v * v, axis=-1, keepdims=True) + EPS) * g).astype(BF16)

    cqn = rms(zs_ref[:, 0:Q_LORA], qg_ref[...])
    ckvn = rms(zs_ref[:, Q_LORA:Q_LORA + KV_LORA], kvg_ref[...])
    kr = zs_ref[:, Q_LORA + KV_LORA:Q_LORA + KV_LORA + QK_ROPE]
    kr_sw = jnp.concatenate([kr[:, QK_ROPE // 2:], kr[:, :QK_ROPE // 2]], axis=1)

    ang_t = invc_ref[...] * posr_ref[...].astype(F32)
    cos_t, sin_t = jnp.cos(ang_t), jnp.sin(ang_t)
    cos, sin = cos_t.T, sin_t.T

    qf = lax.dot_general(wq_ref[...], cqn, NT_DIMS, preferred_element_type=F32)
    half = QK_ROPE // 2
    for h in range(N_HEADS):
        r0 = h * QK_HEAD + QK_NOPE
        qt_ref[h * QK_HEAD:r0, :] = (qf[h * QK_HEAD:r0] * q_scale).astype(BF16)
        rope = qf[r0:r0 + QK_ROPE]
        rope_sw = jnp.concatenate([rope[half:], rope[:half]], axis=0)
        qt_ref[r0:r0 + QK_ROPE, :] = ((rope * cos_t + rope_sw * sin_t) * q_scale).astype(BF16)

    kn = jnp.dot(ckvn, wk_ref[...], preferred_element_type=F32)
    k_rot = (kr * cos + kr_sw * sin).astype(BF16)
    for h in range(N_HEADS):
        k_ref[h, :, 0:QK_NOPE] = kn[:, h * QK_NOPE:(h + 1) * QK_NOPE].astype(BF16)
        k_ref[h, :, QK_NOPE:QK_HEAD] = k_rot

    vt_ref[...] = lax.dot_general(wv_ref[...], ckvn, NT_DIMS,
                                  preferred_element_type=F32).astype(BF16)


def _mla_prep(zs, posr, inv_signed, q_norm_g, kv_norm_g, w_uq, w_ukv, batch, seq, q_scale):
    tm = PREP_TM
    per_b = seq // tm
    const = lambda i: (0, 0)
    resident = functools.partial(pl.BlockSpec, pipeline_mode=pl.Buffered(1))
    kern = functools.partial(_mla_prep_kernel, q_scale=q_scale)
    return pl.pallas_call(
        kern,
        grid=(batch * per_b,),
        in_specs=[pl.BlockSpec((tm, SMALL_WIDTH), lambda i: (i, 0)),
                  pl.BlockSpec((None, 1, tm), lambda i: (i // per_b, 0, i % per_b)),
                  pl.BlockSpec((QK_ROPE, 1), const),
                  pl.BlockSpec((1, Q_LORA), const),
                  pl.BlockSpec((1, KV_LORA), const),
                  resident(w_uq.shape, const),
                  resident(w_ukv.shape, const)],
        out_specs=[pl.BlockSpec((None, N_HEADS * QK_HEAD, tm), lambda i: (i // per_b, 0, i % per_b)),
                   pl.BlockSpec((None, N_HEADS, tm, QK_HEAD), lambda i: (i // per_b, 0, i % per_b, 0)),
                   pl.BlockSpec((None, MLA_WIDTH, tm), lambda i: (i // per_b, 0, i % per_b))],
        out_shape=[jax.ShapeDtypeStruct((batch, N_HEADS * QK_HEAD, seq), BF16),
                   jax.ShapeDtypeStruct((batch, N_HEADS, seq, QK_HEAD), BF16),
                   jax.ShapeDtypeStruct((batch, MLA_WIDTH, seq), BF16)],
        scratch_shapes=[pltpu.VMEM((N_HEADS * QK_HEAD, Q_LORA), BF16),
                        pltpu.VMEM((KV_LORA, N_HEADS * QK_NOPE), BF16),
                        pltpu.VMEM((MLA_WIDTH, KV_LORA), BF16)],
        compiler_params=_params(("arbitrary",)),
        name="mla_prep",
    )(zs, posr, inv_signed.reshape(QK_ROPE, 1),
      q_norm_g.reshape(1, Q_LORA), kv_norm_g.reshape(1, KV_LORA), w_uq, w_ukv)


def _attention_kernel(qt_ref, k_ref, knext_ref, vt_ref, *refs, n_cast):
    cast_in, (o_ref, *cast_out) = refs[:n_cast], refs[n_cast:2 * n_cast + 1]
    lmin_ref, ksq_ref = refs[2 * n_cast + 1:]
    for w_in, w_out in zip(cast_in, cast_out):
        w_out[...] = w_in[...].astype(w_out.dtype)

    def max_sq_norm(rows):
        rf = rows.astype(F32)
        return jnp.max(jnp.sum(rf * rf, axis=1, keepdims=True), axis=0, keepdims=True)

    tq = lmin_ref.shape[1]
    nq = qt_ref.shape[1] // tq

    @pl.when(jnp.logical_and(pl.program_id(0) == 0, pl.program_id(1) == 0))
    def _():
        ksq_ref[...] = max_sq_norm(k_ref[...])

    k_norm = jnp.sqrt(ksq_ref[...])
    lmin_ref[...] = jnp.full(lmin_ref.shape, jnp.inf, F32)

    def fast_tile(i, next_ksq):
        off = pl.multiple_of(i * tq, tq)
        next_ksq = jnp.maximum(next_ksq, max_sq_norm(knext_ref[pl.ds(off, tq), :]))
        qt = qt_ref[:, pl.ds(off, tq)]
        qf = qt.astype(F32)
        shift = jnp.sqrt(jnp.sum(qf * qf, axis=0, keepdims=True)) * k_norm
        s = jnp.dot(k_ref[...], qt, preferred_element_type=F32)
        p = jnp.exp2(s - shift)
        l = jnp.sum(p, axis=0, keepdims=True)
        ot = jnp.dot(vt_ref[...], p.astype(BF16), preferred_element_type=F32)
        o_ref[pl.ds(off, tq), :] = (ot / l).T.astype(o_ref.dtype)
        lmin_ref[...] = jnp.minimum(lmin_ref[...], l)
        return next_ksq

    ksq_ref[...] = lax.fori_loop(0, nq, fast_tile, jnp.zeros((1, 1), F32), unroll=True)
    trusted = jnp.min(lmin_ref[...]) >= MIN_SOFTMAX_MASS

    @pl.when(jnp.logical_not(trusted))
    def _():
        _attention_exact(qt_ref, k_ref, vt_ref, o_ref)


def _attention_exact(qt_ref, k_ref, vt_ref, o_ref):
    tq = ATTN_TQ_EXACT
    nq = qt_ref.shape[1] // tq

    def tile(i, carry):
        off = pl.multiple_of(i * tq, tq)
        s = jnp.dot(k_ref[...], qt_ref[:, pl.ds(off, tq)], preferred_element_type=F32)
        p = jnp.exp2(s - jnp.max(s, axis=0, keepdims=True))
        l = jnp.sum(p, axis=0, keepdims=True)
        ot = jnp.dot(vt_ref[...], p.astype(BF16), preferred_element_type=F32)
        o_ref[pl.ds(off, tq), :] = (ot / l).T.astype(o_ref.dtype)
        return carry

    lax.fori_loop(0, nq, tile, 0)


def _attention(qt, k, vt, batch, seq, later_weights):
    tq_fast = ATTN_TQ_FAST
    assert seq % ATTN_TQ_EXACT == 0 and seq % tq_fast == 0
    steps = batch * N_HEADS
    slabs = [w.shape[0] // steps for w in later_weights]
    assert all(s % BF16_SUBLANES == 0 and s * steps == w.shape[0]
               for s, w in zip(slabs, later_weights))
    step = lambda b, h: (b * N_HEADS + h, 0)
    cast_specs = [pl.BlockSpec((s, w.shape[1]), step) for s, w in zip(slabs, later_weights)]

    def next_head(b, h):
        t = jnp.minimum(b * N_HEADS + h + 1, steps - 1)
        return t // N_HEADS, t % N_HEADS, 0, 0

    outs = pl.pallas_call(
        functools.partial(_attention_kernel, n_cast=len(later_weights)),
        grid=(batch, N_HEADS),
        in_specs=[pl.BlockSpec((None, QK_HEAD, seq), lambda b, h: (b, h, 0)),
                  pl.BlockSpec((None, None, seq, QK_HEAD), lambda b, h: (b, h, 0, 0)),
                  pl.BlockSpec((None, None, seq, QK_HEAD), next_head),
                  pl.BlockSpec((None, V_HEAD, seq), lambda b, h: (b, h, 0))] + cast_specs,
        out_specs=[pl.BlockSpec((None, seq, V_HEAD), lambda b, h: (b, 0, h))] + cast_specs,
        out_shape=[jax.ShapeDtypeStruct((batch, seq, MLA_WIDTH), BF16)]
                  + [jax.ShapeDtypeStruct(w.shape, BF16) for w in later_weights],
        scratch_shapes=[pltpu.VMEM((1, tq_fast), F32), pltpu.VMEM((1, 1), F32)],
        compiler_params=_params(("arbitrary", "arbitrary")),
        name="attention",
    )(qt, k, k, vt, *later_weights)
    return outs[0], outs[1:]


def _merge_out_kernel(x_ref, gate_ref, attn_ref, gm_ref, gp_ref, vp_ref, vprev_ref, vnext_ref,
                      mm_ref, mp_ref, pw_ref, ps_ref, wop_ref, wom_ref, wout_ref, fg_ref, o_ref,
                      *, seq, final_norm):
    tm = x_ref.shape[0]
    t0 = (pl.program_id(0) % (seq // tm)) * tm

    gated = attn_ref[...] * gm_ref[...]

    cur = vp_ref[...].astype(F32)
    prev = jnp.where(t0 > 0, vprev_ref[...].astype(F32), 0.0)
    nxt = jnp.where(t0 + tm < seq, vnext_ref[...].astype(F32), 0.0)
    ext = jnp.concatenate([prev, cur, nxt], axis=0)
    n_ext = tm + 2 * POOL_HALO
    tok = t0 + lax.broadcasted_iota(jnp.int32, (tm, 1), 0)
    mixed = []
    p_mla = []
    mla_cols = wom_ref.shape[1] // POOL_GROUPS
    for g, w in enumerate(POOL_WINDOWS):
        p_mla.append(jnp.dot(gated, wom_ref[:, g * mla_cols:(g + 1) * mla_cols],
                             preferred_element_type=F32))
        acc = ext[:, g * POOL_GROUP_DIM:(g + 1) * POOL_GROUP_DIM]
        acc = acc + pltpu.roll(acc, 1, axis=0)
        half = 1
        while 2 * half < w:
            acc = pltpu.roll(acc, half, axis=0) + pltpu.roll(acc, n_ext - half, axis=0)
            half *= 2
        wsum = acc[POOL_HALO:POOL_HALO + tm]
        count = (jnp.minimum(tok + w // 2, seq) - jnp.maximum(tok - w // 2, 0)).astype(F32)
        pooled = wsum / count - cur[:, g * POOL_GROUP_DIM:(g + 1) * POOL_GROUP_DIM]
        mixed.append(jnp.dot(pooled.astype(BF16), pw_ref[g], preferred_element_type=F32))
    mixed = jnp.concatenate(mixed, axis=1)
    p_mla = jnp.concatenate(p_mla, axis=1)
    u = (mixed * ps_ref[...] * gp_ref[...].astype(F32)).astype(BF16)
    p_pool = jnp.dot(u, wop_ref[...], preferred_element_type=F32)

    y = mm_ref[...].astype(F32) * p_mla + mp_ref[...].astype(F32) * p_pool
    r = jnp.dot(y.astype(BF16), wout_ref[...], preferred_element_type=F32)
    xo = x_ref[...] + gate_ref[...] * r
    if final_norm:
        xo = xo * lax.rsqrt(jnp.mean(xo * xo, axis=-1, keepdims=True) + EPS) * fg_ref[...]
    o_ref[...] = xo


def _merge_out(xt, mod4, attn, zbig, pool_w, pool_scale, w_o_pool, w_o_mla, w_out, final_g,
               seq, final_norm):
    T, D = xt.shape
    tm = MERGE_TM
    per_b = seq // tm
    halo_per_tile = tm // POOL_HALO
    n_halo = T // POOL_HALO
    vp_blk = MLA_WIDTH // POOL_WIDTH
    gp_blk = vp_blk + 1
    mm_blk = (MLA_WIDTH + 2 * POOL_WIDTH) // D
    resident = functools.partial(pl.BlockSpec, pipeline_mode=pl.Buffered(1))
    kern = functools.partial(_merge_out_kernel, seq=seq, final_norm=final_norm)
    return pl.pallas_call(
        kern,
        grid=(T // tm,),
        in_specs=[pl.BlockSpec((tm, D), lambda i: (i, 0)),
                  pl.BlockSpec((None, None, 1, D), lambda i: (i // per_b, 2, 0, 0)),
                  pl.BlockSpec((tm, MLA_WIDTH), lambda i: (i, 0)),
                  pl.BlockSpec((tm, MLA_WIDTH), lambda i: (i, 0)),
                  pl.BlockSpec((tm, POOL_WIDTH), lambda i: (i, gp_blk)),
                  pl.BlockSpec((tm, POOL_WIDTH), lambda i: (i, vp_blk)),
                  pl.BlockSpec((POOL_HALO, POOL_WIDTH),
                               lambda i: (jnp.maximum(i * halo_per_tile - 1, 0), vp_blk)),
                  pl.BlockSpec((POOL_HALO, POOL_WIDTH),
                               lambda i: (jnp.minimum((i + 1) * halo_per_tile, n_halo - 1), vp_blk)),
                  pl.BlockSpec((tm, D), lambda i: (i, mm_blk)),
                  pl.BlockSpec((tm, D), lambda i: (i, mm_blk + 1)),
                  resident(pool_w.shape, lambda i: (0, 0, 0)),
                  resident((1, POOL_WIDTH), lambda i: (0, 0)),
                  resident(w_o_pool.shape, lambda i: (0, 0)),
                  resident(w_o_mla.shape, lambda i: (0, 0)),
                  resident(w_out.shape, lambda i: (0, 0)),
                  resident((1, D), lambda i: (0, 0))],
        out_specs=pl.BlockSpec((tm, D), lambda i: (i, 0)),
        out_shape=jax.ShapeDtypeStruct((T, D), F32),
        compiler_params=_params(("arbitrary",)),
        name="merge_out",
    )(xt, mod4, attn, zbig, zbig, zbig, zbig, zbig, zbig, zbig,
      pool_w, pool_scale.reshape(1, POOL_WIDTH), w_o_pool, w_o_mla, w_out, final_g.reshape(1, D))


def kernel(x, c, positions, ada_w, ada_b, norm_g, w_in, q_norm_g, w_uq, kv_norm_g, w_ukv, w_o_mla,
           pool_w, pool_scale, w_o_pool, w_out, final_g):
    B, S, D = x.shape
    depth = ada_w.shape[0]
    inv_freq = 1.0 / (ROPE_THETA ** (jnp.arange(0, QK_ROPE, 2, dtype=F32) / QK_ROPE))
    inv_signed = jnp.concatenate([-inv_freq, inv_freq])
    posr = positions.reshape(B, 1, S)
    q_scale = QK_HEAD ** -0.5 * math.log2(math.e)

    xt = x.reshape(B * S, D)
    for l in range(depth):
        mod4 = _adaln(c, ada_w[l], ada_b[l]).reshape(B, 3, 1, D)
        w_in_t = w_in[l].T
        h, zs = _norm_proj(xt, mod4, norm_g[l], w_in_t, S)
        zbig = _gate_proj(h, w_in_t)
        qt, k, vt = _mla_prep(zs, posr, inv_signed, q_norm_g[l], kv_norm_g[l],
                              w_uq[l], w_ukv[l], B, S, q_scale)
        attn, (pw, wop, wom, wout) = _attention(
            qt, k, vt, B, S,
            (pool_w[l].reshape(POOL_WIDTH, POOL_GROUP_DIM), w_o_pool[l], w_o_mla[l], w_out[l]))
        xt = _merge_out(xt, mod4, attn.reshape(B * S, MLA_WIDTH), zbig,
                        pw.reshape(POOL_GROUPS, POOL_GROUP_DIM, POOL_GROUP_DIM), pool_scale[l],
                        wop, wom, wout, final_g, S, final_norm=(l == depth - 1))
    return xt.reshape(B, S, D)
```

```python
import functools
import math

import jax
import jax.numpy as jnp
from jax import lax
from jax.experimental import pallas as pl
from jax.experimental.pallas import tpu as pltpu

EPS = 1e-6
N_HEADS = 16
QK_NOPE = 128
QK_ROPE = 64
QK_HEAD = QK_NOPE + QK_ROPE
V_HEAD = 128
Q_LORA = 512
KV_LORA = 512
MLA_WIDTH = N_HEADS * V_HEAD
ROPE_THETA = 10000.0
POOL_WINDOWS = (2, 4, 8, 16)
POOL_GROUPS = len(POOL_WINDOWS)
POOL_GROUP_DIM = 256
POOL_WIDTH = POOL_GROUPS * POOL_GROUP_DIM
POOL_HALO = 16
SMALL_WIDTH = Q_LORA + KV_LORA + 2 * QK_ROPE

V7X_VMEM_LIMIT = 56 * 1024 * 1024
ADALN_TN = 1024
NORM_TM = 512
NORM_ROW_CHUNKS = 4
GATE_TM, GATE_TN = 1024, 2048
PREP_TM = 512
ATTN_TQ_EXACT = 256
ATTN_TQ_FAST = 512
MERGE_TM = 256
BF16_SUBLANES = 16
F32_SUBLANES = 8
MIN_SOFTMAX_MASS = 2.0 ** -60

F32 = jnp.float32
BF16 = jnp.bfloat16
NT_DIMS = (((1,), (1,)), ((), ()))


def _sigmoid(v):
    return 0.5 * jnp.tanh(0.5 * v) + 0.5


def _params(semantics):
    return pltpu.CompilerParams(dimension_semantics=semantics, vmem_limit_bytes=V7X_VMEM_LIMIT)


def _adaln_kernel(ct_ref, w_ref, b_ref, o_ref):
    w = w_ref[...]
    for b in range(ct_ref.shape[1]):
        cb = ct_ref[:, b:b + 1]
        act = cb * _sigmoid(cb)
        o_ref[b:b + 1, :] = jnp.sum(w * act, axis=0, keepdims=True) + b_ref[...]


def _adaln(c, w, bias):
    B, D = c.shape
    n = w.shape[1]
    tn = ADALN_TN
    return pl.pallas_call(
        _adaln_kernel,
        grid=(n // tn,),
        in_specs=[pl.BlockSpec((D, B), lambda j: (0, 0)),
                  pl.BlockSpec((D, tn), lambda j: (0, j)),
                  pl.BlockSpec((1, tn), lambda j: (0, j))],
        out_specs=pl.BlockSpec((B, tn), lambda j: (0, j)),
        out_shape=jax.ShapeDtypeStruct((B, n), F32),
        compiler_params=_params(("arbitrary",)),
        name="adaln",
    )(c.T, w, bias.reshape(1, n))


def _norm_proj_kernel(x_ref, shift_ref, scale_ref, g_ref, ws_ref, h_ref, zs_ref, ws_bf16):
    @pl.when(pl.program_id(0) == 0)
    def _():
        ws_bf16[...] = ws_ref[...].T.astype(BF16)

    rows = x_ref.shape[0] // NORM_ROW_CHUNKS
    for c in range(NORM_ROW_CHUNKS):
        sl = slice(c * rows, (c + 1) * rows)
        x = x_ref[sl, :]
        y = x * lax.rsqrt(jnp.mean(x * x, axis=-1, keepdims=True) + EPS) * g_ref[...]
        h = (y * (1.0 + scale_ref[...]) + shift_ref[...]).astype(BF16)
        h_ref[sl, :] = h
        zs_ref[sl, :] = jnp.dot(h, ws_bf16[...], preferred_element_type=F32)


def _norm_proj(xt, mod4, norm_g, w_in_t, seq):
    T, D = xt.shape
    tm = NORM_TM
    per_b = seq // tm
    return pl.pallas_call(
        _norm_proj_kernel,
        grid=(T // tm,),
        in_specs=[pl.BlockSpec((tm, D), lambda i: (i, 0)),
                  pl.BlockSpec((None, None, 1, D), lambda i: (i // per_b, 0, 0, 0)),
                  pl.BlockSpec((None, None, 1, D), lambda i: (i // per_b, 1, 0, 0)),
                  pl.BlockSpec((1, D), lambda i: (0, 0)),
                  pl.BlockSpec((SMALL_WIDTH, D), lambda i: (0, 0), pipeline_mode=pl.Buffered(1))],
        out_specs=[pl.BlockSpec((tm, D), lambda i: (i, 0)),
                   pl.BlockSpec((tm, SMALL_WIDTH), lambda i: (i, 0))],
        out_shape=[jax.ShapeDtypeStruct((T, D), BF16),
                   jax.ShapeDtypeStruct((T, SMALL_WIDTH), F32)],
        scratch_shapes=[pltpu.VMEM((D, SMALL_WIDTH), BF16)],
        compiler_params=_params(("arbitrary",)),
        name="norm_proj",
    )(xt, mod4, mod4, norm_g.reshape(1, D), w_in_t)


def _gate_proj_kernel(h_ref, w_hbm, o_ref, w_f32, w_bf16, sem, *, tile_kinds, first_row):
    j = pl.program_id(0)
    tn = w_bf16.shape[0]

    def window_copy(tile):
        rows = pl.ds(pl.multiple_of(first_row + tile * tn, F32_SUBLANES), tn)
        return pltpu.make_async_copy(w_hbm.at[rows, :], w_f32, sem)

    @pl.when(pl.program_id(1) == 0)
    def _():
        @pl.when(j == 0)
        def _():
            window_copy(0).start()

        window_copy(j).wait()
        w_bf16[...] = w_f32[...].astype(BF16)

        @pl.when(j + 1 < pl.num_programs(0))
        def _():
            window_copy(j + 1).start()

    def branch_of(kinds):
        return tuple("sigmoid" if k == "sigmoid" else "silu" for k in kinds)

    for branch in sorted(set(map(branch_of, tile_kinds))):
        tiles = [t for t, k in enumerate(tile_kinds) if branch_of(k) == branch]
        cond = functools.reduce(jnp.logical_or, [j == t for t in tiles])

        @pl.when(cond)
        def _(branch=branch, tiles=tiles):
            acc = lax.dot_general(h_ref[...], w_bf16[...], NT_DIMS, preferred_element_type=F32)
            width = tn // len(branch)
            for s, kind in enumerate(branch):
                cols = slice(s * width, (s + 1) * width)
                a = acc[:, cols]
                sg = _sigmoid(a)
                identity_tiles = [t for t in tiles if tile_kinds[t][s] == "linear"]
                if kind == "sigmoid":
                    out = sg
                elif identity_tiles:
                    is_identity = functools.reduce(jnp.logical_or, [j == t for t in identity_tiles])
                    out = a * jnp.where(is_identity, 1.0, sg)
                else:
                    out = a * sg
                o_ref[:, cols] = out.astype(o_ref.dtype)


def _gate_proj(h, w_in_t):
    T, D = h.shape
    tm, tn = GATE_TM, GATE_TN
    first_row = Q_LORA + KV_LORA + QK_ROPE
    n = w_in_t.shape[0] - first_row
    assert n % tn == 0 and first_row % F32_SUBLANES == 0 and tn % F32_SUBLANES == 0
    group_kinds = (("silu",) * (MLA_WIDTH // POOL_WIDTH) + ("linear", "silu")
                   + ("sigmoid",) * (2 * D // POOL_WIDTH))
    per_tile = tn // POOL_WIDTH
    tile_kinds = tuple(group_kinds[t * per_tile:(t + 1) * per_tile] for t in range(n // tn))
    kern = functools.partial(_gate_proj_kernel, tile_kinds=tile_kinds, first_row=first_row)
    return pl.pallas_call(
        kern,
        grid=(n // tn, T // tm),
        in_specs=[pl.BlockSpec((tm, D), lambda j, i: (i, 0)),
                  pl.BlockSpec(memory_space=pl.ANY)],
        out_specs=pl.BlockSpec((tm, tn), lambda j, i: (i, j)),
        out_shape=jax.ShapeDtypeStruct((T, n), BF16),
        scratch_shapes=[pltpu.VMEM((tn, D), F32), pltpu.VMEM((tn, D), BF16),
                        pltpu.SemaphoreType.DMA(())],
        compiler_params=_params(("arbitrary", "arbitrary")),
        name="gate_proj",
    )(h, w_in_t)


def _mla_prep_kernel(zs_ref, posr_ref, invc_ref, qg_ref, kvg_ref, wuq_ref, wukv_ref,
                     qt_ref, k_ref, vt_ref, wq_ref, wk_ref, wv_ref, *, q_scale):
    @pl.when(pl.program_id(0) == 0)
    def _():
        wq_ref[...] = wuq_ref[...].T.astype(BF16)
        kv_head = QK_NOPE + V_HEAD
        for h in range(N_HEADS):
            wk_ref[:, h * QK_NOPE:(h + 1) * QK_NOPE] = (
                wukv_ref[:, h * kv_head:h * kv_head + QK_NOPE].astype(BF16))
            wv_ref[h * V_HEAD:(h + 1) * V_HEAD, :] = (
                wukv_ref[:, h * kv_head + QK_NOPE:(h + 1) * kv_head].T.astype(BF16))

    def rms(v, g):
        return (v * lax.rsqrt(jnp.mean(v * v, axis=-1, keepdims=True) + EPS) * g).astype(BF16)

    cqn = rms(zs_ref[:, 0:Q_LORA], qg_ref[...])
    ckvn = rms(zs_ref[:, Q_LORA:Q_LORA + KV_LORA], kvg_ref[...])
    kr = zs_ref[:, Q_LORA + KV_LORA:Q_LORA + KV_LORA + QK_ROPE]
    kr_sw = jnp.concatenate([kr[:, QK_ROPE // 2:], kr[:, :QK_ROPE // 2]], axis=1)

    ang_t = invc_ref[...] * posr_ref[...].astype(F32)
    cos_t, sin_t = jnp.cos(ang_t), jnp.sin(ang_t)
    cos, sin = cos_t.T, sin_t.T

    qf = lax.dot_general(wq_ref[...], cqn, NT_DIMS, preferred_element_type=F32)
    half = QK_ROPE // 2
    for h in range(N_HEADS):
        r0 = h * QK_HEAD + QK_NOPE
        qt_ref[h * QK_HEAD:r0, :] = (qf[h * QK_HEAD:r0] * q_scale).astype(BF16)
        rope = qf[r0:r0 + QK_ROPE]
        rope_sw = jnp.concatenate([rope[half:], rope[:half]], axis=0)
        qt_ref[r0:r0 + QK_ROPE, :] = ((rope * cos_t + rope_sw * sin_t) * q_scale).astype(BF16)

    kn = jnp.dot(ckvn, wk_ref[...], preferred_element_type=F32)
    k_rot = (kr * cos + kr_sw * sin).astype(BF16)
    for h in range(N_HEADS):
        k_ref[h, :, 0:QK_NOPE] = kn[:, h * QK_NOPE:(h + 1) * QK_NOPE].astype(BF16)
        k_ref[h, :, QK_NOPE:QK_HEAD] = k_rot

    vt_ref[...] = lax.dot_general(wv_ref[...], ckvn, NT_DIMS,
                                  preferred_element_type=F32).astype(BF16)


def _mla_prep(zs, posr, inv_signed, q_norm_g, kv_norm_g, w_uq, w_ukv, batch, seq, q_scale):
    tm = PREP_TM
    per_b = seq // tm
    const = lambda i: (0, 0)
    resident = functools.partial(pl.BlockSpec, pipeline_mode=pl.Buffered(1))
    kern = functools.partial(_mla_prep_kernel, q_scale=q_scale)
    return pl.pallas_call(
        kern,
        grid=(batch * per_b,),
        in_specs=[pl.BlockSpec((tm, SMALL_WIDTH), lambda i: (i, 0)),
                  pl.BlockSpec((None, 1, tm), lambda i: (i // per_b, 0, i % per_b)),
                  pl.BlockSpec((QK_ROPE, 1), const),
                  pl.BlockSpec((1, Q_LORA), const),
                  pl.BlockSpec((1, KV_LORA), const),
                  resident(w_uq.shape, const),
                  resident(w_ukv.shape, const)],
        out_specs=[pl.BlockSpec((None, N_HEADS * QK_HEAD, tm), lambda i: (i // per_b, 0, i % per_b)),
                   pl.BlockSpec((None, N_HEADS, tm, QK_HEAD), lambda i: (i // per_b, 0, i % per_b, 0)),
                   pl.BlockSpec((None, MLA_WIDTH, tm), lambda i: (i // per_b, 0, i % per_b))],
        out_shape=[jax.ShapeDtypeStruct((batch, N_HEADS * QK_HEAD, seq), BF16),
                   jax.ShapeDtypeStruct((batch, N_HEADS, seq, QK_HEAD), BF16),
                   jax.ShapeDtypeStruct((batch, MLA_WIDTH, seq), BF16)],
        scratch_shapes=[pltpu.VMEM((N_HEADS * QK_HEAD, Q_LORA), BF16),
                        pltpu.VMEM((KV_LORA, N_HEADS * QK_NOPE), BF16),
                        pltpu.VMEM((MLA_WIDTH, KV_LORA), BF16)],
        compiler_params=_params(("arbitrary",)),
        name="mla_prep",
    )(zs, posr, inv_signed.reshape(QK_ROPE, 1),
      q_norm_g.reshape(1, Q_LORA), kv_norm_g.reshape(1, KV_LORA), w_uq, w_ukv)


def _attention_kernel(qt_ref, k_ref, knext_ref, vt_ref, *refs, n_cast):
    cast_in, (o_ref, *cast_out) = refs[:n_cast], refs[n_cast:2 * n_cast + 1]
    lmin_ref, ksq_ref = refs[2 * n_cast + 1:]
    for w_in, w_out in zip(cast_in, cast_out):
        w_out[...] = w_in[...].astype(w_out.dtype)

    def max_sq_norm(rows):
        rf = rows.astype(F32)
        return jnp.max(jnp.sum(rf * rf, axis=1, keepdims=True), axis=0, keepdims=True)

    tq = lmin_ref.shape[1]
    nq = qt_ref.shape[1] // tq

    @pl.when(jnp.logical_and(pl.program_id(0) == 0, pl.program_id(1) == 0))
    def _():
        ksq_ref[...] = max_sq_norm(k_ref[...])

    k_norm = jnp.sqrt(ksq_ref[...])
    lmin_ref[...] = jnp.full(lmin_ref.shape, jnp.inf, F32)

    def fast_tile(i, next_ksq):
        off = pl.multiple_of(i * tq, tq)
        next_ksq = jnp.maximum(next_ksq, max_sq_norm(knext_ref[pl.ds(off, tq), :]))
        qt = qt_ref[:, pl.ds(off, tq)]
        qf = qt.astype(F32)
        shift = jnp.sqrt(jnp.sum(qf * qf, axis=0, keepdims=True)) * k_norm
        s = jnp.dot(k_ref[...], qt, preferred_element_type=F32)
        p = jnp.exp2(s - shift)
        l = jnp.sum(p, axis=0, keepdims=True)
        ot = jnp.dot(vt_ref[...], p.astype(BF16), preferred_element_type=F32)
        o_ref[pl.ds(off, tq), :] = (ot / l).T.astype(o_ref.dtype)
        lmin_ref[...] = jnp.minimum(lmin_ref[...], l)
        return next_ksq

    ksq_ref[...] = lax.fori_loop(0, nq, fast_tile, jnp.zeros((1, 1), F32), unroll=True)
    trusted = jnp.min(lmin_ref[...]) >= MIN_SOFTMAX_MASS

    @pl.when(jnp.logical_not(trusted))
    def _():
        _attention_exact(qt_ref, k_ref, vt_ref, o_ref)


def _attention_exact(qt_ref, k_ref, vt_ref, o_ref):
    tq = ATTN_TQ_EXACT
    nq = qt_ref.shape[1] // tq

    def tile(i, carry):
        off = pl.multiple_of(i * tq, tq)
        s = jnp.dot(k_ref[...], qt_ref[:, pl.ds(off, tq)], preferred_element_type=F32)
        p = jnp.exp2(s - jnp.max(s, axis=0, keepdims=True))
        l = jnp.sum(p, axis=0, keepdims=True)
        ot = jnp.dot(vt_ref[...], p.astype(BF16), preferred_element_type=F32)
        o_ref[pl.ds(off, tq), :] = (ot / l).T.astype(o_ref.dtype)
        return carry

    lax.fori_loop(0, nq, tile, 0)


def _attention(qt, k, vt, batch, seq, later_weights):
    tq_fast = ATTN_TQ_FAST
    assert seq % ATTN_TQ_EXACT == 0 and seq % tq_fast == 0
    steps = batch * N_HEADS
    slabs = [w.shape[0] // steps for w in later_weights]
    assert all(s % BF16_SUBLANES == 0 and s * steps == w.shape[0]
               for s, w in zip(slabs, later_weights))
    step = lambda b, h: (b * N_HEADS + h, 0)
    cast_specs = [pl.BlockSpec((s, w.shape[1]), step) for s, w in zip(slabs, later_weights)]

    def next_head(b, h):
        t = jnp.minimum(b * N_HEADS + h + 1, steps - 1)
        return t // N_HEADS, t % N_HEADS, 0, 0

    outs = pl.pallas_call(
        functools.partial(_attention_kernel, n_cast=len(later_weights)),
        grid=(batch, N_HEADS),
        in_specs=[pl.BlockSpec((None, QK_HEAD, seq), lambda b, h: (b, h, 0)),
                  pl.BlockSpec((None, None, seq, QK_HEAD), lambda b, h: (b, h, 0, 0)),
                  pl.BlockSpec((None, None, seq, QK_HEAD), next_head),
                  pl.BlockSpec((None, V_HEAD, seq), lambda b, h: (b, h, 0))] + cast_specs,
        out_specs=[pl.BlockSpec((None, seq, V_HEAD), lambda b, h: (b, 0, h))] + cast_specs,
        out_shape=[jax.ShapeDtypeStruct((batch, seq, MLA_WIDTH), BF16)]
                  + [jax.ShapeDtypeStruct(w.shape, BF16) for w in later_weights],
        scratch_shapes=[pltpu.VMEM((1, tq_fast), F32), pltpu.VMEM((1, 1), F32)],
        compiler_params=_params(("arbitrary", "arbitrary")),
        name="attention",
    )(qt, k, k, vt, *later_weights)
    return outs[0], outs[1:]


def _merge_out_kernel(x_ref, gate_ref, attn_ref, gm_ref, gp_ref, vp_ref, vprev_ref, vnext_ref,
                      mm_ref, mp_ref, pw_ref, ps_ref, wop_ref, wom_ref, wout_ref, fg_ref, o_ref,
                      *, seq, final_norm):
    tm = x_ref.shape[0]
    t0 = (pl.program_id(0) % (seq // tm)) * tm

    gated = attn_ref[...] * gm_ref[...]

    cur = vp_ref[...].astype(F32)
    prev = jnp.where(t0 > 0, vprev_ref[...].astype(F32), 0.0)
    nxt = jnp.where(t0 + tm < seq, vnext_ref[...].astype(F32), 0.0)
    ext = jnp.concatenate([prev, cur, nxt], axis=0)
    n_ext = tm + 2 * POOL_HALO
    tok = t0 + lax.broadcasted_iota(jnp.int32, (tm, 1), 0)
    mixed = []
    p_mla = []
    mla_cols = wom_ref.shape[1] // POOL_GROUPS
    for g, w in enumerate(POOL_WINDOWS):
        p_mla.append(jnp.dot(gated, wom_ref[:, g * mla_cols:(g + 1) * mla_cols],
                             preferred_element_type=F32))
        acc = ext[:, g * POOL_GROUP_DIM:(g + 1) * POOL_GROUP_DIM]
        acc = acc + pltpu.roll(acc, 1, axis=0)
        half = 1
        while 2 * half < w:
            acc = pltpu.roll(acc, half, axis=0) + pltpu.roll(acc, n_ext - half, axis=0)
            half *= 2
        wsum = acc[POOL_HALO:POOL_HALO + tm]
        count = (jnp.minimum(tok + w // 2, seq) - jnp.maximum(tok - w // 2, 0)).astype(F32)
        pooled = wsum / count - cur[:, g * POOL_GROUP_DIM:(g + 1) * POOL_GROUP_DIM]
        mixed.append(jnp.dot(pooled.astype(BF16), pw_ref[g], preferred_element_type=F32))
    mixed = jnp.concatenate(mixed, axis=1)
    p_mla = jnp.concatenate(p_mla, axis=1)
    u = (mixed * ps_ref[...] * gp_ref[...].astype(F32)).astype(BF16)
    p_pool = jnp.dot(u, wop_ref[...], preferred_element_type=F32)

    y = mm_ref[...].astype(F32) * p_mla + mp_ref[...].astype(F32) * p_pool
    r = jnp.dot(y.astype(BF16), wout_ref[...], preferred_element_type=F32)
    xo = x_ref[...] + gate_ref[...] * r
    if final_norm:
        xo = xo * lax.rsqrt(jnp.mean(xo * xo, axis=-1, keepdims=True) + EPS) * fg_ref[...]
    o_ref[...] = xo


def _merge_out(xt, mod4, attn, zbig, pool_w, pool_scale, w_o_pool, w_o_mla, w_out, final_g,
               seq, final_norm):
    T, D = xt.shape
    tm = MERGE_TM
    per_b = seq // tm
    halo_per_tile = tm // POOL_HALO
    n_halo = T // POOL_HALO
    vp_blk = MLA_WIDTH // POOL_WIDTH
    gp_blk = vp_blk + 1
    mm_blk = (MLA_WIDTH + 2 * POOL_WIDTH) // D
    resident = functools.partial(pl.BlockSpec, pipeline_mode=pl.Buffered(1))
    kern = functools.partial(_merge_out_kernel, seq=seq, final_norm=final_norm)
    return pl.pallas_call(
        kern,
        grid=(T // tm,),
        in_specs=[pl.BlockSpec((tm, D), lambda i: (i, 0)),
                  pl.BlockSpec((None, None, 1, D), lambda i: (i // per_b, 2, 0, 0)),
                  pl.BlockSpec((tm, MLA_WIDTH), lambda i: (i, 0)),
                  pl.BlockSpec((tm, MLA_WIDTH), lambda i: (i, 0)),
                  pl.BlockSpec((tm, POOL_WIDTH), lambda i: (i, gp_blk)),
                  pl.BlockSpec((tm, POOL_WIDTH), lambda i: (i, vp_blk)),
                  pl.BlockSpec((POOL_HALO, POOL_WIDTH),
                               lambda i: (jnp.maximum(i * halo_per_tile - 1, 0), vp_blk)),
                  pl.BlockSpec((POOL_HALO, POOL_WIDTH),
                               lambda i: (jnp.minimum((i + 1) * halo_per_tile, n_halo - 1), vp_blk)),
                  pl.BlockSpec((tm, D), lambda i: (i, mm_blk)),
                  pl.BlockSpec((tm, D), lambda i: (i, mm_blk + 1)),
                  resident(pool_w.shape, lambda i: (0, 0, 0)),
                  resident((1, POOL_WIDTH), lambda i: (0, 0)),
                  resident(w_o_pool.shape, lambda i: (0, 0)),
                  resident(w_o_mla.shape, lambda i: (0, 0)),
                  resident(w_out.shape, lambda i: (0, 0)),
                  resident((1, D), lambda i: (0, 0))],
        out_specs=pl.BlockSpec((tm, D), lambda i: (i, 0)),
        out_shape=jax.ShapeDtypeStruct((T, D), F32),
        compiler_params=_params(("arbitrary",)),
        name="merge_out",
    )(xt, mod4, attn, zbig, zbig, zbig, zbig, zbig, zbig, zbig,
      pool_w, pool_scale.reshape(1, POOL_WIDTH), w_o_pool, w_o_mla, w_out, final_g.reshape(1, D))


def kernel(x, c, positions, ada_w, ada_b, norm_g, w_in, q_norm_g, w_uq, kv_norm_g, w_ukv, w_o_mla,
           pool_w, pool_scale, w_o_pool, w_out, final_g):
    B, S, D = x.shape
    depth = ada_w.shape[0]
    inv_freq = 1.0 / (ROPE_THETA ** (jnp.arange(0, QK_ROPE, 2, dtype=F32) / QK_ROPE))
    inv_signed = jnp.concatenate([-inv_freq, inv_freq])
    posr = positions.reshape(B, 1, S)
    q_scale = QK_HEAD ** -0.5 * math.log2(math.e)

    xt = x.reshape(B * S, D)
    for l in range(depth):
        mod4 = _adaln(c, ada_w[l], ada_b[l]).reshape(B, 3, 1, D)
        w_in_t = w_in[l].T
        h, zs = _norm_proj(xt, mod4, norm_g[l], w_in_t, S)
        zbig = _gate_proj(h, w_in_t)
        qt, k, vt = _mla_prep(zs, posr, inv_signed, q_norm_g[l], kv_norm_g[l],
                              w_uq[l], w_ukv[l], B, S, q_scale)
        attn, (pw, wop, wom, wout) = _attention(
            qt, k, vt, B, S,
            (pool_w[l].reshape(POOL_WIDTH, POOL_GROUP_DIM), w_o_pool[l], w_o_mla[l], w_out[l]))
        xt = _merge_out(xt, mod4, attn.reshape(B * S, MLA_WIDTH), zbig,
                        pw.reshape(POOL_GROUPS, POOL_GROUP_DIM, POOL_GROUP_DIM), pool_scale[l],
                        wop, wom, wout, final_g, S, final_norm=(l == depth - 1))
    return xt.reshape(B, S, D)
```

```python
import functools
import math

import jax
import jax.numpy as jnp
from jax import lax
from jax.experimental import pallas as pl
from jax.experimental.pallas import tpu as pltpu

EPS = 1e-6
N_HEADS = 16
QK_NOPE = 128
QK_ROPE = 64
QK_HEAD = QK_NOPE + QK_ROPE
V_HEAD = 128
Q_LORA = 512
KV_LORA = 512
MLA_WIDTH = N_HEADS * V_HEAD
ROPE_THETA = 10000.0
POOL_WINDOWS = (2, 4, 8, 16)
POOL_GROUPS = len(POOL_WINDOWS)
POOL_GROUP_DIM = 256
POOL_WIDTH = POOL_GROUPS * POOL_GROUP_DIM
POOL_HALO = 16
SMALL_WIDTH = Q_LORA + KV_LORA + 2 * QK_ROPE

V7X_VMEM_LIMIT = 56 * 1024 * 1024
ADALN_TN = 1024
NORM_TM = 512
NORM_ROW_CHUNKS = 4
GATE_TM, GATE_TN = 512, 2048
PREP_TM = 512
ATTN_TQ_EXACT = 256
ATTN_TQ_FAST = 512
MERGE_TM = 256
BF16_SUBLANES = 16
F32_SUBLANES = 8
MIN_SOFTMAX_MASS = 2.0 ** -60

F32 = jnp.float32
BF16 = jnp.bfloat16
NT_DIMS = (((1,), (1,)), ((), ()))


def _sigmoid(v):
    return 0.5 * jnp.tanh(0.5 * v) + 0.5


def _params(semantics):
    return pltpu.CompilerParams(dimension_semantics=semantics, vmem_limit_bytes=V7X_VMEM_LIMIT)


def _adaln_kernel(ct_ref, w_ref, b_ref, o_ref):
    w = w_ref[...]
    for b in range(ct_ref.shape[1]):
        cb = ct_ref[:, b:b + 1]
        act = cb * _sigmoid(cb)
        o_ref[b:b + 1, :] = jnp.sum(w * act, axis=0, keepdims=True) + b_ref[...]


def _adaln(c, w, bias):
    B, D = c.shape
    n = w.shape[1]
    tn = ADALN_TN
    return pl.pallas_call(
        _adaln_kernel,
        grid=(n // tn,),
        in_specs=[pl.BlockSpec((D, B), lambda j: (0, 0)),
                  pl.BlockSpec((D, tn), lambda j: (0, j)),
                  pl.BlockSpec((1, tn), lambda j: (0, j))],
        out_specs=pl.BlockSpec((B, tn), lambda j: (0, j)),
        out_shape=jax.ShapeDtypeStruct((B, n), F32),
        compiler_params=_params(("arbitrary",)),
        name="adaln",
    )(c.T, w, bias.reshape(1, n))


def _norm_proj_kernel(x_ref, shift_ref, scale_ref, g_ref, ws_ref, h_ref, zs_ref, ws_bf16):
    @pl.when(pl.program_id(0) == 0)
    def _():
        ws_bf16[...] = ws_ref[...].T.astype(BF16)

    rows = x_ref.shape[0] // NORM_ROW_CHUNKS
    for c in range(NORM_ROW_CHUNKS):
        sl = slice(c * rows, (c + 1) * rows)
        x = x_ref[sl, :]
        y = x * lax.rsqrt(jnp.mean(x * x, axis=-1, keepdims=True) + EPS) * g_ref[...]
        h = (y * (1.0 + scale_ref[...]) + shift_ref[...]).astype(BF16)
        h_ref[sl, :] = h
        zs_ref[sl, :] = jnp.dot(h, ws_bf16[...], preferred_element_type=F32)


def _norm_proj(xt, mod4, norm_g, w_in_t, seq):
    T, D = xt.shape
    tm = NORM_TM
    per_b = seq // tm
    return pl.pallas_call(
        _norm_proj_kernel,
        grid=(T // tm,),
        in_specs=[pl.BlockSpec((tm, D), lambda i: (i, 0)),
                  pl.BlockSpec((None, None, 1, D), lambda i: (i // per_b, 0, 0, 0)),
                  pl.BlockSpec((None, None, 1, D), lambda i: (i // per_b, 1, 0, 0)),
                  pl.BlockSpec((1, D), lambda i: (0, 0)),
                  pl.BlockSpec((SMALL_WIDTH, D), lambda i: (0, 0), pipeline_mode=pl.Buffered(1))],
        out_specs=[pl.BlockSpec((tm, D), lambda i: (i, 0)),
                   pl.BlockSpec((tm, SMALL_WIDTH), lambda i: (i, 0))],
        out_shape=[jax.ShapeDtypeStruct((T, D), BF16),
                   jax.ShapeDtypeStruct((T, SMALL_WIDTH), F32)],
        scratch_shapes=[pltpu.VMEM((D, SMALL_WIDTH), BF16)],
        compiler_params=_params(("arbitrary",)),
        name="norm_proj",
    )(xt, mod4, mod4, norm_g.reshape(1, D), w_in_t)


def _gate_proj_kernel(h_ref, w_hbm, o_ref, w_f32, w_bf16, sem, *, tile_kinds, first_row):
    j = pl.program_id(0)
    tn = w_bf16.shape[0]

    def window_copy(tile):
        rows = pl.ds(pl.multiple_of(first_row + tile * tn, F32_SUBLANES), tn)
        return pltpu.make_async_copy(w_hbm.at[rows, :], w_f32, sem)

    @pl.when(pl.program_id(1) == 0)
    def _():
        @pl.when(j == 0)
        def _():
            window_copy(0).start()

        window_copy(j).wait()
        w_bf16[...] = w_f32[...].astype(BF16)

        @pl.when(j + 1 < pl.num_programs(0))
        def _():
            window_copy(j + 1).start()

    def branch_of(kinds):
        return tuple("sigmoid" if k == "sigmoid" else "silu" for k in kinds)

    for branch in sorted(set(map(branch_of, tile_kinds))):
        tiles = [t for t, k in enumerate(tile_kinds) if branch_of(k) == branch]
        cond = functools.reduce(jnp.logical_or, [j == t for t in tiles])

        @pl.when(cond)
        def _(branch=branch, tiles=tiles):
            acc = lax.dot_general(h_ref[...], w_bf16[...], NT_DIMS, preferred_element_type=F32)
            width = tn // len(branch)
            for s, kind in enumerate(branch):
                cols = slice(s * width, (s + 1) * width)
                a = acc[:, cols]
                sg = _sigmoid(a)
                identity_tiles = [t for t in tiles if tile_kinds[t][s] == "linear"]
                if kind == "sigmoid":
                    out = sg
                elif identity_tiles:
                    is_identity = functools.reduce(jnp.logical_or, [j == t for t in identity_tiles])
                    out = a * jnp.where(is_identity, 1.0, sg)
                else:
                    out = a * sg
                o_ref[:, cols] = out.astype(o_ref.dtype)


def _gate_proj(h, w_in_t):
    T, D = h.shape
    tm, tn = GATE_TM, GATE_TN
    first_row = Q_LORA + KV_LORA + QK_ROPE
    n = w_in_t.shape[0] - first_row
    assert n % tn == 0 and first_row % F32_SUBLANES == 0 and tn % F32_SUBLANES == 0
    group_kinds = (("silu",) * (MLA_WIDTH // POOL_WIDTH) + ("linear", "silu")
                   + ("sigmoid",) * (2 * D // POOL_WIDTH))
    per_tile = tn // POOL_WIDTH
    tile_kinds = tuple(group_kinds[t * per_tile:(t + 1) * per_tile] for t in range(n // tn))
    kern = functools.partial(_gate_proj_kernel, tile_kinds=tile_kinds, first_row=first_row)
    return pl.pallas_call(
        kern,
        grid=(n // tn, T // tm),
        in_specs=[pl.BlockSpec((tm, D), lambda j, i: (i, 0)),
                  pl.BlockSpec(memory_space=pl.ANY)],
        out_specs=pl.BlockSpec((tm, tn), lambda j, i: (i, j)),
        out_shape=jax.ShapeDtypeStruct((T, n), BF16),
        scratch_shapes=[pltpu.VMEM((tn, D), F32), pltpu.VMEM((tn, D), BF16),
                        pltpu.SemaphoreType.DMA(())],
        compiler_params=_params(("arbitrary", "arbitrary")),
        name="gate_proj",
    )(h, w_in_t)


def _mla_prep_kernel(zs_ref, posr_ref, invc_ref, qg_ref, kvg_ref, wuq_ref, wukv_ref,
                     qt_ref, k_ref, vt_ref, wq_ref, wk_ref, wv_ref, *, q_scale):
    @pl.when(pl.program_id(0) == 0)
    def _():
        wq_ref[...] = wuq_ref[...].T.astype(BF16)
        kv_head = QK_NOPE + V_HEAD
        for h in range(N_HEADS):
            wk_ref[:, h * QK_NOPE:(h + 1) * QK_NOPE] = (
                wukv_ref[:, h * kv_head:h * kv_head + QK_NOPE].astype(BF16))
            wv_ref[h * V_HEAD:(h + 1) * V_HEAD, :] = (
                wukv_ref[:, h * kv_head + QK_NOPE:(h + 1) * kv_head].T.astype(BF16))

    def rms(v, g):
        return (v * lax.rsqrt(jnp.mean(v * v, axis=-1, keepdims=True) + EPS) * g).astype(BF16)

    cqn = rms(zs_ref[:, 0:Q_LORA], qg_ref[...])
    ckvn = rms(zs_ref[:, Q_LORA:Q_LORA + KV_LORA], kvg_ref[...])
    kr = zs_ref[:, Q_LORA + KV_LORA:Q_LORA + KV_LORA + QK_ROPE]
    kr_sw = jnp.concatenate([kr[:, QK_ROPE // 2:], kr[:, :QK_ROPE // 2]], axis=1)

    ang_t = invc_ref[...] * posr_ref[...].astype(F32)
    cos_t, sin_t = jnp.cos(ang_t), jnp.sin(ang_t)
    cos, sin = cos_t.T, sin_t.T

    qf = lax.dot_general(wq_ref[...], cqn, NT_DIMS, preferred_element_type=F32)
    half = QK_ROPE // 2
    for h in range(N_HEADS):
        r0 = h * QK_HEAD + QK_NOPE
        qt_ref[h * QK_HEAD:r0, :] = (qf[h * QK_HEAD:r0] * q_scale).astype(BF16)
        rope = qf[r0:r0 + QK_ROPE]
        rope_sw = jnp.concatenate([rope[half:], rope[:half]], axis=0)
        qt_ref[r0:r0 + QK_ROPE, :] = ((rope * cos_t + rope_sw * sin_t) * q_scale).astype(BF16)

    kn = jnp.dot(ckvn, wk_ref[...], preferred_element_type=F32)
    k_rot = (kr * cos + kr_sw * sin).astype(BF16)
    for h in range(N_HEADS):
        k_ref[h, :, 0:QK_NOPE] = kn[:, h * QK_NOPE:(h + 1) * QK_NOPE].astype(BF16)
        k_ref[h, :, QK_NOPE:QK_HEAD] = k_rot

    vt_ref[...] = lax.dot_general(wv_ref[...], ckvn, NT_DIMS,
                                  preferred_element_type=F32).astype(BF16)


def _mla_prep(zs, posr, inv_signed, q_norm_g, kv_norm_g, w_uq, w_ukv, batch, seq, q_scale):
    tm = PREP_TM
    per_b = seq // tm
    const = lambda i: (0, 0)
    resident = functools.partial(pl.BlockSpec, pipeline_mode=pl.Buffered(1))
    kern = functools.partial(_mla_prep_kernel, q_scale=q_scale)
    return pl.pallas_call(
        kern,
        grid=(batch * per_b,),
        in_specs=[pl.BlockSpec((tm, SMALL_WIDTH), lambda i: (i, 0)),
                  pl.BlockSpec((None, 1, tm), lambda i: (i // per_b, 0, i % per_b)),
                  pl.BlockSpec((QK_ROPE, 1), const),
                  pl.BlockSpec((1, Q_LORA), const),
                  pl.BlockSpec((1, KV_LORA), const),
                  resident(w_uq.shape, const),
                  resident(w_ukv.shape, const)],
        out_specs=[pl.BlockSpec((None, N_HEADS * QK_HEAD, tm), lambda i: (i // per_b, 0, i % per_b)),
                   pl.BlockSpec((None, N_HEADS, tm, QK_HEAD), lambda i: (i // per_b, 0, i % per_b, 0)),
                   pl.BlockSpec((None, MLA_WIDTH, tm), lambda i: (i // per_b, 0, i % per_b))],
        out_shape=[jax.ShapeDtypeStruct((batch, N_HEADS * QK_HEAD, seq), BF16),
                   jax.ShapeDtypeStruct((batch, N_HEADS, seq, QK_HEAD), BF16),
                   jax.ShapeDtypeStruct((batch, MLA_WIDTH, seq), BF16)],
        scratch_shapes=[pltpu.VMEM((N_HEADS * QK_HEAD, Q_LORA), BF16),
                        pltpu.VMEM((KV_LORA, N_HEADS * QK_NOPE), BF16),
                        pltpu.VMEM((MLA_WIDTH, KV_LORA), BF16)],
        compiler_params=_params(("arbitrary",)),
        name="mla_prep",
    )(zs, posr, inv_signed.reshape(QK_ROPE, 1),
      q_norm_g.reshape(1, Q_LORA), kv_norm_g.reshape(1, KV_LORA), w_uq, w_ukv)


def _attention_kernel(qt_ref, k_ref, knext_ref, vt_ref, *refs, n_cast):
    cast_in, (o_ref, *cast_out) = refs[:n_cast], refs[n_cast:2 * n_cast + 1]
    lmin_ref, ksq_ref = refs[2 * n_cast + 1:]
    for w_in, w_out in zip(cast_in, cast_out):
        w_out[...] = w_in[...].astype(w_out.dtype)

    def max_sq_norm(rows):
        rf = rows.astype(F32)
        return jnp.max(jnp.sum(rf * rf, axis=1, keepdims=True), axis=0, keepdims=True)

    tq = lmin_ref.shape[1]
    nq = qt_ref.shape[1] // tq

    @pl.when(jnp.logical_and(pl.program_id(0) == 0, pl.program_id(1) == 0))
    def _():
        ksq_ref[...] = max_sq_norm(k_ref[...])

    k_norm = jnp.sqrt(ksq_ref[...])
    lmin_ref[...] = jnp.full(lmin_ref.shape, jnp.inf, F32)

    def fast_tile(i, next_ksq):
        off = pl.multiple_of(i * tq, tq)
        next_ksq = jnp.maximum(next_ksq, max_sq_norm(knext_ref[pl.ds(off, tq), :]))
        qt = qt_ref[:, pl.ds(off, tq)]
        qf = qt.astype(F32)
        shift = jnp.sqrt(jnp.sum(qf * qf, axis=0, keepdims=True)) * k_norm
        s = jnp.dot(k_ref[...], qt, preferred_element_type=F32)
        p = jnp.exp2(s - shift)
        l = jnp.sum(p, axis=0, keepdims=True)
        ot = jnp.dot(vt_ref[...], p.astype(BF16), preferred_element_type=F32)
        o_ref[pl.ds(off, tq), :] = (ot / l).T.astype(o_ref.dtype)
        lmin_ref[...] = jnp.minimum(lmin_ref[...], l)
        return next_ksq

    ksq_ref[...] = lax.fori_loop(0, nq, fast_tile, jnp.zeros((1, 1), F32), unroll=True)
    trusted = jnp.min(lmin_ref[...]) >= MIN_SOFTMAX_MASS

    @pl.when(jnp.logical_not(trusted))
    def _():
        _attention_exact(qt_ref, k_ref, vt_ref, o_ref)


def _attention_exact(qt_ref, k_ref, vt_ref, o_ref):
    tq = ATTN_TQ_EXACT
    nq = qt_ref.shape[1] // tq

    def tile(i, carry):
        off = pl.multiple_of(i * tq, tq)
        s = jnp.dot(k_ref[...], qt_ref[:, pl.ds(off, tq)], preferred_element_type=F32)
        p = jnp.exp2(s - jnp.max(s, axis=0, keepdims=True))
        l = jnp.sum(p, axis=0, keepdims=True)
        ot = jnp.dot(vt_ref[...], p.astype(BF16), preferred_element_type=F32)
        o_ref[pl.ds(off, tq), :] = (ot / l).T.astype(o_ref.dtype)
        return carry

    lax.fori_loop(0, nq, tile, 0)


def _attention(qt, k, vt, batch, seq, later_weights):
    tq_fast = ATTN_TQ_FAST
    assert seq % ATTN_TQ_EXACT == 0 and seq % tq_fast == 0
    steps = batch * N_HEADS
    slabs = [w.shape[0] // steps for w in later_weights]
    assert all(s % BF16_SUBLANES == 0 and s * steps == w.shape[0]
               for s, w in zip(slabs, later_weights))
    step = lambda b, h: (b * N_HEADS + h, 0)
    cast_specs = [pl.BlockSpec((s, w.shape[1]), step) for s, w in zip(slabs, later_weights)]

    def next_head(b, h):
        t = jnp.minimum(b * N_HEADS + h + 1, steps - 1)
        return t // N_HEADS, t % N_HEADS, 0, 0

    outs = pl.pallas_call(
        functools.partial(_attention_kernel, n_cast=len(later_weights)),
        grid=(batch, N_HEADS),
        in_specs=[pl.BlockSpec((None, QK_HEAD, seq), lambda b, h: (b, h, 0)),
                  pl.BlockSpec((None, None, seq, QK_HEAD), lambda b, h: (b, h, 0, 0)),
                  pl.BlockSpec((None, None, seq, QK_HEAD), next_head),
                  pl.BlockSpec((None, V_HEAD, seq), lambda b, h: (b, h, 0))] + cast_specs,
        out_specs=[pl.BlockSpec((None, seq, V_HEAD), lambda b, h: (b, 0, h))] + cast_specs,
        out_shape=[jax.ShapeDtypeStruct((batch, seq, MLA_WIDTH), BF16)]
                  + [jax.ShapeDtypeStruct(w.shape, BF16) for w in later_weights],
        scratch_shapes=[pltpu.VMEM((1, tq_fast), F32), pltpu.VMEM((1, 1), F32)],
        compiler_params=_params(("arbitrary", "arbitrary")),
        name="attention",
    )(qt, k, k, vt, *later_weights)
    return outs[0], outs[1:]


def _merge_out_kernel(x_ref, gate_ref, attn_ref, gm_ref, gp_ref, vp_ref, vprev_ref, vnext_ref,
                      mm_ref, mp_ref, pw_ref, ps_ref, wop_ref, wom_ref, wout_ref, fg_ref, o_ref,
                      *, seq, final_norm):
    tm = x_ref.shape[0]
    t0 = (pl.program_id(0) % (seq // tm)) * tm

    gated = attn_ref[...] * gm_ref[...]

    cur = vp_ref[...].astype(F32)
    prev = jnp.where(t0 > 0, vprev_ref[...].astype(F32), 0.0)
    nxt = jnp.where(t0 + tm < seq, vnext_ref[...].astype(F32), 0.0)
    ext = jnp.concatenate([prev, cur, nxt], axis=0)
    n_ext = tm + 2 * POOL_HALO
    tok = t0 + lax.broadcasted_iota(jnp.int32, (tm, 1), 0)
    mixed = []
    p_mla = []
    mla_cols = wom_ref.shape[1] // POOL_GROUPS
    for g, w in enumerate(POOL_WINDOWS):
        p_mla.append(jnp.dot(gated, wom_ref[:, g * mla_cols:(g + 1) * mla_cols],
                             preferred_element_type=F32))
        acc = ext[:, g * POOL_GROUP_DIM:(g + 1) * POOL_GROUP_DIM]
        acc = acc + pltpu.roll(acc, 1, axis=0)
        half = 1
        while 2 * half < w:
            acc = pltpu.roll(acc, half, axis=0) + pltpu.roll(acc, n_ext - half, axis=0)
            half *= 2
        wsum = acc[POOL_HALO:POOL_HALO + tm]
        count = (jnp.minimum(tok + w // 2, seq) - jnp.maximum(tok - w // 2, 0)).astype(F32)
        pooled = wsum / count - cur[:, g * POOL_GROUP_DIM:(g + 1) * POOL_GROUP_DIM]
        mixed.append(jnp.dot(pooled.astype(BF16), pw_ref[g], preferred_element_type=F32))
    mixed = jnp.concatenate(mixed, axis=1)
    p_mla = jnp.concatenate(p_mla, axis=1)
    u = (mixed * ps_ref[...] * gp_ref[...].astype(F32)).astype(BF16)
    p_pool = jnp.dot(u, wop_ref[...], preferred_element_type=F32)

    y = mm_ref[...].astype(F32) * p_mla + mp_ref[...].astype(F32) * p_pool
    r = jnp.dot(y.astype(BF16), wout_ref[...], preferred_element_type=F32)
    xo = x_ref[...] + gate_ref[...] * r
    if final_norm:
        xo = xo * lax.rsqrt(jnp.mean(xo * xo, axis=-1, keepdims=True) + EPS) * fg_ref[...]
    o_ref[...] = xo


def _merge_out(xt, mod4, attn, zbig, pool_w, pool_scale, w_o_pool, w_o_mla, w_out, final_g,
               seq, final_norm):
    T, D = xt.shape
    tm = MERGE_TM
    per_b = seq // tm
    halo_per_tile = tm // POOL_HALO
    n_halo = T // POOL_HALO
    vp_blk = MLA_WIDTH // POOL_WIDTH
    gp_blk = vp_blk + 1
    mm_blk = (MLA_WIDTH + 2 * POOL_WIDTH) // D
    resident = functools.partial(pl.BlockSpec, pipeline_mode=pl.Buffered(1))
    kern = functools.partial(_merge_out_kernel, seq=seq, final_norm=final_norm)
    return pl.pallas_call(
        kern,
        grid=(T // tm,),
        in_specs=[pl.BlockSpec((tm, D), lambda i: (i, 0)),
                  pl.BlockSpec((None, None, 1, D), lambda i: (i // per_b, 2, 0, 0)),
                  pl.BlockSpec((tm, MLA_WIDTH), lambda i: (i, 0)),
                  pl.BlockSpec((tm, MLA_WIDTH), lambda i: (i, 0)),
                  pl.BlockSpec((tm, POOL_WIDTH), lambda i: (i, gp_blk)),
                  pl.BlockSpec((tm, POOL_WIDTH), lambda i: (i, vp_blk)),
                  pl.BlockSpec((POOL_HALO, POOL_WIDTH),
                               lambda i: (jnp.maximum(i * halo_per_tile - 1, 0), vp_blk)),
                  pl.BlockSpec((POOL_HALO, POOL_WIDTH),
                               lambda i: (jnp.minimum((i + 1) * halo_per_tile, n_halo - 1), vp_blk)),
                  pl.BlockSpec((tm, D), lambda i: (i, mm_blk)),
                  pl.BlockSpec((tm, D), lambda i: (i, mm_blk + 1)),
                  resident(pool_w.shape, lambda i: (0, 0, 0)),
                  resident((1, POOL_WIDTH), lambda i: (0, 0)),
                  resident(w_o_pool.shape, lambda i: (0, 0)),
                  resident(w_o_mla.shape, lambda i: (0, 0)),
                  resident(w_out.shape, lambda i: (0, 0)),
                  resident((1, D), lambda i: (0, 0))],
        out_specs=pl.BlockSpec((tm, D), lambda i: (i, 0)),
        out_shape=jax.ShapeDtypeStruct((T, D), F32),
        compiler_params=_params(("arbitrary",)),
        name="merge_out",
    )(xt, mod4, attn, zbig, zbig, zbig, zbig, zbig, zbig, zbig,
      pool_w, pool_scale.reshape(1, POOL_WIDTH), w_o_pool, w_o_mla, w_out, final_g.reshape(1, D))


def kernel(x, c, positions, ada_w, ada_b, norm_g, w_in, q_norm_g, w_uq, kv_norm_g, w_ukv, w_o_mla,
           pool_w, pool_scale, w_o_pool, w_out, final_g):
    B, S, D = x.shape
    depth = ada_w.shape[0]
    inv_freq = 1.0 / (ROPE_THETA ** (jnp.arange(0, QK_ROPE, 2, dtype=F32) / QK_ROPE))
    inv_signed = jnp.concatenate([-inv_freq, inv_freq])
    posr = positions.reshape(B, 1, S)
    q_scale = QK_HEAD ** -0.5 * math.log2(math.e)

    xt = x.reshape(B * S, D)
    for l in range(depth):
        mod4 = _adaln(c, ada_w[l], ada_b[l]).reshape(B, 3, 1, D)
        w_in_t = w_in[l].T
        h, zs = _norm_proj(xt, mod4, norm_g[l], w_in_t, S)
        zbig = _gate_proj(h, w_in_t)
        qt, k, vt = _mla_prep(zs, posr, inv_signed, q_norm_g[l], kv_norm_g[l],
                              w_uq[l], w_ukv[l], B, S, q_scale)
        attn, (pw, wop, wom, wout) = _attention(
            qt, k, vt, B, S,
            (pool_w[l].reshape(POOL_WIDTH, POOL_GROUP_DIM), w_o_pool[l], w_o_mla[l], w_out[l]))
        xt = _merge_out(xt, mod4, attn.reshape(B * S, MLA_WIDTH), zbig,
                        pw.reshape(POOL_GROUPS, POOL_GROUP_DIM, POOL_GROUP_DIM), pool_scale[l],
                        wop, wom, wout, final_g, S, final_norm=(l == depth - 1))
    return xt.reshape(B, S, D)
```

```python
import functools
import math

import jax
import jax.numpy as jnp
from jax import lax
from jax.experimental import pallas as pl
from jax.experimental.pallas import tpu as pltpu

EPS = 1e-6
N_HEADS = 16
QK_NOPE = 128
QK_ROPE = 64
QK_HEAD = QK_NOPE + QK_ROPE
V_HEAD = 128
Q_LORA = 512
KV_LORA = 512
MLA_WIDTH = N_HEADS * V_HEAD
ROPE_THETA = 10000.0
POOL_WINDOWS = (2, 4, 8, 16)
POOL_GROUPS = len(POOL_WINDOWS)
POOL_GROUP_DIM = 256
POOL_WIDTH = POOL_GROUPS * POOL_GROUP_DIM
POOL_HALO = 16
SMALL_WIDTH = Q_LORA + KV_LORA + 2 * QK_ROPE

V7X_VMEM_LIMIT = 56 * 1024 * 1024
ADALN_TN = 1024
NORM_TM = 512
NORM_ROW_CHUNKS = 4
GATE_TM, GATE_TN = 1024, 2048
PREP_TM = 512
ATTN_TQ_EXACT = 256
ATTN_TQ_FAST = 512
MERGE_TM = 256
BF16_SUBLANES = 16
F32_SUBLANES = 8
MIN_SOFTMAX_MASS = 2.0 ** -60

F32 = jnp.float32
BF16 = jnp.bfloat16
NT_DIMS = (((1,), (1,)), ((), ()))


def _sigmoid(v):
    return 0.5 * jnp.tanh(0.5 * v) + 0.5


def _params(semantics):
    return pltpu.CompilerParams(dimension_semantics=semantics, vmem_limit_bytes=V7X_VMEM_LIMIT)


def _adaln_kernel(ct_ref, w_ref, b_ref, o_ref):
    w = w_ref[...]
    for b in range(ct_ref.shape[1]):
        cb = ct_ref[:, b:b + 1]
        act = cb * _sigmoid(cb)
        o_ref[b:b + 1, :] = jnp.sum(w * act, axis=0, keepdims=True) + b_ref[...]


def _adaln(c, w, bias):
    B, D = c.shape
    n = w.shape[1]
    tn = ADALN_TN
    return pl.pallas_call(
        _adaln_kernel,
        grid=(n // tn,),
        in_specs=[pl.BlockSpec((D, B), lambda j: (0, 0)),
                  pl.BlockSpec((D, tn), lambda j: (0, j)),
                  pl.BlockSpec((1, tn), lambda j: (0, j))],
        out_specs=pl.BlockSpec((B, tn), lambda j: (0, j)),
        out_shape=jax.ShapeDtypeStruct((B, n), F32),
        compiler_params=_params(("arbitrary",)),
        name="adaln",
    )(c.T, w, bias.reshape(1, n))


def _norm_proj_kernel(x_ref, shift_ref, scale_ref, g_ref, ws_ref, h_ref, zs_ref, ws_bf16):
    @pl.when(pl.program_id(0) == 0)
    def _():
        ws_bf16[...] = ws_ref[...].T.astype(BF16)

    rows = x_ref.shape[0] // NORM_ROW_CHUNKS
    for c in range(NORM_ROW_CHUNKS):
        sl = slice(c * rows, (c + 1) * rows)
        x = x_ref[sl, :]
        y = x * lax.rsqrt(jnp.mean(x * x, axis=-1, keepdims=True) + EPS) * g_ref[...]
        h = (y * (1.0 + scale_ref[...]) + shift_ref[...]).astype(BF16)
        h_ref[sl, :] = h
        zs_ref[sl, :] = jnp.dot(h, ws_bf16[...], preferred_element_type=F32)


def _norm_proj(xt, mod4, norm_g, w_in_t, seq):
    T, D = xt.shape
    tm = NORM_TM
    per_b = seq // tm
    return pl.pallas_call(
        _norm_proj_kernel,
        grid=(T // tm,),
        in_specs=[pl.BlockSpec((tm, D), lambda i: (i, 0)),
                  pl.BlockSpec((None, None, 1, D), lambda i: (i // per_b, 0, 0, 0)),
                  pl.BlockSpec((None, None, 1, D), lambda i: (i // per_b, 1, 0, 0)),
                  pl.BlockSpec((1, D), lambda i: (0, 0)),
                  pl.BlockSpec((SMALL_WIDTH, D), lambda i: (0, 0), pipeline_mode=pl.Buffered(1))],
        out_specs=[pl.BlockSpec((tm, D), lambda i: (i, 0)),
                   pl.BlockSpec((tm, SMALL_WIDTH), lambda i: (i, 0))],
        out_shape=[jax.ShapeDtypeStruct((T, D), BF16),
                   jax.ShapeDtypeStruct((T, SMALL_WIDTH), F32)],
        scratch_shapes=[pltpu.VMEM((D, SMALL_WIDTH), BF16)],
        compiler_params=_params(("arbitrary",)),
        name="norm_proj",
    )(xt, mod4, mod4, norm_g.reshape(1, D), w_in_t)


def _gate_proj_kernel(h_ref, w_hbm, o_ref, w_f32, w_bf16, sem, *, tile_kinds, first_row):
    j = pl.program_id(0)
    tn = w_bf16.shape[0]

    def window_copy(tile):
        rows = pl.ds(pl.multiple_of(first_row + tile * tn, F32_SUBLANES), tn)
        return pltpu.make_async_copy(w_hbm.at[rows, :], w_f32, sem)

    @pl.when(pl.program_id(1) == 0)
    def _():
        @pl.when(j == 0)
        def _():
            window_copy(0).start()

        window_copy(j).wait()
        w_bf16[...] = w_f32[...].astype(BF16)

        @pl.when(j + 1 < pl.num_programs(0))
        def _():
            window_copy(j + 1).start()

    def branch_of(kinds):
        return tuple("sigmoid" if k == "sigmoid" else "silu" for k in kinds)

    for branch in sorted(set(map(branch_of, tile_kinds))):
        tiles = [t for t, k in enumerate(tile_kinds) if branch_of(k) == branch]
        cond = functools.reduce(jnp.logical_or, [j == t for t in tiles])

        @pl.when(cond)
        def _(branch=branch, tiles=tiles):
            acc = lax.dot_general(h_ref[...], w_bf16[...], NT_DIMS, preferred_element_type=F32)
            width = tn // len(branch)
            for s, kind in enumerate(branch):
                cols = slice(s * width, (s + 1) * width)
                a = acc[:, cols]
                sg = _sigmoid(a)
                identity_tiles = [t for t in tiles if tile_kinds[t][s] == "linear"]
                if kind == "sigmoid":
                    out = sg
                elif identity_tiles:
                    is_identity = functools.reduce(jnp.logical_or, [j == t for t in identity_tiles])
                    out = a * jnp.where(is_identity, 1.0, sg)
                else:
                    out = a * sg
                o_ref[:, cols] = out.astype(o_ref.dtype)


def _gate_proj(h, w_in_t):
    T, D = h.shape
    tm, tn = GATE_TM, GATE_TN
    first_row = Q_LORA + KV_LORA + QK_ROPE
    n = w_in_t.shape[0] - first_row
    assert n % tn == 0 and first_row % F32_SUBLANES == 0 and tn % F32_SUBLANES == 0
    group_kinds = (("silu",) * (MLA_WIDTH // POOL_WIDTH) + ("linear", "silu")
                   + ("sigmoid",) * (2 * D // POOL_WIDTH))
    per_tile = tn // POOL_WIDTH
    tile_kinds = tuple(group_kinds[t * per_tile:(t + 1) * per_tile] for t in range(n // tn))
    kern = functools.partial(_gate_proj_kernel, tile_kinds=tile_kinds, first_row=first_row)
    return pl.pallas_call(
        kern,
        grid=(n // tn, T // tm),
        in_specs=[pl.BlockSpec((tm, D), lambda j, i: (i, 0)),
                  pl.BlockSpec(memory_space=pl.ANY)],
        out_specs=pl.BlockSpec((tm, tn), lambda j, i: (i, j)),
        out_shape=jax.ShapeDtypeStruct((T, n), BF16),
        scratch_shapes=[pltpu.VMEM((tn, D), F32), pltpu.VMEM((tn, D), BF16),
                        pltpu.SemaphoreType.DMA(())],
        compiler_params=_params(("arbitrary", "arbitrary")),
        name="gate_proj",
    )(h, w_in_t)


def _mla_prep_kernel(zs_ref, posr_ref, invc_ref, qg_ref, kvg_ref, wuq_ref, wukv_ref,
                     qt_ref, k_ref, vt_ref, wq_ref, wk_ref, wv_ref, *, q_scale):
    @pl.when(pl.program_id(0) == 0)
    def _():
        wq_ref[...] = wuq_ref[...].T.astype(BF16)
        kv_head = QK_NOPE + V_HEAD
        for h in range(N_HEADS):
            wk_ref[:, h * QK_NOPE:(h + 1) * QK_NOPE] = (
                wukv_ref[:, h * kv_head:h * kv_head + QK_NOPE].astype(BF16))
            wv_ref[h * V_HEAD:(h + 1) * V_HEAD, :] = (
                wukv_ref[:, h * kv_head + QK_NOPE:(h + 1) * kv_head].T.astype(BF16))

    def rms(v, g):
        return (v * lax.rsqrt(jnp.mean(v * v, axis=-1, keepdims=True) + EPS) * g).astype(BF16)

    cqn = rms(zs_ref[:, 0:Q_LORA], qg_ref[...])
    ckvn = rms(zs_ref[:, Q_LORA:Q_LORA + KV_LORA], kvg_ref[...])
    kr = zs_ref[:, Q_LORA + KV_LORA:Q_LORA + KV_LORA + QK_ROPE]
    kr_sw = jnp.concatenate([kr[:, QK_ROPE // 2:], kr[:, :QK_ROPE // 2]], axis=1)

    ang_t = invc_ref[...] * posr_ref[...].astype(F32)
    cos_t, sin_t = jnp.cos(ang_t), jnp.sin(ang_t)
    cos, sin = cos_t.T, sin_t.T

    qf = lax.dot_general(wq_ref[...], cqn, NT_DIMS, preferred_element_type=F32)
    half = QK_ROPE // 2
    for h in range(N_HEADS):
        r0 = h * QK_HEAD + QK_NOPE
        qt_ref[h * QK_HEAD:r0, :] = (qf[h * QK_HEAD:r0] * q_scale).astype(BF16)
        rope = qf[r0:r0 + QK_ROPE]
        rope_sw = jnp.concatenate([rope[half:], rope[:half]], axis=0)
        qt_ref[r0:r0 + QK_ROPE, :] = ((rope * cos_t + rope_sw * sin_t) * q_scale).astype(BF16)

    kn = jnp.dot(ckvn, wk_ref[...], preferred_element_type=F32)
    k_rot = (kr * cos + kr_sw * sin).astype(BF16)
    for h in range(N_HEADS):
        k_ref[h, :, 0:QK_NOPE] = kn[:, h * QK_NOPE:(h + 1) * QK_NOPE].astype(BF16)
        k_ref[h, :, QK_NOPE:QK_HEAD] = k_rot

    vt_ref[...] = lax.dot_general(wv_ref[...], ckvn, NT_DIMS,
                                  preferred_element_type=F32).astype(BF16)


def _mla_prep(zs, posr, inv_signed, q_norm_g, kv_norm_g, w_uq, w_ukv, batch, seq, q_scale):
    tm = PREP_TM
    per_b = seq // tm
    const = lambda i: (0, 0)
    resident = functools.partial(pl.BlockSpec, pipeline_mode=pl.Buffered(1))
    kern = functools.partial(_mla_prep_kernel, q_scale=q_scale)
    return pl.pallas_call(
        kern,
        grid=(batch * per_b,),
        in_specs=[pl.BlockSpec((tm, SMALL_WIDTH), lambda i: (i, 0)),
                  pl.BlockSpec((None, 1, tm), lambda i: (i // per_b, 0, i % per_b)),
                  pl.BlockSpec((QK_ROPE, 1), const),
                  pl.BlockSpec((1, Q_LORA), const),
                  pl.BlockSpec((1, KV_LORA), const),
                  resident(w_uq.shape, const),
                  resident(w_ukv.shape, const)],
        out_specs=[pl.BlockSpec((None, N_HEADS * QK_HEAD, tm), lambda i: (i // per_b, 0, i % per_b)),
                   pl.BlockSpec((None, N_HEADS, tm, QK_HEAD), lambda i: (i // per_b, 0, i % per_b, 0)),
                   pl.BlockSpec((None, MLA_WIDTH, tm), lambda i: (i // per_b, 0, i % per_b))],
        out_shape=[jax.ShapeDtypeStruct((batch, N_HEADS * QK_HEAD, seq), BF16),
                   jax.ShapeDtypeStruct((batch, N_HEADS, seq, QK_HEAD), BF16),
                   jax.ShapeDtypeStruct((batch, MLA_WIDTH, seq), BF16)],
        scratch_shapes=[pltpu.VMEM((N_HEADS * QK_HEAD, Q_LORA), BF16),
                        pltpu.VMEM((KV_LORA, N_HEADS * QK_NOPE), BF16),
                        pltpu.VMEM((MLA_WIDTH, KV_LORA), BF16)],
        compiler_params=_params(("arbitrary",)),
        name="mla_prep",
    )(zs, posr, inv_signed.reshape(QK_ROPE, 1),
      q_norm_g.reshape(1, Q_LORA), kv_norm_g.reshape(1, KV_LORA), w_uq, w_ukv)


def _attention_kernel(qt_ref, k_ref, knext_ref, vt_ref, *refs, n_cast):
    cast_in, (o_ref, *cast_out) = refs[:n_cast], refs[n_cast:2 * n_cast + 1]
    lmin_ref, ksq_ref = refs[2 * n_cast + 1:]
    for w_in, w_out in zip(cast_in, cast_out):
        w_out[...] = w_in[...].astype(w_out.dtype)

    def max_sq_norm(rows):
        rf = rows.astype(F32)
        return jnp.max(jnp.sum(rf * rf, axis=1, keepdims=True), axis=0, keepdims=True)

    tq = lmin_ref.shape[1]
    nq = qt_ref.shape[1] // tq

    @pl.when(jnp.logical_and(pl.program_id(0) == 0, pl.program_id(1) == 0))
    def _():
        ksq_ref[...] = max_sq_norm(k_ref[...])

    k_norm = jnp.sqrt(ksq_ref[...])
    lmin_ref[...] = jnp.full(lmin_ref.shape, jnp.inf, F32)

    def fast_tile(i, next_ksq):
        off = pl.multiple_of(i * tq, tq)
        next_ksq = jnp.maximum(next_ksq, max_sq_norm(knext_ref[pl.ds(off, tq), :]))
        qt = qt_ref[:, pl.ds(off, tq)]
        qf = qt.astype(F32)
        shift = jnp.sqrt(jnp.sum(qf * qf, axis=0, keepdims=True)) * k_norm
        s = jnp.dot(k_ref[...], qt, preferred_element_type=F32)
        p = jnp.exp2(s - shift)
        l = jnp.sum(p, axis=0, keepdims=True)
        ot = jnp.dot(vt_ref[...], p.astype(BF16), preferred_element_type=F32)
        o_ref[pl.ds(off, tq), :] = (ot / l).T.astype(o_ref.dtype)
        lmin_ref[...] = jnp.minimum(lmin_ref[...], l)
        return next_ksq

    ksq_ref[...] = lax.fori_loop(0, nq, fast_tile, jnp.zeros((1, 1), F32), unroll=4)
    trusted = jnp.min(lmin_ref[...]) >= MIN_SOFTMAX_MASS

    @pl.when(jnp.logical_not(trusted))
    def _():
        _attention_exact(qt_ref, k_ref, vt_ref, o_ref)


def _attention_exact(qt_ref, k_ref, vt_ref, o_ref):
    tq = ATTN_TQ_EXACT
    nq = qt_ref.shape[1] // tq

    def tile(i, carry):
        off = pl.multiple_of(i * tq, tq)
        s = jnp.dot(k_ref[...], qt_ref[:, pl.ds(off, tq)], preferred_element_type=F32)
        p = jnp.exp2(s - jnp.max(s, axis=0, keepdims=True))
        l = jnp.sum(p, axis=0, keepdims=True)
        ot = jnp.dot(vt_ref[...], p.astype(BF16), preferred_element_type=F32)
        o_ref[pl.ds(off, tq), :] = (ot / l).T.astype(o_ref.dtype)
        return carry

    lax.fori_loop(0, nq, tile, 0)


def _attention(qt, k, vt, batch, seq, later_weights):
    tq_fast = ATTN_TQ_FAST
    assert seq % ATTN_TQ_EXACT == 0 and seq % tq_fast == 0
    steps = batch * N_HEADS
    slabs = [w.shape[0] // steps for w in later_weights]
    assert all(s % BF16_SUBLANES == 0 and s * steps == w.shape[0]
               for s, w in zip(slabs, later_weights))
    step = lambda b, h: (b * N_HEADS + h, 0)
    cast_specs = [pl.BlockSpec((s, w.shape[1]), step) for s, w in zip(slabs, later_weights)]

    def next_head(b, h):
        t = jnp.minimum(b * N_HEADS + h + 1, steps - 1)
        return t // N_HEADS, t % N_HEADS, 0, 0

    outs = pl.pallas_call(
        functools.partial(_attention_kernel, n_cast=len(later_weights)),
        grid=(batch, N_HEADS),
        in_specs=[pl.BlockSpec((None, QK_HEAD, seq), lambda b, h: (b, h, 0)),
                  pl.BlockSpec((None, None, seq, QK_HEAD), lambda b, h: (b, h, 0, 0)),
                  pl.BlockSpec((None, None, seq, QK_HEAD), next_head),
                  pl.BlockSpec((None, V_HEAD, seq), lambda b, h: (b, h, 0))] + cast_specs,
        out_specs=[pl.BlockSpec((None, seq, V_HEAD), lambda b, h: (b, 0, h))] + cast_specs,
        out_shape=[jax.ShapeDtypeStruct((batch, seq, MLA_WIDTH), BF16)]
                  + [jax.ShapeDtypeStruct(w.shape, BF16) for w in later_weights],
        scratch_shapes=[pltpu.VMEM((1, tq_fast), F32), pltpu.VMEM((1, 1), F32)],
        compiler_params=_params(("arbitrary", "arbitrary")),
        name="attention",
    )(qt, k, k, vt, *later_weights)
    return outs[0], outs[1:]


def _merge_out_kernel(x_ref, gate_ref, attn_ref, gm_ref, gp_ref, vp_ref, vprev_ref, vnext_ref,
                      mm_ref, mp_ref, pw_ref, ps_ref, wop_ref, wom_ref, wout_ref, fg_ref, o_ref,
                      *, seq, final_norm):
    tm = x_ref.shape[0]
    t0 = (pl.program_id(0) % (seq // tm)) * tm

    gated = attn_ref[...] * gm_ref[...]

    cur = vp_ref[...].astype(F32)
    prev = jnp.where(t0 > 0, vprev_ref[...].astype(F32), 0.0)
    nxt = jnp.where(t0 + tm < seq, vnext_ref[...].astype(F32), 0.0)
    ext = jnp.concatenate([prev, cur, nxt], axis=0)
    n_ext = tm + 2 * POOL_HALO
    tok = t0 + lax.broadcasted_iota(jnp.int32, (tm, 1), 0)
    mixed = []
    p_mla = []
    mla_cols = wom_ref.shape[1] // POOL_GROUPS
    for g, w in enumerate(POOL_WINDOWS):
        p_mla.append(jnp.dot(gated, wom_ref[:, g * mla_cols:(g + 1) * mla_cols],
                             preferred_element_type=F32))
        acc = ext[:, g * POOL_GROUP_DIM:(g + 1) * POOL_GROUP_DIM]
        acc = acc + pltpu.roll(acc, 1, axis=0)
        half = 1
        while 2 * half < w:
            acc = pltpu.roll(acc, half, axis=0) + pltpu.roll(acc, n_ext - half, axis=0)
            half *= 2
        wsum = acc[POOL_HALO:POOL_HALO + tm]
        count = (jnp.minimum(tok + w // 2, seq) - jnp.maximum(tok - w // 2, 0)).astype(F32)
        pooled = wsum / count - cur[:, g * POOL_GROUP_DIM:(g + 1) * POOL_GROUP_DIM]
        mixed.append(jnp.dot(pooled.astype(BF16), pw_ref[g], preferred_element_type=F32))
    mixed = jnp.concatenate(mixed, axis=1)
    p_mla = jnp.concatenate(p_mla, axis=1)
    u = (mixed * ps_ref[...] * gp_ref[...].astype(F32)).astype(BF16)
    p_pool = jnp.dot(u, wop_ref[...], preferred_element_type=F32)

    y = mm_ref[...].astype(F32) * p_mla + mp_ref[...].astype(F32) * p_pool
    r = jnp.dot(y.astype(BF16), wout_ref[...], preferred_element_type=F32)
    xo = x_ref[...] + gate_ref[...] * r
    if final_norm:
        xo = xo * lax.rsqrt(jnp.mean(xo * xo, axis=-1, keepdims=True) + EPS) * fg_ref[...]
    o_ref[...] = xo


def _merge_out(xt, mod4, attn, zbig, pool_w, pool_scale, w_o_pool, w_o_mla, w_out, final_g,
               seq, final_norm):
    T, D = xt.shape
    tm = MERGE_TM
    per_b = seq // tm
    halo_per_tile = tm // POOL_HALO
    n_halo = T // POOL_HALO
    vp_blk = MLA_WIDTH // POOL_WIDTH
    gp_blk = vp_blk + 1
    mm_blk = (MLA_WIDTH + 2 * POOL_WIDTH) // D
    resident = functools.partial(pl.BlockSpec, pipeline_mode=pl.Buffered(1))
    kern = functools.partial(_merge_out_kernel, seq=seq, final_norm=final_norm)
    return pl.pallas_call(
        kern,
        grid=(T // tm,),
        in_specs=[pl.BlockSpec((tm, D), lambda i: (i, 0)),
                  pl.BlockSpec((None, None, 1, D), lambda i: (i // per_b, 2, 0, 0)),
                  pl.BlockSpec((tm, MLA_WIDTH), lambda i: (i, 0)),
                  pl.BlockSpec((tm, MLA_WIDTH), lambda i: (i, 0)),
                  pl.BlockSpec((tm, POOL_WIDTH), lambda i: (i, gp_blk)),
                  pl.BlockSpec((tm, POOL_WIDTH), lambda i: (i, vp_blk)),
                  pl.BlockSpec((POOL_HALO, POOL_WIDTH),
                               lambda i: (jnp.maximum(i * halo_per_tile - 1, 0), vp_blk)),
                  pl.BlockSpec((POOL_HALO, POOL_WIDTH),
                               lambda i: (jnp.minimum((i + 1) * halo_per_tile, n_halo - 1), vp_blk)),
                  pl.BlockSpec((tm, D), lambda i: (i, mm_blk)),
                  pl.BlockSpec((tm, D), lambda i: (i, mm_blk + 1)),
                  resident(pool_w.shape, lambda i: (0, 0, 0)),
                  resident((1, POOL_WIDTH), lambda i: (0, 0)),
                  resident(w_o_pool.shape, lambda i: (0, 0)),
                  resident(w_o_mla.shape, lambda i: (0, 0)),
                  resident(w_out.shape, lambda i: (0, 0)),
                  resident((1, D), lambda i: (0, 0))],
        out_specs=pl.BlockSpec((tm, D), lambda i: (i, 0)),
        out_shape=jax.ShapeDtypeStruct((T, D), F32),
        compiler_params=_params(("arbitrary",)),
        name="merge_out",
    )(xt, mod4, attn, zbig, zbig, zbig, zbig, zbig, zbig, zbig,
      pool_w, pool_scale.reshape(1, POOL_WIDTH), w_o_pool, w_o_mla, w_out, final_g.reshape(1, D))


def kernel(x, c, positions, ada_w, ada_b, norm_g, w_in, q_norm_g, w_uq, kv_norm_g, w_ukv, w_o_mla,
           pool_w, pool_scale, w_o_pool, w_out, final_g):
    B, S, D = x.shape
    depth = ada_w.shape[0]
    inv_freq = 1.0 / (ROPE_THETA ** (jnp.arange(0, QK_ROPE, 2, dtype=F32) / QK_ROPE))
    inv_signed = jnp.concatenate([-inv_freq, inv_freq])
    posr = positions.reshape(B, 1, S)
    q_scale = QK_HEAD ** -0.5 * math.log2(math.e)

    xt = x.reshape(B * S, D)
    for l in range(depth):
        mod4 = _adaln(c, ada_w[l], ada_b[l]).reshape(B, 3, 1, D)
        w_in_t = w_in[l].T
        h, zs = _norm_proj(xt, mod4, norm_g[l], w_in_t, S)
        zbig = _gate_proj(h, w_in_t)
        qt, k, vt = _mla_prep(zs, posr, inv_signed, q_norm_g[l], kv_norm_g[l],
                              w_uq[l], w_ukv[l], B, S, q_scale)
        attn, (pw, wop, wom, wout) = _attention(
            qt, k, vt, B, S,
            (pool_w[l].reshape(POOL_WIDTH, POOL_GROUP_DIM), w_o_pool[l], w_o_mla[l], w_out[l]))
        xt = _merge_out(xt, mod4, attn.reshape(B * S, MLA_WIDTH), zbig,
                        pw.reshape(POOL_GROUPS, POOL_GROUP_DIM, POOL_GROUP_DIM), pool_scale[l],
                        wop, wom, wout, final_g, S, final_norm=(l == depth - 1))
    return xt.reshape(B, S, D)
```

```python
import functools
import math

import jax
import jax.numpy as jnp
from jax import lax
from jax.experimental import pallas as pl
from jax.experimental.pallas import tpu as pltpu

EPS = 1e-6
N_HEADS = 16
QK_NOPE = 128
QK_ROPE = 64
QK_HEAD = QK_NOPE + QK_ROPE
V_HEAD = 128
Q_LORA = 512
KV_LORA = 512
MLA_WIDTH = N_HEADS * V_HEAD
ROPE_THETA = 10000.0
POOL_WINDOWS = (2, 4, 8, 16)
POOL_GROUPS = len(POOL_WINDOWS)
POOL_GROUP_DIM = 256
POOL_WIDTH = POOL_GROUPS * POOL_GROUP_DIM
POOL_HALO = 16
SMALL_WIDTH = Q_LORA + KV_LORA + 2 * QK_ROPE

V7X_VMEM_LIMIT = 56 * 1024 * 1024
ADALN_TN = 1024
NORM_TM = 512
NORM_ROW_CHUNKS = 4
GATE_TM, GATE_TN = 1024, 2048
PREP_TM = 512
ATTN_TQ_EXACT = 256
ATTN_TQ_FAST = 512
MERGE_TM = 256
BF16_SUBLANES = 16
F32_SUBLANES = 8
MIN_SOFTMAX_MASS = 2.0 ** -60

F32 = jnp.float32
BF16 = jnp.bfloat16
NT_DIMS = (((1,), (1,)), ((), ()))


def _sigmoid(v):
    return 0.5 * jnp.tanh(0.5 * v) + 0.5


def _params(semantics):
    return pltpu.CompilerParams(dimension_semantics=semantics, vmem_limit_bytes=V7X_VMEM_LIMIT)


def _adaln_kernel(ct_ref, w_ref, b_ref, o_ref):
    w = w_ref[...]
    for b in range(ct_ref.shape[1]):
        cb = ct_ref[:, b:b + 1]
        act = cb * _sigmoid(cb)
        o_ref[b:b + 1, :] = jnp.sum(w * act, axis=0, keepdims=True) + b_ref[...]


def _adaln(c, w, bias):
    B, D = c.shape
    n = w.shape[1]
    tn = ADALN_TN
    return pl.pallas_call(
        _adaln_kernel,
        grid=(n // tn,),
        in_specs=[pl.BlockSpec((D, B), lambda j: (0, 0)),
                  pl.BlockSpec((D, tn), lambda j: (0, j)),
                  pl.BlockSpec((1, tn), lambda j: (0, j))],
        out_specs=pl.BlockSpec((B, tn), lambda j: (0, j)),
        out_shape=jax.ShapeDtypeStruct((B, n), F32),
        compiler_params=_params(("arbitrary",)),
        name="adaln",
    )(c.T, w, bias.reshape(1, n))


def _norm_proj_kernel(x_ref, shift_ref, scale_ref, g_ref, ws_ref, h_ref, zs_ref, ws_bf16):
    @pl.when(pl.program_id(0) == 0)
    def _():
        ws_bf16[...] = ws_ref[...].T.astype(BF16)

    rows = x_ref.shape[0] // NORM_ROW_CHUNKS
    for c in range(NORM_ROW_CHUNKS):
        sl = slice(c * rows, (c + 1) * rows)
        x = x_ref[sl, :]
        y = x * lax.rsqrt(jnp.mean(x * x, axis=-1, keepdims=True) + EPS) * g_ref[...]
        h = (y * (1.0 + scale_ref[...]) + shift_ref[...]).astype(BF16)
        h_ref[sl, :] = h
        zs_ref[sl, :] = jnp.dot(h, ws_bf16[...], preferred_element_type=F32)


def _norm_proj(xt, mod4, norm_g, w_in_t, seq):
    T, D = xt.shape
    tm = NORM_TM
    per_b = seq // tm
    return pl.pallas_call(
        _norm_proj_kernel,
        grid=(T // tm,),
        in_specs=[pl.BlockSpec((tm, D), lambda i: (i, 0)),
                  pl.BlockSpec((None, None, 1, D), lambda i: (i // per_b, 0, 0, 0)),
                  pl.BlockSpec((None, None, 1, D), lambda i: (i // per_b, 1, 0, 0)),
                  pl.BlockSpec((1, D), lambda i: (0, 0)),
                  pl.BlockSpec((SMALL_WIDTH, D), lambda i: (0, 0), pipeline_mode=pl.Buffered(1))],
        out_specs=[pl.BlockSpec((tm, D), lambda i: (i, 0)),
                   pl.BlockSpec((tm, SMALL_WIDTH), lambda i: (i, 0))],
        out_shape=[jax.ShapeDtypeStruct((T, D), BF16),
                   jax.ShapeDtypeStruct((T, SMALL_WIDTH), F32)],
        scratch_shapes=[pltpu.VMEM((D, SMALL_WIDTH), BF16)],
        compiler_params=_params(("arbitrary",)),
        name="norm_proj",
    )(xt, mod4, mod4, norm_g.reshape(1, D), w_in_t)


def _gate_proj_kernel(h_ref, w_hbm, o_ref, w_f32, w_bf16, sem, *, tile_kinds, first_row):
    j = pl.program_id(0)
    tn = w_bf16.shape[0]

    def window_copy(tile):
        rows = pl.ds(pl.multiple_of(first_row + tile * tn, F32_SUBLANES), tn)
        return pltpu.make_async_copy(w_hbm.at[rows, :], w_f32, sem)

    @pl.when(pl.program_id(1) == 0)
    def _():
        @pl.when(j == 0)
        def _():
            window_copy(0).start()

        window_copy(j).wait()
        w_bf16[...] = w_f32[...].astype(BF16)

        @pl.when(j + 1 < pl.num_programs(0))
        def _():
            window_copy(j + 1).start()

    def any_tile(kind, s):
        tiles = [t for t, k in enumerate(tile_kinds) if k[s] == kind]
        return functools.reduce(jnp.logical_or, [j == t for t in tiles]) if tiles else None

    acc = lax.dot_general(h_ref[...], w_bf16[...], NT_DIMS, preferred_element_type=F32)
    groups = len(tile_kinds[0])
    width = tn // groups
    for s in range(groups):
        cols = slice(s * width, (s + 1) * width)
        a = acc[:, cols]
        sg = _sigmoid(a)
        is_sigmoid, is_identity = any_tile("sigmoid", s), any_tile("linear", s)
        left = a if is_sigmoid is None else jnp.where(is_sigmoid, 1.0, a)
        right = sg if is_identity is None else jnp.where(is_identity, 1.0, sg)
        o_ref[:, cols] = (left * right).astype(o_ref.dtype)


def _gate_proj(h, w_in_t):
    T, D = h.shape
    tm, tn = GATE_TM, GATE_TN
    first_row = Q_LORA + KV_LORA + QK_ROPE
    n = w_in_t.shape[0] - first_row
    assert n % tn == 0 and first_row % F32_SUBLANES == 0 and tn % F32_SUBLANES == 0
    group_kinds = (("silu",) * (MLA_WIDTH // POOL_WIDTH) + ("linear", "silu")
                   + ("sigmoid",) * (2 * D // POOL_WIDTH))
    per_tile = tn // POOL_WIDTH
    tile_kinds = tuple(group_kinds[t * per_tile:(t + 1) * per_tile] for t in range(n // tn))
    kern = functools.partial(_gate_proj_kernel, tile_kinds=tile_kinds, first_row=first_row)
    return pl.pallas_call(
        kern,
        grid=(n // tn, T // tm),
        in_specs=[pl.BlockSpec((tm, D), lambda j, i: (i, 0)),
                  pl.BlockSpec(memory_space=pl.ANY)],
        out_specs=pl.BlockSpec((tm, tn), lambda j, i: (i, j)),
        out_shape=jax.ShapeDtypeStruct((T, n), BF16),
        scratch_shapes=[pltpu.VMEM((tn, D), F32), pltpu.VMEM((tn, D), BF16),
                        pltpu.SemaphoreType.DMA(())],
        compiler_params=_params(("arbitrary", "arbitrary")),
        name="gate_proj",
    )(h, w_in_t)


def _mla_prep_kernel(zs_ref, posr_ref, invc_ref, qg_ref, kvg_ref, wuq_ref, wukv_ref,
                     qt_ref, k_ref, vt_ref, wq_ref, wk_ref, wv_ref, *, q_scale):
    @pl.when(pl.program_id(0) == 0)
    def _():
        wq_ref[...] = wuq_ref[...].T.astype(BF16)
        kv_head = QK_NOPE + V_HEAD
        for h in range(N_HEADS):
            wk_ref[:, h * QK_NOPE:(h + 1) * QK_NOPE] = (
                wukv_ref[:, h * kv_head:h * kv_head + QK_NOPE].astype(BF16))
            wv_ref[h * V_HEAD:(h + 1) * V_HEAD, :] = (
                wukv_ref[:, h * kv_head + QK_NOPE:(h + 1) * kv_head].T.astype(BF16))

    def rms(v, g):
        return (v * lax.rsqrt(jnp.mean(v * v, axis=-1, keepdims=True) + EPS) * g).astype(BF16)

    cqn = rms(zs_ref[:, 0:Q_LORA], qg_ref[...])
    ckvn = rms(zs_ref[:, Q_LORA:Q_LORA + KV_LORA], kvg_ref[...])
    kr = zs_ref[:, Q_LORA + KV_LORA:Q_LORA + KV_LORA + QK_ROPE]
    kr_sw = jnp.concatenate([kr[:, QK_ROPE // 2:], kr[:, :QK_ROPE // 2]], axis=1)

    ang_t = invc_ref[...] * posr_ref[...].astype(F32)
    cos_t, sin_t = jnp.cos(ang_t), jnp.sin(ang_t)
    cos, sin = cos_t.T, sin_t.T

    qf = lax.dot_general(wq_ref[...], cqn, NT_DIMS, preferred_element_type=F32)
    half = QK_ROPE // 2
    for h in range(N_HEADS):
        r0 = h * QK_HEAD + QK_NOPE
        qt_ref[h * QK_HEAD:r0, :] = (qf[h * QK_HEAD:r0] * q_scale).astype(BF16)
        rope = qf[r0:r0 + QK_ROPE]
        rope_sw = jnp.concatenate([rope[half:], rope[:half]], axis=0)
        qt_ref[r0:r0 + QK_ROPE, :] = ((rope * cos_t + rope_sw * sin_t) * q_scale).astype(BF16)

    kn = jnp.dot(ckvn, wk_ref[...], preferred_element_type=F32)
    k_rot = (kr * cos + kr_sw * sin).astype(BF16)
    for h in range(N_HEADS):
        k_ref[h, :, 0:QK_NOPE] = kn[:, h * QK_NOPE:(h + 1) * QK_NOPE].astype(BF16)
        k_ref[h, :, QK_NOPE:QK_HEAD] = k_rot

    vt_ref[...] = lax.dot_general(wv_ref[...], ckvn, NT_DIMS,
                                  preferred_element_type=F32).astype(BF16)


def _mla_prep(zs, posr, inv_signed, q_norm_g, kv_norm_g, w_uq, w_ukv, batch, seq, q_scale):
    tm = PREP_TM
    per_b = seq // tm
    const = lambda i: (0, 0)
    resident = functools.partial(pl.BlockSpec, pipeline_mode=pl.Buffered(1))
    kern = functools.partial(_mla_prep_kernel, q_scale=q_scale)
    return pl.pallas_call(
        kern,
        grid=(batch * per_b,),
        in_specs=[pl.BlockSpec((tm, SMALL_WIDTH), lambda i: (i, 0)),
                  pl.BlockSpec((None, 1, tm), lambda i: (i // per_b, 0, i % per_b)),
                  pl.BlockSpec((QK_ROPE, 1), const),
                  pl.BlockSpec((1, Q_LORA), const),
                  pl.BlockSpec((1, KV_LORA), const),
                  resident(w_uq.shape, const),
                  resident(w_ukv.shape, const)],
        out_specs=[pl.BlockSpec((None, N_HEADS * QK_HEAD, tm), lambda i: (i // per_b, 0, i % per_b)),
                   pl.BlockSpec((None, N_HEADS, tm, QK_HEAD), lambda i: (i // per_b, 0, i % per_b, 0)),
                   pl.BlockSpec((None, MLA_WIDTH, tm), lambda i: (i // per_b, 0, i % per_b))],
        out_shape=[jax.ShapeDtypeStruct((batch, N_HEADS * QK_HEAD, seq), BF16),
                   jax.ShapeDtypeStruct((batch, N_HEADS, seq, QK_HEAD), BF16),
                   jax.ShapeDtypeStruct((batch, MLA_WIDTH, seq), BF16)],
        scratch_shapes=[pltpu.VMEM((N_HEADS * QK_HEAD, Q_LORA), BF16),
                        pltpu.VMEM((KV_LORA, N_HEADS * QK_NOPE), BF16),
                        pltpu.VMEM((MLA_WIDTH, KV_LORA), BF16)],
        compiler_params=_params(("arbitrary",)),
        name="mla_prep",
    )(zs, posr, inv_signed.reshape(QK_ROPE, 1),
      q_norm_g.reshape(1, Q_LORA), kv_norm_g.reshape(1, KV_LORA), w_uq, w_ukv)


def _attention_kernel(qt_ref, k_ref, knext_ref, vt_ref, *refs, n_cast):
    cast_in, (o_ref, *cast_out) = refs[:n_cast], refs[n_cast:2 * n_cast + 1]
    lmin_ref, ksq_ref = refs[2 * n_cast + 1:]
    for w_in, w_out in zip(cast_in, cast_out):
        w_out[...] = w_in[...].astype(w_out.dtype)

    def max_sq_norm(rows):
        rf = rows.astype(F32)
        return jnp.max(jnp.sum(rf * rf, axis=1, keepdims=True), axis=0, keepdims=True)

    tq = lmin_ref.shape[1]
    nq = qt_ref.shape[1] // tq

    @pl.when(jnp.logical_and(pl.program_id(0) == 0, pl.program_id(1) == 0))
    def _():
        ksq_ref[...] = max_sq_norm(k_ref[...])

    k_norm = jnp.sqrt(ksq_ref[...])
    lmin_ref[...] = jnp.full(lmin_ref.shape, jnp.inf, F32)

    def fast_tile(i, next_ksq):
        off = pl.multiple_of(i * tq, tq)
        next_ksq = jnp.maximum(next_ksq, max_sq_norm(knext_ref[pl.ds(off, tq), :]))
        qt = qt_ref[:, pl.ds(off, tq)]
        qf = qt.astype(F32)
        shift = jnp.sqrt(jnp.sum(qf * qf, axis=0, keepdims=True)) * k_norm
        s = jnp.dot(k_ref[...], qt, preferred_element_type=F32)
        p = jnp.exp2(s - shift)
        l = jnp.sum(p, axis=0, keepdims=True)
        ot = jnp.dot(vt_ref[...], p.astype(BF16), preferred_element_type=F32)
        o_ref[pl.ds(off, tq), :] = (ot / l).T.astype(o_ref.dtype)
        lmin_ref[...] = jnp.minimum(lmin_ref[...], l)
        return next_ksq

    ksq_ref[...] = lax.fori_loop(0, nq, fast_tile, jnp.zeros((1, 1), F32), unroll=True)
    trusted = jnp.min(lmin_ref[...]) >= MIN_SOFTMAX_MASS

    @pl.when(jnp.logical_not(trusted))
    def _():
        _attention_exact(qt_ref, k_ref, vt_ref, o_ref)


def _attention_exact(qt_ref, k_ref, vt_ref, o_ref):
    tq = ATTN_TQ_EXACT
    nq = qt_ref.shape[1] // tq

    def tile(i, carry):
        off = pl.multiple_of(i * tq, tq)
        s = jnp.dot(k_ref[...], qt_ref[:, pl.ds(off, tq)], preferred_element_type=F32)
        p = jnp.exp2(s - jnp.max(s, axis=0, keepdims=True))
        l = jnp.sum(p, axis=0, keepdims=True)
        ot = jnp.dot(vt_ref[...], p.astype(BF16), preferred_element_type=F32)
        o_ref[pl.ds(off, tq), :] = (ot / l).T.astype(o_ref.dtype)
        return carry

    lax.fori_loop(0, nq, tile, 0)


def _attention(qt, k, vt, batch, seq, later_weights):
    tq_fast = ATTN_TQ_FAST
    assert seq % ATTN_TQ_EXACT == 0 and seq % tq_fast == 0
    steps = batch * N_HEADS
    slabs = [w.shape[0] // steps for w in later_weights]
    assert all(s % BF16_SUBLANES == 0 and s * steps == w.shape[0]
               for s, w in zip(slabs, later_weights))
    step = lambda b, h: (b * N_HEADS + h, 0)
    cast_specs = [pl.BlockSpec((s, w.shape[1]), step) for s, w in zip(slabs, later_weights)]

    def next_head(b, h):
        t = jnp.minimum(b * N_HEADS + h + 1, steps - 1)
        return t // N_HEADS, t % N_HEADS, 0, 0

    outs = pl.pallas_call(
        functools.partial(_attention_kernel, n_cast=len(later_weights)),
        grid=(batch, N_HEADS),
        in_specs=[pl.BlockSpec((None, QK_HEAD, seq), lambda b, h: (b, h, 0)),
                  pl.BlockSpec((None, None, seq, QK_HEAD), lambda b, h: (b, h, 0, 0)),
                  pl.BlockSpec((None, None, seq, QK_HEAD), next_head),
                  pl.BlockSpec((None, V_HEAD, seq), lambda b, h: (b, h, 0))] + cast_specs,
        out_specs=[pl.BlockSpec((None, seq, V_HEAD), lambda b, h: (b, 0, h))] + cast_specs,
        out_shape=[jax.ShapeDtypeStruct((batch, seq, MLA_WIDTH), BF16)]
                  + [jax.ShapeDtypeStruct(w.shape, BF16) for w in later_weights],
        scratch_shapes=[pltpu.VMEM((1, tq_fast), F32), pltpu.VMEM((1, 1), F32)],
        compiler_params=_params(("arbitrary", "arbitrary")),
        name="attention",
    )(qt, k, k, vt, *later_weights)
    return outs[0], outs[1:]


def _merge_out_kernel(x_ref, gate_ref, attn_ref, gm_ref, gp_ref, vp_ref, vprev_ref, vnext_ref,
                      mm_ref, mp_ref, pw_ref, ps_ref, wop_ref, wom_ref, wout_ref, fg_ref, o_ref,
                      *, seq, final_norm):
    tm = x_ref.shape[0]
    t0 = (pl.program_id(0) % (seq // tm)) * tm

    gated = attn_ref[...] * gm_ref[...]

    cur = vp_ref[...].astype(F32)
    prev = jnp.where(t0 > 0, vprev_ref[...].astype(F32), 0.0)
    nxt = jnp.where(t0 + tm < seq, vnext_ref[...].astype(F32), 0.0)
    ext = jnp.concatenate([prev, cur, nxt], axis=0)
    n_ext = tm + 2 * POOL_HALO
    tok = t0 + lax.broadcasted_iota(jnp.int32, (tm, 1), 0)
    mixed = []
    p_mla = []
    mla_cols = wom_ref.shape[1] // POOL_GROUPS
    for g, w in enumerate(POOL_WINDOWS):
        p_mla.append(jnp.dot(gated, wom_ref[:, g * mla_cols:(g + 1) * mla_cols],
                             preferred_element_type=F32))
        acc = ext[:, g * POOL_GROUP_DIM:(g + 1) * POOL_GROUP_DIM]
        acc = acc + pltpu.roll(acc, 1, axis=0)
        half = 1
        while 2 * half < w:
            acc = pltpu.roll(acc, half, axis=0) + pltpu.roll(acc, n_ext - half, axis=0)
            half *= 2
        wsum = acc[POOL_HALO:POOL_HALO + tm]
        count = (jnp.minimum(tok + w // 2, seq) - jnp.maximum(tok - w // 2, 0)).astype(F32)
        pooled = wsum / count - cur[:, g * POOL_GROUP_DIM:(g + 1) * POOL_GROUP_DIM]
        mixed.append(jnp.dot(pooled.astype(BF16), pw_ref[g], preferred_element_type=F32))
    mixed = jnp.concatenate(mixed, axis=1)
    p_mla = jnp.concatenate(p_mla, axis=1)
    u = (mixed * ps_ref[...] * gp_ref[...].astype(F32)).astype(BF16)
    p_pool = jnp.dot(u, wop_ref[...], preferred_element_type=F32)

    y = mm_ref[...].astype(F32) * p_mla + mp_ref[...].astype(F32) * p_pool
    r = jnp.dot(y.astype(BF16), wout_ref[...], preferred_element_type=F32)
    xo = x_ref[...] + gate_ref[...] * r
    if final_norm:
        xo = xo * lax.rsqrt(jnp.mean(xo * xo, axis=-1, keepdims=True) + EPS) * fg_ref[...]
    o_ref[...] = xo


def _merge_out(xt, mod4, attn, zbig, pool_w, pool_scale, w_o_pool, w_o_mla, w_out, final_g,
               seq, final_norm):
    T, D = xt.shape
    tm = MERGE_TM
    per_b = seq // tm
    halo_per_tile = tm // POOL_HALO
    n_halo = T // POOL_HALO
    vp_blk = MLA_WIDTH // POOL_WIDTH
    gp_blk = vp_blk + 1
    mm_blk = (MLA_WIDTH + 2 * POOL_WIDTH) // D
    resident = functools.partial(pl.BlockSpec, pipeline_mode=pl.Buffered(1))
    kern = functools.partial(_merge_out_kernel, seq=seq, final_norm=final_norm)
    return pl.pallas_call(
        kern,
        grid=(T // tm,),
        in_specs=[pl.BlockSpec((tm, D), lambda i: (i, 0)),
                  pl.BlockSpec((None, None, 1, D), lambda i: (i // per_b, 2, 0, 0)),
                  pl.BlockSpec((tm, MLA_WIDTH), lambda i: (i, 0)),
                  pl.BlockSpec((tm, MLA_WIDTH), lambda i: (i, 0)),
                  pl.BlockSpec((tm, POOL_WIDTH), lambda i: (i, gp_blk)),
                  pl.BlockSpec((tm, POOL_WIDTH), lambda i: (i, vp_blk)),
                  pl.BlockSpec((POOL_HALO, POOL_WIDTH),
                               lambda i: (jnp.maximum(i * halo_per_tile - 1, 0), vp_blk)),
                  pl.BlockSpec((POOL_HALO, POOL_WIDTH),
                               lambda i: (jnp.minimum((i + 1) * halo_per_tile, n_halo - 1), vp_blk)),
                  pl.BlockSpec((tm, D), lambda i: (i, mm_blk)),
                  pl.BlockSpec((tm, D), lambda i: (i, mm_blk + 1)),
                  resident(pool_w.shape, lambda i: (0, 0, 0)),
                  resident((1, POOL_WIDTH), lambda i: (0, 0)),
                  resident(w_o_pool.shape, lambda i: (0, 0)),
                  resident(w_o_mla.shape, lambda i: (0, 0)),
                  resident(w_out.shape, lambda i: (0, 0)),
                  resident((1, D), lambda i: (0, 0))],
        out_specs=pl.BlockSpec((tm, D), lambda i: (i, 0)),
        out_shape=jax.ShapeDtypeStruct((T, D), F32),
        compiler_params=_params(("arbitrary",)),
        name="merge_out",
    )(xt, mod4, attn, zbig, zbig, zbig, zbig, zbig, zbig, zbig,
      pool_w, pool_scale.reshape(1, POOL_WIDTH), w_o_pool, w_o_mla, w_out, final_g.reshape(1, D))


def kernel(x, c, positions, ada_w, ada_b, norm_g, w_in, q_norm_g, w_uq, kv_norm_g, w_ukv, w_o_mla,
           pool_w, pool_scale, w_o_pool, w_out, final_g):
    B, S, D = x.shape
    depth = ada_w.shape[0]
    inv_freq = 1.0 / (ROPE_THETA ** (jnp.arange(0, QK_ROPE, 2, dtype=F32) / QK_ROPE))
    inv_signed = jnp.concatenate([-inv_freq, inv_freq])
    posr = positions.reshape(B, 1, S)
    q_scale = QK_HEAD ** -0.5 * math.log2(math.e)

    xt = x.reshape(B * S, D)
    for l in range(depth):
        mod4 = _adaln(c, ada_w[l], ada_b[l]).reshape(B, 3, 1, D)
        w_in_t = w_in[l].T
        h, zs = _norm_proj(xt, mod4, norm_g[l], w_in_t, S)
        zbig = _gate_proj(h, w_in_t)
        qt, k, vt = _mla_prep(zs, posr, inv_signed, q_norm_g[l], kv_norm_g[l],
                              w_uq[l], w_ukv[l], B, S, q_scale)
        attn, (pw, wop, wom, wout) = _attention(
            qt, k, vt, B, S,
            (pool_w[l].reshape(POOL_WIDTH, POOL_GROUP_DIM), w_o_pool[l], w_o_mla[l], w_out[l]))
        xt = _merge_out(xt, mod4, attn.reshape(B * S, MLA_WIDTH), zbig,
                        pw.reshape(POOL_GROUPS, POOL_GROUP_DIM, POOL_GROUP_DIM), pool_scale[l],
                        wop, wom, wout, final_g, S, final_norm=(l == depth - 1))
    return xt.reshape(B, S, D)
```

```python
import functools
import math

import jax
import jax.numpy as jnp
from jax import lax
from jax.experimental import pallas as pl
from jax.experimental.pallas import tpu as pltpu

EPS = 1e-6
N_HEADS = 16
QK_NOPE = 128
QK_ROPE = 64
QK_HEAD = QK_NOPE + QK_ROPE
V_HEAD = 128
Q_LORA = 512
KV_LORA = 512
MLA_WIDTH = N_HEADS * V_HEAD
ROPE_THETA = 10000.0
POOL_WINDOWS = (2, 4, 8, 16)
POOL_GROUPS = len(POOL_WINDOWS)
POOL_GROUP_DIM = 256
POOL_WIDTH = POOL_GROUPS * POOL_GROUP_DIM
POOL_HALO = 16
SMALL_WIDTH = Q_LORA + KV_LORA + 2 * QK_ROPE

V7X_VMEM_LIMIT = 56 * 1024 * 1024
ADALN_ROWS = 128
ADALN_BUFS = 4
NORM_TM = 512
NORM_ROW_CHUNKS = 4
GATE_TM, GATE_TN = 1024, 2048
PREP_TM = 512
ATTN_TQ_EXACT = 256
ATTN_TQ_FAST = 512
MERGE_TM = 256
BF16_SUBLANES = 16
F32_SUBLANES = 8
MIN_SOFTMAX_MASS = 2.0 ** -60

F32 = jnp.float32
BF16 = jnp.bfloat16
NT_DIMS = (((1,), (1,)), ((), ()))


def _sigmoid(v):
    return 0.5 * jnp.tanh(0.5 * v) + 0.5


def _params(semantics):
    return pltpu.CompilerParams(dimension_semantics=semantics, vmem_limit_bytes=V7X_VMEM_LIMIT)


def _adaln_kernel(ct_ref, w_hbm, b_ref, o_ref, ring, sems):
    rows = ring.shape[1]
    n_chunks = w_hbm.shape[0] // rows

    def chunk_copy(i):
        slot = i % ADALN_BUFS
        return pltpu.make_async_copy(w_hbm.at[i * rows:(i + 1) * rows, :], ring.at[slot], sems.at[slot])

    for i in range(min(ADALN_BUFS, n_chunks)):
        chunk_copy(i).start()
    acts = []
    for b in range(ct_ref.shape[1]):
        cb = ct_ref[:, b:b + 1]
        acts.append(cb * _sigmoid(cb))
    for i in range(n_chunks):
        chunk_copy(i).wait()
        w = ring[i % ADALN_BUFS]
        for b, act in enumerate(acts):
            part = jnp.sum(w * act[i * rows:(i + 1) * rows], axis=0, keepdims=True)
            o_ref[b:b + 1, :] = part + (b_ref[...] if i == 0 else o_ref[b:b + 1, :])
        if i + ADALN_BUFS < n_chunks:
            chunk_copy(i + ADALN_BUFS).start()


def _adaln(c, w, bias):
    B, D = c.shape
    n = w.shape[1]
    assert D % ADALN_ROWS == 0
    return pl.pallas_call(
        _adaln_kernel,
        in_specs=[pl.BlockSpec(memory_space=pltpu.VMEM),
                  pl.BlockSpec(memory_space=pl.ANY),
                  pl.BlockSpec(memory_space=pltpu.VMEM)],
        out_specs=pl.BlockSpec(memory_space=pltpu.VMEM),
        out_shape=jax.ShapeDtypeStruct((B, n), F32),
        scratch_shapes=[pltpu.VMEM((ADALN_BUFS, ADALN_ROWS, n), F32),
                        pltpu.SemaphoreType.DMA((ADALN_BUFS,))],
        compiler_params=pltpu.CompilerParams(vmem_limit_bytes=V7X_VMEM_LIMIT),
        name="adaln",
    )(c.T, w, bias.reshape(1, n))


def _norm_proj_kernel(x_ref, shift_ref, scale_ref, g_ref, ws_ref, h_ref, zs_ref, ws_bf16):
    @pl.when(pl.program_id(0) == 0)
    def _():
        ws_bf16[...] = ws_ref[...].T.astype(BF16)

    rows = x_ref.shape[0] // NORM_ROW_CHUNKS
    for c in range(NORM_ROW_CHUNKS):
        sl = slice(c * rows, (c + 1) * rows)
        x = x_ref[sl, :]
        y = x * lax.rsqrt(jnp.mean(x * x, axis=-1, keepdims=True) + EPS) * g_ref[...]
        h = (y * (1.0 + scale_ref[...]) + shift_ref[...]).astype(BF16)
        h_ref[sl, :] = h
        zs_ref[sl, :] = jnp.dot(h, ws_bf16[...], preferred_element_type=F32)


def _norm_proj(xt, mod4, norm_g, w_in_t, seq):
    T, D = xt.shape
    tm = NORM_TM
    per_b = seq // tm
    return pl.pallas_call(
        _norm_proj_kernel,
        grid=(T // tm,),
        in_specs=[pl.BlockSpec((tm, D), lambda i: (i, 0)),
                  pl.BlockSpec((None, None, 1, D), lambda i: (i // per_b, 0, 0, 0)),
                  pl.BlockSpec((None, None, 1, D), lambda i: (i // per_b, 1, 0, 0)),
                  pl.BlockSpec((1, D), lambda i: (0, 0)),
                  pl.BlockSpec((SMALL_WIDTH, D), lambda i: (0, 0), pipeline_mode=pl.Buffered(1))],
        out_specs=[pl.BlockSpec((tm, D), lambda i: (i, 0)),
                   pl.BlockSpec((tm, SMALL_WIDTH), lambda i: (i, 0))],
        out_shape=[jax.ShapeDtypeStruct((T, D), BF16),
                   jax.ShapeDtypeStruct((T, SMALL_WIDTH), F32)],
        scratch_shapes=[pltpu.VMEM((D, SMALL_WIDTH), BF16)],
        compiler_params=_params(("arbitrary",)),
        name="norm_proj",
    )(xt, mod4, mod4, norm_g.reshape(1, D), w_in_t)


def _gate_proj_kernel(h_ref, w_hbm, o_ref, w_f32, w_bf16, sem, *, tile_kinds, first_row):
    j = pl.program_id(0)
    tn = w_bf16.shape[0]

    def window_copy(tile):
        rows = pl.ds(pl.multiple_of(first_row + tile * tn, F32_SUBLANES), tn)
        return pltpu.make_async_copy(w_hbm.at[rows, :], w_f32, sem)

    @pl.when(pl.program_id(1) == 0)
    def _():
        @pl.when(j == 0)
        def _():
            window_copy(0).start()

        window_copy(j).wait()
        w_bf16[...] = w_f32[...].astype(BF16)

        @pl.when(j + 1 < pl.num_programs(0))
        def _():
            window_copy(j + 1).start()

    def branch_of(kinds):
        return tuple("sigmoid" if k == "sigmoid" else "silu" for k in kinds)

    for branch in sorted(set(map(branch_of, tile_kinds))):
        tiles = [t for t, k in enumerate(tile_kinds) if branch_of(k) == branch]
        cond = functools.reduce(jnp.logical_or, [j == t for t in tiles])

        @pl.when(cond)
        def _(branch=branch, tiles=tiles):
            acc = lax.dot_general(h_ref[...], w_bf16[...], NT_DIMS, preferred_element_type=F32)
            width = tn // len(branch)
            for s, kind in enumerate(branch):
                cols = slice(s * width, (s + 1) * width)
                a = acc[:, cols]
                sg = _sigmoid(a)
                identity_tiles = [t for t in tiles if tile_kinds[t][s] == "linear"]
                if kind == "sigmoid":
                    out = sg
                elif identity_tiles:
                    is_identity = functools.reduce(jnp.logical_or, [j == t for t in identity_tiles])
                    out = a * jnp.where(is_identity, 1.0, sg)
                else:
                    out = a * sg
                o_ref[:, cols] = out.astype(o_ref.dtype)


def _gate_proj(h, w_in_t):
    T, D = h.shape
    tm, tn = GATE_TM, GATE_TN
    first_row = Q_LORA + KV_LORA + QK_ROPE
    n = w_in_t.shape[0] - first_row
    assert n % tn == 0 and first_row % F32_SUBLANES == 0 and tn % F32_SUBLANES == 0
    group_kinds = (("silu",) * (MLA_WIDTH // POOL_WIDTH) + ("linear", "silu")
                   + ("sigmoid",) * (2 * D // POOL_WIDTH))
    per_tile = tn // POOL_WIDTH
    tile_kinds = tuple(group_kinds[t * per_tile:(t + 1) * per_tile] for t in range(n // tn))
    kern = functools.partial(_gate_proj_kernel, tile_kinds=tile_kinds, first_row=first_row)
    return pl.pallas_call(
        kern,
        grid=(n // tn, T // tm),
        in_specs=[pl.BlockSpec((tm, D), lambda j, i: (i, 0)),
                  pl.BlockSpec(memory_space=pl.ANY)],
        out_specs=pl.BlockSpec((tm, tn), lambda j, i: (i, j)),
        out_shape=jax.ShapeDtypeStruct((T, n), BF16),
        scratch_shapes=[pltpu.VMEM((tn, D), F32), pltpu.VMEM((tn, D), BF16),
                        pltpu.SemaphoreType.DMA(())],
        compiler_params=_params(("arbitrary", "arbitrary")),
        name="gate_proj",
    )(h, w_in_t)


def _mla_prep_kernel(zs_ref, posr_ref, invc_ref, qg_ref, kvg_ref, wuq_ref, wukv_ref,
                     qt_ref, k_ref, vt_ref, wq_ref, wk_ref, wv_ref, *, q_scale):
    @pl.when(pl.program_id(0) == 0)
    def _():
        wq_ref[...] = wuq_ref[...].T.astype(BF16)
        kv_head = QK_NOPE + V_HEAD
        for h in range(N_HEADS):
            wk_ref[:, h * QK_NOPE:(h + 1) * QK_NOPE] = (
                wukv_ref[:, h * kv_head:h * kv_head + QK_NOPE].astype(BF16))
            wv_ref[h * V_HEAD:(h + 1) * V_HEAD, :] = (
                wukv_ref[:, h * kv_head + QK_NOPE:(h + 1) * kv_head].T.astype(BF16))

    def rms(v, g):
        return (v * lax.rsqrt(jnp.mean(v * v, axis=-1, keepdims=True) + EPS) * g).astype(BF16)

    cqn = rms(zs_ref[:, 0:Q_LORA], qg_ref[...])
    ckvn = rms(zs_ref[:, Q_LORA:Q_LORA + KV_LORA], kvg_ref[...])
    kr = zs_ref[:, Q_LORA + KV_LORA:Q_LORA + KV_LORA + QK_ROPE]
    kr_sw = jnp.concatenate([kr[:, QK_ROPE // 2:], kr[:, :QK_ROPE // 2]], axis=1)

    ang_t = invc_ref[...] * posr_ref[...].astype(F32)
    cos_t, sin_t = jnp.cos(ang_t), jnp.sin(ang_t)
    cos, sin = cos_t.T, sin_t.T

    qf = lax.dot_general(wq_ref[...], cqn, NT_DIMS, preferred_element_type=F32)
    half = QK_ROPE // 2
    for h in range(N_HEADS):
        r0 = h * QK_HEAD + QK_NOPE
        qt_ref[h * QK_HEAD:r0, :] = (qf[h * QK_HEAD:r0] * q_scale).astype(BF16)
        rope = qf[r0:r0 + QK_ROPE]
        rope_sw = jnp.concatenate([rope[half:], rope[:half]], axis=0)
        qt_ref[r0:r0 + QK_ROPE, :] = ((rope * cos_t + rope_sw * sin_t) * q_scale).astype(BF16)

    kn = jnp.dot(ckvn, wk_ref[...], preferred_element_type=F32)
    k_rot = (kr * cos + kr_sw * sin).astype(BF16)
    for h in range(N_HEADS):
        k_ref[h, :, 0:QK_NOPE] = kn[:, h * QK_NOPE:(h + 1) * QK_NOPE].astype(BF16)
        k_ref[h, :, QK_NOPE:QK_HEAD] = k_rot

    vt_ref[...] = lax.dot_general(wv_ref[...], ckvn, NT_DIMS,
                                  preferred_element_type=F32).astype(BF16)


def _mla_prep(zs, posr, inv_signed, q_norm_g, kv_norm_g, w_uq, w_ukv, batch, seq, q_scale):
    tm = PREP_TM
    per_b = seq // tm
    const = lambda i: (0, 0)
    resident = functools.partial(pl.BlockSpec, pipeline_mode=pl.Buffered(1))
    kern = functools.partial(_mla_prep_kernel, q_scale=q_scale)
    return pl.pallas_call(
        kern,
        grid=(batch * per_b,),
        in_specs=[pl.BlockSpec((tm, SMALL_WIDTH), lambda i: (i, 0)),
                  pl.BlockSpec((None, 1, tm), lambda i: (i // per_b, 0, i % per_b)),
                  pl.BlockSpec((QK_ROPE, 1), const),
                  pl.BlockSpec((1, Q_LORA), const),
                  pl.BlockSpec((1, KV_LORA), const),
                  resident(w_uq.shape, const),
                  resident(w_ukv.shape, const)],
        out_specs=[pl.BlockSpec((None, N_HEADS * QK_HEAD, tm), lambda i: (i // per_b, 0, i % per_b)),
                   pl.BlockSpec((None, N_HEADS, tm, QK_HEAD), lambda i: (i // per_b, 0, i % per_b, 0)),
                   pl.BlockSpec((None, MLA_WIDTH, tm), lambda i: (i // per_b, 0, i % per_b))],
        out_shape=[jax.ShapeDtypeStruct((batch, N_HEADS * QK_HEAD, seq), BF16),
                   jax.ShapeDtypeStruct((batch, N_HEADS, seq, QK_HEAD), BF16),
                   jax.ShapeDtypeStruct((batch, MLA_WIDTH, seq), BF16)],
        scratch_shapes=[pltpu.VMEM((N_HEADS * QK_HEAD, Q_LORA), BF16),
                        pltpu.VMEM((KV_LORA, N_HEADS * QK_NOPE), BF16),
                        pltpu.VMEM((MLA_WIDTH, KV_LORA), BF16)],
        compiler_params=_params(("arbitrary",)),
        name="mla_prep",
    )(zs, posr, inv_signed.reshape(QK_ROPE, 1),
      q_norm_g.reshape(1, Q_LORA), kv_norm_g.reshape(1, KV_LORA), w_uq, w_ukv)


def _attention_kernel(qt_ref, k_ref, knext_ref, vt_ref, *refs, n_cast):
    cast_in, (o_ref, *cast_out) = refs[:n_cast], refs[n_cast:2 * n_cast + 1]
    lmin_ref, ksq_ref = refs[2 * n_cast + 1:]
    for w_in, w_out in zip(cast_in, cast_out):
        w_out[...] = w_in[...].astype(w_out.dtype)

    def max_sq_norm(rows):
        rf = rows.astype(F32)
        return jnp.max(jnp.sum(rf * rf, axis=1, keepdims=True), axis=0, keepdims=True)

    tq = lmin_ref.shape[1]
    nq = qt_ref.shape[1] // tq

    @pl.when(jnp.logical_and(pl.program_id(0) == 0, pl.program_id(1) == 0))
    def _():
        ksq_ref[...] = max_sq_norm(k_ref[...])

    k_norm = jnp.sqrt(ksq_ref[...])
    lmin_ref[...] = jnp.full(lmin_ref.shape, jnp.inf, F32)

    def fast_tile(i, next_ksq):
        off = pl.multiple_of(i * tq, tq)
        next_ksq = jnp.maximum(next_ksq, max_sq_norm(knext_ref[pl.ds(off, tq), :]))
        qt = qt_ref[:, pl.ds(off, tq)]
        qf = qt.astype(F32)
        shift = jnp.sqrt(jnp.sum(qf * qf, axis=0, keepdims=True)) * k_norm
        s = jnp.dot(k_ref[...], qt, preferred_element_type=F32)
        p = jnp.exp2(s - shift)
        l = jnp.sum(p, axis=0, keepdims=True)
        ot = jnp.dot(vt_ref[...], p.astype(BF16), preferred_element_type=F32)
        o_ref[pl.ds(off, tq), :] = (ot / l).T.astype(o_ref.dtype)
        lmin_ref[...] = jnp.minimum(lmin_ref[...], l)
        return next_ksq

    ksq_ref[...] = lax.fori_loop(0, nq, fast_tile, jnp.zeros((1, 1), F32), unroll=True)
    trusted = jnp.min(lmin_ref[...]) >= MIN_SOFTMAX_MASS

    @pl.when(jnp.logical_not(trusted))
    def _():
        _attention_exact(qt_ref, k_ref, vt_ref, o_ref)


def _attention_exact(qt_ref, k_ref, vt_ref, o_ref):
    tq = ATTN_TQ_EXACT
    nq = qt_ref.shape[1] // tq

    def tile(i, carry):
        off = pl.multiple_of(i * tq, tq)
        s = jnp.dot(k_ref[...], qt_ref[:, pl.ds(off, tq)], preferred_element_type=F32)
        p = jnp.exp2(s - jnp.max(s, axis=0, keepdims=True))
        l = jnp.sum(p, axis=0, keepdims=True)
        ot = jnp.dot(vt_ref[...], p.astype(BF16), preferred_element_type=F32)
        o_ref[pl.ds(off, tq), :] = (ot / l).T.astype(o_ref.dtype)
        return carry

    lax.fori_loop(0, nq, tile, 0)


def _attention(qt, k, vt, batch, seq, later_weights):
    tq_fast = ATTN_TQ_FAST
    assert seq % ATTN_TQ_EXACT == 0 and seq % tq_fast == 0
    steps = batch * N_HEADS
    slabs = [w.shape[0] // steps for w in later_weights]
    assert all(s % BF16_SUBLANES == 0 and s * steps == w.shape[0]
               for s, w in zip(slabs, later_weights))
    step = lambda b, h: (b * N_HEADS + h, 0)
    cast_specs = [pl.BlockSpec((s, w.shape[1]), step) for s, w in zip(slabs, later_weights)]

    def next_head(b, h):
        t = jnp.minimum(b * N_HEADS + h + 1, steps - 1)
        return t // N_HEADS, t % N_HEADS, 0, 0

    outs = pl.pallas_call(
        functools.partial(_attention_kernel, n_cast=len(later_weights)),
        grid=(batch, N_HEADS),
        in_specs=[pl.BlockSpec((None, QK_HEAD, seq), lambda b, h: (b, h, 0)),
                  pl.BlockSpec((None, None, seq, QK_HEAD), lambda b, h: (b, h, 0, 0)),
                  pl.BlockSpec((None, None, seq, QK_HEAD), next_head),
                  pl.BlockSpec((None, V_HEAD, seq), lambda b, h: (b, h, 0))] + cast_specs,
        out_specs=[pl.BlockSpec((None, seq, V_HEAD), lambda b, h: (b, 0, h))] + cast_specs,
        out_shape=[jax.ShapeDtypeStruct((batch, seq, MLA_WIDTH), BF16)]
                  + [jax.ShapeDtypeStruct(w.shape, BF16) for w in later_weights],
        scratch_shapes=[pltpu.VMEM((1, tq_fast), F32), pltpu.VMEM((1, 1), F32)],
        compiler_params=_params(("arbitrary", "arbitrary")),
        name="attention",
    )(qt, k, k, vt, *later_weights)
    return outs[0], outs[1:]


def _merge_out_kernel(x_ref, gate_ref, attn_ref, gm_ref, gp_ref, vp_ref, vprev_ref, vnext_ref,
                      mm_ref, mp_ref, pw_ref, ps_ref, wop_ref, wom_ref, wout_ref, fg_ref, o_ref,
                      *, seq, final_norm):
    tm = x_ref.shape[0]
    t0 = (pl.program_id(0) % (seq // tm)) * tm

    gated = attn_ref[...] * gm_ref[...]

    cur = vp_ref[...].astype(F32)
    prev = jnp.where(t0 > 0, vprev_ref[...].astype(F32), 0.0)
    nxt = jnp.where(t0 + tm < seq, vnext_ref[...].astype(F32), 0.0)
    ext = jnp.concatenate([prev, cur, nxt], axis=0)
    n_ext = tm + 2 * POOL_HALO
    tok = t0 + lax.broadcasted_iota(jnp.int32, (tm, 1), 0)
    mixed = []
    p_mla = []
    mla_cols = wom_ref.shape[1] // POOL_GROUPS
    for g, w in enumerate(POOL_WINDOWS):
        p_mla.append(jnp.dot(gated, wom_ref[:, g * mla_cols:(g + 1) * mla_cols],
                             preferred_element_type=F32))
        acc = ext[:, g * POOL_GROUP_DIM:(g + 1) * POOL_GROUP_DIM]
        acc = acc + pltpu.roll(acc, 1, axis=0)
        half = 1
        while 2 * half < w:
            acc = pltpu.roll(acc, half, axis=0) + pltpu.roll(acc, n_ext - half, axis=0)
            half *= 2
        wsum = acc[POOL_HALO:POOL_HALO + tm]
        count = (jnp.minimum(tok + w // 2, seq) - jnp.maximum(tok - w // 2, 0)).astype(F32)
        pooled = wsum / count - cur[:, g * POOL_GROUP_DIM:(g + 1) * POOL_GROUP_DIM]
        mixed.append(jnp.dot(pooled.astype(BF16), pw_ref[g], preferred_element_type=F32))
    mixed = jnp.concatenate(mixed, axis=1)
    p_mla = jnp.concatenate(p_mla, axis=1)
    u = (mixed * ps_ref[...] * gp_ref[...].astype(F32)).astype(BF16)
    p_pool = jnp.dot(u, wop_ref[...], preferred_element_type=F32)

    y = mm_ref[...].astype(F32) * p_mla + mp_ref[...].astype(F32) * p_pool
    r = jnp.dot(y.astype(BF16), wout_ref[...], preferred_element_type=F32)
    xo = x_ref[...] + gate_ref[...] * r
    if final_norm:
        xo = xo * lax.rsqrt(jnp.mean(xo * xo, axis=-1, keepdims=True) + EPS) * fg_ref[...]
    o_ref[...] = xo


def _merge_out(xt, mod4, attn, zbig, pool_w, pool_scale, w_o_pool, w_o_mla, w_out, final_g,
               seq, final_norm):
    T, D = xt.shape
    tm = MERGE_TM
    per_b = seq // tm
    halo_per_tile = tm // POOL_HALO
    n_halo = T // POOL_HALO
    vp_blk = MLA_WIDTH // POOL_WIDTH
    gp_blk = vp_blk + 1
    mm_blk = (MLA_WIDTH + 2 * POOL_WIDTH) // D
    resident = functools.partial(pl.BlockSpec, pipeline_mode=pl.Buffered(1))
    kern = functools.partial(_merge_out_kernel, seq=seq, final_norm=final_norm)
    return pl.pallas_call(
        kern,
        grid=(T // tm,),
        in_specs=[pl.BlockSpec((tm, D), lambda i: (i, 0)),
                  pl.BlockSpec((None, None, 1, D), lambda i: (i // per_b, 2, 0, 0)),
                  pl.BlockSpec((tm, MLA_WIDTH), lambda i: (i, 0)),
                  pl.BlockSpec((tm, MLA_WIDTH), lambda i: (i, 0)),
                  pl.BlockSpec((tm, POOL_WIDTH), lambda i: (i, gp_blk)),
                  pl.BlockSpec((tm, POOL_WIDTH), lambda i: (i, vp_blk)),
                  pl.BlockSpec((POOL_HALO, POOL_WIDTH),
                               lambda i: (jnp.maximum(i * halo_per_tile - 1, 0), vp_blk)),
                  pl.BlockSpec((POOL_HALO, POOL_WIDTH),
                               lambda i: (jnp.minimum((i + 1) * halo_per_tile, n_halo - 1), vp_blk)),
                  pl.BlockSpec((tm, D), lambda i: (i, mm_blk)),
                  pl.BlockSpec((tm, D), lambda i: (i, mm_blk + 1)),
                  resident(pool_w.shape, lambda i: (0, 0, 0)),
                  resident((1, POOL_WIDTH), lambda i: (0, 0)),
                  resident(w_o_pool.shape, lambda i: (0, 0)),
                  resident(w_o_mla.shape, lambda i: (0, 0)),
                  resident(w_out.shape, lambda i: (0, 0)),
                  resident((1, D), lambda i: (0, 0))],
        out_specs=pl.BlockSpec((tm, D), lambda i: (i, 0)),
        out_shape=jax.ShapeDtypeStruct((T, D), F32),
        compiler_params=_params(("arbitrary",)),
        name="merge_out",
    )(xt, mod4, attn, zbig, zbig, zbig, zbig, zbig, zbig, zbig,
      pool_w, pool_scale.reshape(1, POOL_WIDTH), w_o_pool, w_o_mla, w_out, final_g.reshape(1, D))


def kernel(x, c, positions, ada_w, ada_b, norm_g, w_in, q_norm_g, w_uq, kv_norm_g, w_ukv, w_o_mla,
           pool_w, pool_scale, w_o_pool, w_out, final_g):
    B, S, D = x.shape
    depth = ada_w.shape[0]
    inv_freq = 1.0 / (ROPE_THETA ** (jnp.arange(0, QK_ROPE, 2, dtype=F32) / QK_ROPE))
    inv_signed = jnp.concatenate([-inv_freq, inv_freq])
    posr = positions.reshape(B, 1, S)
    q_scale = QK_HEAD ** -0.5 * math.log2(math.e)

    xt = x.reshape(B * S, D)
    for l in range(depth):
        mod4 = _adaln(c, ada_w[l], ada_b[l]).reshape(B, 3, 1, D)
        w_in_t = w_in[l].T
        h, zs = _norm_proj(xt, mod4, norm_g[l], w_in_t, S)
        zbig = _gate_proj(h, w_in_t)
        qt, k, vt = _mla_prep(zs, posr, inv_signed, q_norm_g[l], kv_norm_g[l],
                              w_uq[l], w_ukv[l], B, S, q_scale)
        attn, (pw, wop, wom, wout) = _attention(
            qt, k, vt, B, S,
            (pool_w[l].reshape(POOL_WIDTH, POOL_GROUP_DIM), w_o_pool[l], w_o_mla[l], w_out[l]))
        xt = _merge_out(xt, mod4, attn.reshape(B * S, MLA_WIDTH), zbig,
                        pw.reshape(POOL_GROUPS, POOL_GROUP_DIM, POOL_GROUP_DIM), pool_scale[l],
                        wop, wom, wout, final_g, S, final_norm=(l == depth - 1))
    return xt.reshape(B, S, D)
```
